```python
import jax, jax.numpy as jnp
from jax import lax
import numpy as np

D_MODEL = 2048
BATCH = 4
SEQ = 2048
DEPTH = 2

N_META = 16
GRID_W = 64
GLA_HEADS = 4
GLA_DK = 128
GLA_DV = 256
GLA_RANK = 16
GLA_TAU = 16.0
GLA_CHUNK = 64
NA_HEADS = 16
NA_DH = 64
NA_WIN_H = 8
NA_WIN_W = 16
NA_QB = 16
NA_KB = 32
D_FF = 4 * D_MODEL
DEEPNORM_ALPHA = (2 * DEPTH) ** 0.25
DEEPNORM_BETA = (8 * DEPTH) ** -0.25
LN_EPS = 1e-5

MIX_WIDTH = GLA_HEADS * GLA_DV + NA_HEADS * NA_DH
PROJ_SIZES = (GLA_HEADS * GLA_DK, GLA_HEADS * GLA_DK, GLA_HEADS * GLA_DV, GLA_HEADS * GLA_DV,
              2 * GLA_RANK, NA_HEADS * NA_DH, NA_HEADS * NA_DH, NA_HEADS * NA_DH)

kernel_name = "hybrid_gla_natten_deepnorm_encoder"


def _layer_norm(x, w, b):
    xf = x.astype(jnp.float32)
    mu = jnp.mean(xf, axis=-1, keepdims=True)
    var = jnp.mean(jnp.square(xf - mu), axis=-1, keepdims=True)
    y = (xf - mu) * lax.rsqrt(var + LN_EPS) * w.astype(jnp.float32) + b.astype(jnp.float32)
    return y.astype(x.dtype)


def _to_chunks(t, n_heads):
    B, Lp, W = t.shape
    return t.reshape(B, Lp // GLA_CHUNK, GLA_CHUNK, n_heads, W // n_heads).transpose(1, 0, 3, 2, 4)


def _from_chunks(t):
    N, B, H, C, d = t.shape
    return t.transpose(1, 0, 3, 2, 4).reshape(B, N * C, H * d)


def _gla_scan(q, k, v, g):
    qc, kc, vc, gc = (_to_chunks(t, GLA_HEADS) for t in (q, k, v, g))
    B = q.shape[0]
    C = GLA_CHUNK
    incl = jnp.tril(jnp.ones((C, C), dtype=bool))[:, :, None]

    def step(S, inp):
        qi, ki, vi, gi = inp
        bcum = jnp.cumsum(gi, axis=-2)
        b_last = bcum[..., -1:, :]
        inter = jnp.einsum('bhtd,bhde->bhte', qi * jnp.exp(bcum), S)
        diff = bcum[..., :, None, :] - bcum[..., None, :, :]
        decay = jnp.exp(jnp.where(incl, diff, -jnp.inf))
        scores = jnp.einsum('bhtd,bhsd,bhtsd->bhts', qi, ki, decay)
        intra = jnp.einsum('bhts,bhse->bhte', scores, vi)
        S = (jnp.exp(b_last[..., 0, :])[..., None] * S
             + jnp.einsum('bhsd,bhse->bhde', ki * jnp.exp(b_last - bcum), vi))
        return S, inter + intra

    S0 = jnp.zeros((B, GLA_HEADS, GLA_DK, GLA_DV), jnp.float32)
    _, out = lax.scan(step, S0, (qc, kc, vc, gc))
    return _from_chunks(out)


def _gla_mixer(q, k, v, r, gate_lr, w_up, b_up, norm_w):
    B, L, _ = q.shape
    f32 = jnp.float32
    pad = GLA_CHUNK - N_META
    lr = gate_lr.astype(f32).reshape(B, L, 2, GLA_RANK)
    g = jax.nn.log_sigmoid(jnp.einsum('blzr,zrk->blzk', lr, w_up.astype(f32))
                           + b_up.astype(f32)) / GLA_TAU
    padf = lambda t: jnp.pad(t, ((0, 0), (pad, 0), (0, 0)))
    qp = padf(q.astype(f32) * GLA_DK ** -0.5)
    kp = padf(k.astype(f32))
    vp = padf(v.astype(f32))
    g_fwd = padf(g[:, :, 0])
    g_bwd = padf(g[:, :, 1])
    flip = lambda t: jnp.flip(t, axis=1)
    o_fwd = _gla_scan(qp, kp, vp, g_fwd)
    o_bwd = flip(_gla_scan(flip(qp), flip(kp), flip(vp), flip(g_bwd)))
    o = (o_fwd + o_bwd)[:, pad:].reshape(B, L, GLA_HEADS, GLA_DV)
    o = o * lax.rsqrt(jnp.mean(jnp.square(o), axis=-1, keepdims=True) + LN_EPS) * norm_w.astype(f32)
    o = o.reshape(B, L, GLA_HEADS * GLA_DV) * jax.nn.silu(r.astype(f32))
    return o.astype(q.dtype)


def _na_mixer(q, k, v, rel_bias):
    B, L, _ = q.shape
    T = L - N_META
    rows = T // GRID_W
    kh = min(NA_WIN_H, rows)
    ncb = GRID_W // NA_QB
    f32 = jnp.float32
    split = lambda t: t.astype(f32).reshape(B, L, NA_HEADS, NA_DH).transpose(0, 2, 1, 3)
    q, k, v = split(q) * NA_DH ** -0.5, split(k), split(v)
    qm, km, vm = q[:, :, :N_META], k[:, :, :N_META], v[:, :, :N_META]
    meta_out = jnp.einsum('bhqk,bhkd->bhqd',
                          jax.nn.softmax(jnp.einsum('bhqd,bhkd->bhqk', qm, km), axis=-1), vm)
    qg = q[:, :, N_META:].reshape(B, NA_HEADS, rows, GRID_W, NA_DH)
    kg = k[:, :, N_META:].reshape(B, NA_HEADS, rows, GRID_W, NA_DH)
    vg = v[:, :, N_META:].reshape(B, NA_HEADS, rows, GRID_W, NA_DH)

    col = np.arange(GRID_W)
    cs = np.clip(col - NA_WIN_W // 2, 0, GRID_W - NA_WIN_W).reshape(ncb, NA_QB)
    q_col = col.reshape(ncb, NA_QB)
    cb = np.clip(np.arange(ncb) * NA_QB - NA_WIN_W // 2, 0, GRID_W - NA_KB)
    key_col = cb[:, None] + np.arange(NA_KB)
    col_mask = ((key_col[:, None, :] >= cs[:, :, None])
                & (key_col[:, None, :] < cs[:, :, None] + NA_WIN_W))
    dc_idx = np.clip(key_col[:, None, :] - q_col[:, :, None],
                     -(NA_WIN_W - 1), NA_WIN_W - 1) + NA_WIN_W - 1
    bias_tab = rel_bias.astype(f32)
    n_loc = kh * NA_KB

    def row_attn(args):
        r, q_row = args
        rs = jnp.clip(r - kh // 2, 0, rows - kh)
        k_blk = lax.dynamic_slice_in_dim(kg, rs, kh, axis=2)[:, :, :, key_col, :]
        v_blk = lax.dynamic_slice_in_dim(vg, rs, kh, axis=2)[:, :, :, key_col, :]
        qb = q_row.reshape(B, NA_HEADS, ncb, NA_QB, NA_DH)
        s_loc = jnp.einsum('bhnqd,bhrnkd->bhnqrk', qb, k_blk)
        dr_idx = rs + jnp.arange(kh) - r + NA_WIN_H - 1
        bias = bias_tab[:, dr_idx[:, None, None, None], dc_idx[None]]
        bias = jnp.transpose(bias, (0, 2, 3, 1, 4))
        s_loc = jnp.where(col_mask[:, :, None, :], s_loc + bias, -jnp.inf)
        s_meta = jnp.einsum('bhnqd,bhmd->bhnqm', qb, km)
        s = jnp.concatenate([s_loc.reshape(B, NA_HEADS, ncb, NA_QB, n_loc), s_meta], axis=-1)
        p = jax.nn.softmax(s, axis=-1)
        p_loc = p[..., :n_loc].reshape(B, NA_HEADS, ncb, NA_QB, kh, NA_KB)
        out = (jnp.einsum('bhnqrk,bhrnkd->bhnqd', p_loc, v_blk)
               + jnp.einsum('bhnqm,bhmd->bhnqd', p[..., n_loc:], vm))
        return out.reshape(B, NA_HEADS, GRID_W, NA_DH)

    grid_out = lax.map(row_attn, (jnp.arange(rows), jnp.moveaxis(qg, 2, 0)))
    grid_out = jnp.moveaxis(grid_out, 0, 2).reshape(B, NA_HEADS, T, NA_DH)
    out = jnp.concatenate([meta_out, grid_out], axis=2)
    return out.transpose(0, 2, 1, 3).reshape(B, L, NA_HEADS * NA_DH)


def _layer(h, w_in, gla_w_up, gla_b_up, gla_norm_w, na_rel_bias, w_out,
           ln1_w, ln1_b, w_ff1, w_ff2, ln2_w, ln2_b):
    proj = jnp.einsum('bld,dp->blp', h, w_in)
    points = [int(p) for p in np.cumsum(PROJ_SIZES)[:-1]]
    gq, gk, gv, gr, glr, nq, nk, nv = jnp.split(proj, points, axis=-1)
    y_gla = _gla_mixer(gq, gk, gv, gr, glr, gla_w_up, gla_b_up, gla_norm_w)
    y_na = _na_mixer(nq, nk, nv, na_rel_bias).astype(h.dtype)
    mix = jnp.einsum('blm,md->bld', jnp.concatenate([y_gla, y_na], axis=-1), w_out)
    h = _layer_norm(DEEPNORM_ALPHA * h + mix, ln1_w, ln1_b)
    ff = jnp.einsum('blf,fd->bld', jnp.square(jax.nn.relu(jnp.einsum('bld,df->blf', h, w_ff1))), w_ff2)
    return _layer_norm(DEEPNORM_ALPHA * h + ff, ln2_w, ln2_b)


def setup_inputs(seed: int = 0) -> dict:
    key = jax.random.key(seed)
    ks = jax.random.split(key, 14)
    nrm = jax.random.normal
    f32 = jnp.float32
    p_total = sum(PROJ_SIZES)
    return {
        "x": nrm(ks[0], (BATCH, SEQ, D_MODEL), f32),
        "meta": nrm(ks[1], (N_META, D_MODEL), f32),
        "w_in": nrm(ks[2], (DEPTH, D_MODEL, p_total), f32) * D_MODEL ** -0.5,
        "gla_w_up": nrm(ks[3], (DEPTH, 2, GLA_RANK, GLA_HEADS * GLA_DK), f32) * GLA_RANK ** -0.5,
        "gla_b_up": 0.1 * nrm(ks[4], (DEPTH, 2, GLA_HEADS * GLA_DK), f32),
        "gla_norm_w": 1.0 + 0.02 * nrm(ks[5], (DEPTH, GLA_DV), f32),
        "na_rel_bias": 0.1 * nrm(ks[6], (DEPTH, NA_HEADS, 2 * NA_WIN_H - 1, 2 * NA_WIN_W - 1), f32),
        "w_out": nrm(ks[7], (DEPTH, MIX_WIDTH, D_MODEL), f32) * (MIX_WIDTH ** -0.5 * DEEPNORM_BETA),
        "ln1_w": 1.0 + 0.02 * nrm(ks[8], (DEPTH, D_MODEL), f32),
        "ln1_b": 0.02 * nrm(ks[9], (DEPTH, D_MODEL), f32),
        "w_ff1": nrm(ks[10], (DEPTH, D_MODEL, D_FF), f32) * D_MODEL ** -0.5,
        "w_ff2": nrm(ks[11], (DEPTH, D_FF, D_MODEL), f32) * (D_FF ** -0.5 * DEEPNORM_BETA),
        "ln2_w": 1.0 + 0.02 * nrm(ks[12], (DEPTH, D_MODEL), f32),
        "ln2_b": 0.02 * nrm(ks[13], (DEPTH, D_MODEL), f32),
    }


def reference(x, meta, w_in, gla_w_up, gla_b_up, gla_norm_w, na_rel_bias, w_out,
              ln1_w, ln1_b, w_ff1, w_ff2, ln2_w, ln2_b):
    B = x.shape[0]
    meta_b = jnp.broadcast_to(meta.astype(x.dtype)[None], (B, N_META, x.shape[-1]))
    h = jnp.concatenate([meta_b, x], axis=1)
    for l in range(DEPTH):
        h = _layer(h, w_in[l], gla_w_up[l], gla_b_up[l], gla_norm_w[l], na_rel_bias[l], w_out[l],
                   ln1_w[l], ln1_b[l], w_ff1[l], w_ff2[l], ln2_w[l], ln2_b[l])
    return h[:, N_META:]
```

```python
import functools

import numpy as np
import jax
import jax.numpy as jnp
from jax import lax
from jax.experimental import pallas as pl
from jax.experimental.pallas import tpu as pltpu

N_META = 16
GRID_W = 64
GLA_HEADS = 4
GLA_DK = 128
GLA_DV = 256
GLA_RANK = 16
GLA_TAU = 16.0
GLA_CHUNK = 64
NA_HEADS = 16
NA_DH = 64
NA_WIN_H = 8
NA_WIN_W = 16
NA_HEAD_GROUP = 4
LN_EPS = 1e-5
MASK_VALUE = -1e30

V7X_VMEM_LIMIT_BYTES = 58 * 1024 * 1024

F32 = jnp.float32
BF16 = jnp.bfloat16

_NT = (((1,), (1,)), ((), ()))
_TN = (((0,), (0,)), ((), ()))


def _dot(a, b, dims=None):
    if dims is None:
        return jnp.dot(a, b, preferred_element_type=F32)
    return lax.dot_general(a, b, dims, preferred_element_type=F32)


def _proj_kernel(a_ref, w_ref, o_ref):
    o_ref[...] = _dot(a_ref[...], w_ref[...]).astype(o_ref.dtype)


def _proj(a, w, tm, tn):
    m, k = a.shape
    n = w.shape[1]
    return pl.pallas_call(
        _proj_kernel,
        grid=(n // tn, m // tm),
        in_specs=[pl.BlockSpec((tm, k), lambda j, i: (i, 0)),
                  pl.BlockSpec((k, tn), lambda j, i: (0, j))],
        out_specs=pl.BlockSpec((tm, tn), lambda j, i: (i, j)),
        out_shape=jax.ShapeDtypeStruct((m, n), BF16),
        compiler_params=pltpu.CompilerParams(
            dimension_semantics=("arbitrary", "arbitrary"),
            vmem_limit_bytes=V7X_VMEM_LIMIT_BYTES),
        name="proj",
    )(a, w)


def _gla_chunk(q, k, v, lr, wc, bc, st_ref, fwd):
    c = q.shape[0]
    x = _dot(lr, wc) + bc
    g = (jnp.minimum(x, 0.0) - jnp.log(1.0 + jnp.exp(-jnp.abs(x)))) * (1.0 / GLA_TAU)
    t_idx = lax.broadcasted_iota(jnp.int32, (c, c), 0)
    u_idx = lax.broadcasted_iota(jnp.int32, (c, c), 1)
    tri = (u_idx <= t_idx) if fwd else (u_idx >= t_idx)
    tri_b = jnp.where(tri, 1.0, 0.0).astype(BF16)
    g_hi = g.astype(BF16)
    g_lo = (g - g_hi.astype(F32)).astype(BF16)
    b = _dot(tri_b, g_hi) + _dot(tri_b, g_lo)
    end = c - 1 if fwd else 0
    mid = c // 2 - 1 if fwd else c // 2
    b_end = b[end:end + 1, :]
    b_mid = b[mid:mid + 1, :]
    qi = q.astype(F32) * (GLA_DK ** -0.5) * jnp.exp(b)
    ks = k.astype(F32) * jnp.exp(b_end - b)
    qt = (qi * jnp.exp(-b_mid)).astype(BF16)
    kt = (ks * jnp.exp(b_mid - b_end)).astype(BF16)
    sc = jnp.where(tri, _dot(qt, kt, _NT), 0.0).astype(BF16)
    st = st_ref[...]
    o = _dot(qi.astype(BF16), st.astype(BF16), _NT) + _dot(sc, v)
    st_ref[...] = st * jnp.exp(b_end) + _dot(v, ks.astype(BF16), _TN)
    return o


def _gla_kernel(q_ref, k_ref, v_ref, r_ref, lr_ref, wc_ref, bc_ref, nw_ref, o_ref, of_ref, st_ref,
                *, seq_len):
    c = GLA_CHUNK
    n_chunks = (seq_len - N_META) // c

    def run(start, size, fwd):
        rows = pl.ds(start, size)
        d = 0 if fwd else 1
        return _gla_chunk(q_ref[0, rows, :], k_ref[0, rows, :], v_ref[0, rows, :], lr_ref[0, rows, :],
                          wc_ref[0, d], bc_ref[0, d], st_ref, fwd)

    def finish(start, size, o_bwd):
        rows = pl.ds(start, size)
        o = of_ref[rows, :] + o_bwd
        o = o * lax.rsqrt(jnp.mean(o * o, axis=-1, keepdims=True) + LN_EPS) * nw_ref[...]
        r = r_ref[0, rows, :].astype(F32)
        o_ref[0, rows, :] = (o * (r / (1.0 + jnp.exp(-r)))).astype(o_ref.dtype)

    def chunk_start(i):
        return pl.multiple_of(N_META + i * c, 16)

    st_ref[...] = jnp.zeros_like(st_ref)
    of_ref[0:N_META, :] = run(0, N_META, True)

    def fwd_body(i, carry):
        s = chunk_start(i)
        of_ref[pl.ds(s, c), :] = run(s, c, True)
        return carry

    lax.fori_loop(0, n_chunks, fwd_body, 0)

    st_ref[...] = jnp.zeros_like(st_ref)

    def bwd_body(j, carry):
        s = chunk_start(n_chunks - 1 - j)
        finish(s, c, run(s, c, False))
        return carry

    lax.fori_loop(0, n_chunks, bwd_body, 0)
    finish(0, N_META, run(0, N_META, False))


def _gla(proj, wc, bc, nw):
    bsz, seq_len, _ = proj.shape
    kern = functools.partial(_gla_kernel, seq_len=seq_len)
    qk_blk = lambda off: pl.BlockSpec((1, seq_len, GLA_DK), lambda b, h: (b, 0, off + h))
    vr_blk = lambda off: pl.BlockSpec((1, seq_len, GLA_DV), lambda b, h: (b, 0, off + h))
    return pl.pallas_call(
        kern,
        grid=(bsz, GLA_HEADS),
        in_specs=[qk_blk(_COL_GQ // GLA_DK), qk_blk(_COL_GK // GLA_DK),
                  vr_blk(_COL_GV // GLA_DV), vr_blk(_COL_GR // GLA_DV),
                  pl.BlockSpec((1, seq_len, 128), lambda b, h: (b, 0, _COL_LR // 128)),
                  pl.BlockSpec((1, 2, 128, GLA_DK), lambda b, h: (h, 0, 0, 0)),
                  pl.BlockSpec((1, 2, 1, GLA_DK), lambda b, h: (h, 0, 0, 0)),
                  pl.BlockSpec((1, GLA_DV), lambda b, h: (0, 0))],
        out_specs=pl.BlockSpec((1, seq_len, GLA_DV), lambda b, h: (b, 0, h)),
        out_shape=jax.ShapeDtypeStruct((bsz, seq_len, GLA_HEADS * GLA_DV), BF16),
        scratch_shapes=[pltpu.VMEM((seq_len, GLA_DV), F32), pltpu.VMEM((GLA_DV, GLA_DK), F32)],
        compiler_params=pltpu.CompilerParams(
            dimension_semantics=("arbitrary", "arbitrary"),
            vmem_limit_bytes=V7X_VMEM_LIMIT_BYTES),
        name="gla",
    )(proj, proj, proj, proj, proj, wc, bc, nw)


def _na_kernel(q_ref, k_ref, v_ref, t_ref, o_ref, *, seq_len):
    rows = (seq_len - N_META) // GRID_W
    hw = NA_HEAD_GROUP * NA_DH
    nq = NA_HEAD_GROUP * GRID_W
    n_keys = NA_WIN_H * GRID_W
    meta_blk = 128

    def block_diag(q4):
        n = q4.shape[0]
        qrep = jnp.concatenate([q4] * NA_HEAD_GROUP, axis=0)
        rb = lax.broadcasted_iota(jnp.int32, (NA_HEAD_GROUP * n, hw), 0) // n
        lb = lax.broadcasted_iota(jnp.int32, (NA_HEAD_GROUP * n, hw), 1) // NA_DH
        return jnp.where(rb == lb, qrep, jnp.zeros_like(qrep))

    def pick_diag(o, n):
        lb = lax.broadcasted_iota(jnp.int32, (n, hw), 1) // NA_DH
        out = jnp.zeros((n, hw), F32)
        for h in range(NA_HEAD_GROUP):
            out = jnp.where(lb == h, o[h * n:(h + 1) * n, :], out)
        return out

    k_meta = k_ref[0, 0:meta_blk, :]
    v_meta = v_ref[0, 0:meta_blk, :]

    def meta_scores(qbd):
        s = _dot(qbd, k_meta, _NT)
        lane = lax.broadcasted_iota(jnp.int32, s.shape, 1)
        return jnp.where(lane < N_META, s, MASK_VALUE)

    qm = block_diag(q_ref[0, 0:N_META, :] * (NA_DH ** -0.5))
    sm = meta_scores(qm)
    pm = jnp.exp(sm - jnp.max(sm, axis=-1, keepdims=True))
    om = _dot(pm.astype(BF16), v_meta) / jnp.sum(pm, axis=-1, keepdims=True)
    o_ref[0, 0:N_META, :] = pick_diag(om, N_META).astype(o_ref.dtype)

    def row_body(r, carry):
        rs = jnp.clip(r - NA_WIN_H // 2, 0, rows - NA_WIN_H)
        delta = r - rs
        q_start = pl.multiple_of(N_META + r * GRID_W, 16)
        k_start = pl.multiple_of(N_META + rs * GRID_W, 16)
        qbd = block_diag(q_ref[0, pl.ds(q_start, GRID_W), :] * (NA_DH ** -0.5))
        kw = k_ref[0, pl.ds(k_start, n_keys), :]
        vw = v_ref[0, pl.ds(k_start, n_keys), :]
        sw = _dot(qbd, kw, _NT)
        bias = jnp.concatenate(
            [t_ref[0, 2 * j - delta + NA_WIN_H - 1] for j in range(NA_WIN_H // 2)], axis=-1)
        sw = sw + bias
        smeta = meta_scores(qbd)
        mx = jnp.maximum(jnp.max(sw, axis=-1, keepdims=True), jnp.max(smeta, axis=-1, keepdims=True))
        pw = jnp.exp(sw - mx)
        pmeta = jnp.exp(smeta - mx)
        den = jnp.sum(pw, axis=-1, keepdims=True) + jnp.sum(pmeta, axis=-1, keepdims=True)
        o = (_dot(pw.astype(BF16), vw) + _dot(pmeta.astype(BF16), v_meta)) / den
        o_ref[0, pl.ds(q_start, GRID_W), :] = pick_diag(o, GRID_W).astype(o_ref.dtype)
        return carry

    lax.fori_loop(0, rows, row_body, 0)


def _na_bias_table(rel_bias):
    col = np.arange(GRID_W)
    cs = np.clip(col - NA_WIN_W // 2, 0, GRID_W - NA_WIN_W)
    inside = (col[None, :] >= cs[:, None]) & (col[None, :] < cs[:, None] + NA_WIN_W)
    dc = np.clip(col[None, :] - col[:, None], -(NA_WIN_W - 1), NA_WIN_W - 1) + NA_WIN_W - 1
    t = jnp.where(inside[None, None], rel_bias.astype(F32)[:, :, dc], MASK_VALUE)
    t = jnp.concatenate([t[:, :-1], t[:, 1:]], axis=-1)
    n_dr = t.shape[1]
    t = t.reshape(NA_HEADS // NA_HEAD_GROUP, NA_HEAD_GROUP, n_dr, GRID_W, 2 * GRID_W)
    return t.transpose(0, 2, 1, 3, 4).reshape(NA_HEADS // NA_HEAD_GROUP, n_dr,
                                               NA_HEAD_GROUP * GRID_W, 2 * GRID_W)


def _na(proj, table):
    bsz, seq_len, _ = proj.shape
    hw = NA_HEAD_GROUP * NA_DH
    n_groups = NA_HEADS // NA_HEAD_GROUP
    kern = functools.partial(_na_kernel, seq_len=seq_len)
    blk = lambda off: pl.BlockSpec((1, seq_len, hw), lambda g, b: (b, 0, off + g))
    return pl.pallas_call(
        kern,
        grid=(n_groups, bsz),
        in_specs=[blk(_COL_NQ // hw), blk(_COL_NK // hw), blk(_COL_NV // hw),
                  pl.BlockSpec((1,) + table.shape[1:], lambda g, b: (g, 0, 0, 0))],
        out_specs=pl.BlockSpec((1, seq_len, hw), lambda g, b: (b, 0, g)),
        out_shape=jax.ShapeDtypeStruct((bsz, seq_len, NA_HEADS * NA_DH), BF16),
        compiler_params=pltpu.CompilerParams(
            dimension_semantics=("arbitrary", "arbitrary"),
            vmem_limit_bytes=V7X_VMEM_LIMIT_BYTES),
        name="na",
    )(proj, proj, proj, table)


def _layer_norm(z, w, b):
    mu = jnp.mean(z, axis=-1, keepdims=True)
    zc = z - mu
    var = jnp.mean(zc * zc, axis=-1, keepdims=True)
    return zc * lax.rsqrt(var + LN_EPS) * w + b


def _out_ln_kernel(y_ref, w_ref, h_ref, lw_ref, lb_ref, o_ref, ob_ref, *, alpha):
    z = alpha * h_ref[...] + _dot(y_ref[...], w_ref[...])
    o = _layer_norm(z, lw_ref[...], lb_ref[...])
    o_ref[...] = o
    ob_ref[...] = o.astype(ob_ref.dtype)


def _out_ln(y, w, h, lw, lb, alpha, tm):
    m, k = y.shape
    n = w.shape[1]
    row = lambda width: pl.BlockSpec((tm, width), lambda i: (i, 0))
    full = lambda shape: pl.BlockSpec(shape, lambda i: (0, 0))
    return pl.pallas_call(
        functools.partial(_out_ln_kernel, alpha=alpha),
        grid=(m // tm,),
        in_specs=[row(k), full((k, n)), row(n), full((1, n)), full((1, n))],
        out_specs=[row(n), row(n)],
        out_shape=[jax.ShapeDtypeStruct((m, n), F32), jax.ShapeDtypeStruct((m, n), BF16)],
        compiler_params=pltpu.CompilerParams(
            dimension_semantics=("arbitrary",),
            vmem_limit_bytes=V7X_VMEM_LIMIT_BYTES),
        name="out_ln",
    )(y, w, h, lw, lb)


def _ffn_ln_kernel(a_ref, w1_ref, w2_ref, h_ref, lw_ref, lb_ref, o_ref, ob_ref, *, alpha):
    f = pl.program_id(1)

    @pl.when(f == 0)
    def _():
        o_ref[...] = alpha * h_ref[...]

    u = jnp.square(jnp.maximum(_dot(a_ref[...], w1_ref[...]), 0.0)).astype(BF16)
    o_ref[...] += _dot(u, w2_ref[...])

    @pl.when(f == pl.num_programs(1) - 1)
    def _():
        o = _layer_norm(o_ref[...], lw_ref[...], lb_ref[...])
        o_ref[...] = o
        ob_ref[...] = o.astype(ob_ref.dtype)


def _ffn_ln(a, w1, w2, h, lw, lb, alpha, tm, tf):
    m, d = a.shape
    d_ff = w1.shape[1]
    row = pl.BlockSpec((tm, d), lambda i, f: (i, 0))
    vec = pl.BlockSpec((1, d), lambda i, f: (0, 0))
    return pl.pallas_call(
        functools.partial(_ffn_ln_kernel, alpha=alpha),
        grid=(m // tm, d_ff // tf),
        in_specs=[row, pl.BlockSpec((d, tf), lambda i, f: (0, f)),
                  pl.BlockSpec((tf, d), lambda i, f: (f, 0)), row, vec, vec],
        out_specs=[row, row],
        out_shape=[jax.ShapeDtypeStruct((m, d), F32), jax.ShapeDtypeStruct((m, d), BF16)],
        compiler_params=pltpu.CompilerParams(
            dimension_semantics=("arbitrary", "arbitrary"),
            vmem_limit_bytes=V7X_VMEM_LIMIT_BYTES),
        name="ffn_ln",
    )(a, w1, w2, h, lw, lb)


_GLA_QK = GLA_HEADS * GLA_DK
_GLA_VR = GLA_HEADS * GLA_DV
_NA_W = NA_HEADS * NA_DH
_COL_GQ = 0
_COL_GK = _COL_GQ + _GLA_QK
_COL_GV = _COL_GK + _GLA_QK
_COL_GR = _COL_GV + _GLA_VR
_COL_NQ = _COL_GR + _GLA_VR
_COL_NK = _COL_NQ + _NA_W
_COL_NV = _COL_NK + _NA_W
_COL_LR = _COL_NV + _NA_W
_PROJ_TN = 1280
_PROJ_COLS = -(-(_COL_LR + 2 * GLA_RANK) // _PROJ_TN) * _PROJ_TN
_ROW_TILE = 688
_OUT_ROW_TILE = 344
_FFN_TF = 512


def _fused_w_in(w_in):
    d = w_in.shape[0]
    lr0 = _COL_GR + _GLA_VR
    lr1 = lr0 + 2 * GLA_RANK
    pad = jnp.zeros((d, _PROJ_COLS - _COL_LR - 2 * GLA_RANK), w_in.dtype)
    return jnp.concatenate([w_in[:, :lr0], w_in[:, lr1:], w_in[:, lr0:lr1], pad], axis=1).astype(BF16)


def _gla_gate_params(w_up, b_up):
    wc = jnp.zeros((GLA_HEADS, 2, 128, GLA_DK), F32)
    for z in range(2):
        w = w_up[z].astype(F32).reshape(GLA_RANK, GLA_HEADS, GLA_DK).transpose(1, 0, 2)
        wc = wc.at[:, z, z * GLA_RANK:(z + 1) * GLA_RANK, :].set(w)
    bc = b_up.astype(F32).reshape(2, GLA_HEADS, 1, GLA_DK).transpose(1, 0, 2, 3)
    return wc.astype(BF16), bc


def kernel(x, meta, w_in, gla_w_up, gla_b_up, gla_norm_w, na_rel_bias, w_out, ln1_w, ln1_b,
           w_ff1, w_ff2, ln2_w, ln2_b):
    bsz, seq, d = x.shape
    depth = w_in.shape[0]
    alpha = (2 * depth) ** 0.25
    seq_len = seq + N_META
    m = bsz * seq_len
    meta_b = jnp.broadcast_to(meta.astype(x.dtype)[None], (bsz, N_META, d))
    h = jnp.concatenate([meta_b, x], axis=1).reshape(m, d).astype(F32)
    hb = h.astype(BF16)
    for l in range(depth):
        proj = _proj(hb, _fused_w_in(w_in[l]), _ROW_TILE, _PROJ_TN).reshape(bsz, seq_len, _PROJ_COLS)
        wc, bc = _gla_gate_params(gla_w_up[l], gla_b_up[l])
        y_gla = _gla(proj, wc, bc, gla_norm_w[l].astype(F32).reshape(1, GLA_DV))
        y_na = _na(proj, _na_bias_table(na_rel_bias[l]))
        y = jnp.concatenate([y_gla, y_na], axis=-1).reshape(m, d)
        h, hb = _out_ln(y, w_out[l].astype(BF16), h, ln1_w[l].astype(F32).reshape(1, d),
                        ln1_b[l].astype(F32).reshape(1, d), alpha, _OUT_ROW_TILE)
        h, hb = _ffn_ln(hb, w_ff1[l].astype(BF16), w_ff2[l].astype(BF16), h,
                        ln2_w[l].astype(F32).reshape(1, d), ln2_b[l].astype(F32).reshape(1, d),
                        alpha, _ROW_TILE, _FFN_TF)
    return h.reshape(bsz, seq_len, d)[:, N_META:].astype(x.dtype)
```

```python
import functools

import numpy as np
import jax
import jax.numpy as jnp
from jax import lax
from jax.experimental import pallas as pl
from jax.experimental.pallas import tpu as pltpu

N_META = 16
GRID_W = 64
GLA_HEADS = 4
GLA_DK = 128
GLA_DV = 256
GLA_RANK = 16
GLA_TAU = 16.0
_GLA_BLOCK = 256
_GLA_SAFE_RANGE = 80.0
NA_HEADS = 16
NA_DH = 64
NA_WIN_H = 8
NA_WIN_W = 16
NA_HEAD_GROUP = 4
LN_EPS = 1e-5
MASK_VALUE = -1e30

V7X_VMEM_LIMIT_BYTES = 58 * 1024 * 1024

F32 = jnp.float32
BF16 = jnp.bfloat16

_NT = (((1,), (1,)), ((), ()))
_TN = (((0,), (0,)), ((), ()))


def _dot(a, b, dims=None):
    if dims is None:
        return jnp.dot(a, b, preferred_element_type=F32)
    return lax.dot_general(a, b, dims, preferred_element_type=F32)


def _proj_kernel(a_ref, w_ref, o_ref):
    o_ref[...] = _dot(a_ref[...], w_ref[...]).astype(o_ref.dtype)


def _proj(a, w, tm, tn):
    m, k = a.shape
    n = w.shape[1]
    return pl.pallas_call(
        _proj_kernel,
        grid=(n // tn, m // tm),
        in_specs=[pl.BlockSpec((tm, k), lambda j, i: (i, 0)),
                  pl.BlockSpec((k, tn), lambda j, i: (0, j))],
        out_specs=pl.BlockSpec((tm, tn), lambda j, i: (i, j)),
        out_shape=jax.ShapeDtypeStruct((m, n), BF16),
        compiler_params=pltpu.CompilerParams(
            dimension_semantics=("arbitrary", "arbitrary"),
            vmem_limit_bytes=V7X_VMEM_LIMIT_BYTES),
        name="proj",
    )(a, w)


def _gla_decays(rows, c, g_ref):
    g2 = g_ref[rows, :]
    t_idx = lax.broadcasted_iota(jnp.int32, (c, c), 0)
    s_idx = lax.broadcasted_iota(jnp.int32, (c, c), 1)
    tri = jnp.where(s_idx <= t_idx, 1.0, 0.0).astype(BF16)
    g_hi = g2.astype(BF16)
    g_lo = (g2 - g_hi.astype(F32)).astype(BF16)
    p = _dot(tri, g_hi) + _dot(tri, g_lo)
    cb = p[c - 1:c, GLA_DK:] - p[:, GLA_DK:] + g2[:, GLA_DK:]
    return p[:, :GLA_DK], cb


def _gla_chunk_terms(rows, ci, size, q_ref, k_ref, v_ref, g_ref, qi_ref, oi_ref, ut_ref, e_ref, rng_ref):
    c = size
    bf, cb = _gla_decays(rows, c, g_ref)
    t_idx = lax.broadcasted_iota(jnp.int32, (c, c), 0)
    s_idx = lax.broadcasted_iota(jnp.int32, (c, c), 1)
    lower = s_idx <= t_idx
    upper = s_idx >= t_idx
    bf_end = bf[c - 1:c, :]
    cb_end = cb[0:1, :]
    bf_mid = bf[c // 2 - 1:c // 2, :]
    cb_mid = cb[c // 2:c // 2 + 1, :]
    rng_ref[...] = jnp.maximum(rng_ref[...], jnp.broadcast_to(jnp.maximum(-bf_end, -cb_end), rng_ref.shape))
    q = q_ref[0, rows, :].astype(F32) * (GLA_DK ** -0.5)
    k = k_ref[0, rows, :].astype(F32)
    v = v_ref[0, rows, :]
    qi_f = q * jnp.exp(bf)
    ks_f = k * jnp.exp(bf_end - bf)
    qi_b = q * jnp.exp(cb)
    ks_b = k * jnp.exp(cb_end - cb)
    sc_f = _dot((qi_f * jnp.exp(-bf_mid)).astype(BF16), (ks_f * jnp.exp(bf_mid - bf_end)).astype(BF16), _NT)
    sc_b = _dot((qi_b * jnp.exp(-cb_mid)).astype(BF16), (ks_b * jnp.exp(cb_mid - cb_end)).astype(BF16), _NT)
    sc = jnp.where(lower, sc_f, 0.0) + jnp.where(upper, sc_b, 0.0)
    oi_ref[rows, :] = _dot(sc.astype(BF16), v)
    qi_ref[rows, :] = jnp.concatenate([qi_f, qi_b], axis=-1).astype(BF16)
    ut_ref[ci] = _dot(v, jnp.concatenate([ks_f, ks_b], axis=-1).astype(BF16), _TN)
    e_ref[ci] = jnp.broadcast_to(jnp.exp(jnp.concatenate([bf_end, cb_end], axis=-1)), (8, 2 * GLA_DK))


def _gla_exact_intra(rows, size, q_ref, k_ref, v_ref, g_ref, oi_ref):
    c = size
    bf, cb = _gla_decays(rows, c, g_ref)
    q = q_ref[0, rows, :].astype(F32) * (GLA_DK ** -0.5)
    k = k_ref[0, rows, :].astype(F32)
    row = lax.broadcasted_iota(jnp.int32, (c, 1), 0)
    lane = lax.broadcasted_iota(jnp.int32, (c, c), 1)

    def body(t, sct):
        pick = row == t
        take = lambda a: jnp.sum(jnp.where(pick, a, 0.0), axis=0, keepdims=True)
        dec = (jnp.exp(jnp.where(row <= t, take(bf) - bf, MASK_VALUE))
               + jnp.exp(jnp.where(row >= t, take(cb) - cb, MASK_VALUE)))
        col = jnp.sum(k * take(q) * dec, axis=-1, keepdims=True)
        return jnp.where(lane == t, col, sct)

    sct = lax.fori_loop(0, c, body, jnp.zeros((c, c), F32))
    oi_ref[rows, :] = _dot(sct.astype(BF16), v_ref[0, rows, :], _TN)


def _gla_chunk_out(rows, ci, r_ref, nw_ref, qi_ref, oi_ref, st_ref, o_ref):
    o = oi_ref[rows, :] + _dot(qi_ref[rows, :], st_ref[ci], _NT)
    o = o * lax.rsqrt(jnp.mean(o * o, axis=-1, keepdims=True) + LN_EPS) * nw_ref[...]
    r = r_ref[0, rows, :].astype(F32)
    o_ref[0, rows, :] = (o * (r / (1.0 + jnp.exp(-r)))).astype(o_ref.dtype)


def _gla_kernel(q_ref, k_ref, v_ref, r_ref, lr_ref, wc_ref, bc_ref, nw_ref, o_ref,
                g_ref, qi_ref, oi_ref, ut_ref, e_ref, st_ref, rng_ref, *, seq_len, gate_rows):
    c = _GLA_BLOCK
    n_full = (seq_len - N_META) // c
    n_chunks = n_full + 1
    rng_ref[...] = jnp.zeros_like(rng_ref)

    for i in range(seq_len // gate_rows):
        rows = pl.ds(i * gate_rows, gate_rows)
        x = _dot(lr_ref[0, rows, :], wc_ref[0]) + bc_ref[0]
        g_ref[rows, :] = (jnp.minimum(x, 0.0) - jnp.log(1.0 + jnp.exp(-jnp.abs(x)))) * (1.0 / GLA_TAU)

    def chunk_rows(ci):
        return pl.ds(pl.multiple_of(N_META + (ci - 1) * c, 16), c)

    terms = functools.partial(_gla_chunk_terms, q_ref=q_ref, k_ref=k_ref, v_ref=v_ref, g_ref=g_ref,
                              qi_ref=qi_ref, oi_ref=oi_ref, ut_ref=ut_ref, e_ref=e_ref, rng_ref=rng_ref)
    terms(pl.ds(0, N_META), 0, N_META)

    def terms_body(ci, carry):
        terms(chunk_rows(ci), ci, c)
        return carry

    lax.fori_loop(1, n_chunks, terms_body, 0, unroll=2)

    @pl.when(jnp.max(rng_ref[...]) > _GLA_SAFE_RANGE)
    def _():
        exact = functools.partial(_gla_exact_intra, q_ref=q_ref, k_ref=k_ref, v_ref=v_ref, g_ref=g_ref,
                                  oi_ref=oi_ref)
        exact(pl.ds(0, N_META), N_META)

        def exact_body(ci, carry):
            exact(chunk_rows(ci), c)
            return carry

        lax.fori_loop(1, n_chunks, exact_body, 0)

    def scan_body(i, carry):
        s_f, s_b = carry
        j = n_chunks - 1 - i
        st_ref[i, :, :GLA_DK] = s_f.astype(BF16)
        st_ref[j, :, GLA_DK:] = s_b.astype(BF16)
        s_f = s_f * e_ref[i, 0:1, :GLA_DK] + ut_ref[i, :, :GLA_DK]
        s_b = s_b * e_ref[j, 0:1, GLA_DK:] + ut_ref[j, :, GLA_DK:]
        return s_f, s_b

    zero = jnp.zeros((GLA_DV, GLA_DK), F32)
    lax.fori_loop(0, n_chunks, scan_body, (zero, zero))

    out = functools.partial(_gla_chunk_out, r_ref=r_ref, nw_ref=nw_ref, qi_ref=qi_ref, oi_ref=oi_ref,
                            st_ref=st_ref, o_ref=o_ref)
    out(pl.ds(0, N_META), 0)

    def out_body(ci, carry):
        out(chunk_rows(ci), ci)
        return carry

    lax.fori_loop(1, n_chunks, out_body, 0, unroll=2)


def _gla(proj, wc, bc, nw):
    bsz, seq_len, _ = proj.shape
    assert (seq_len - N_META) % _GLA_BLOCK == 0 and seq_len % _ROW_TILE == 0
    n_chunks = (seq_len - N_META) // _GLA_BLOCK + 1
    kern = functools.partial(_gla_kernel, seq_len=seq_len, gate_rows=_ROW_TILE)
    qk_blk = lambda off: pl.BlockSpec((1, seq_len, GLA_DK), lambda b, h: (b, 0, off + h))
    vr_blk = lambda off: pl.BlockSpec((1, seq_len, GLA_DV), lambda b, h: (b, 0, off + h))
    return pl.pallas_call(
        kern,
        grid=(bsz, GLA_HEADS),
        in_specs=[qk_blk(_COL_GQ // GLA_DK), qk_blk(_COL_GK // GLA_DK),
                  vr_blk(_COL_GV // GLA_DV), vr_blk(_COL_GR // GLA_DV),
                  pl.BlockSpec((1, seq_len, 128), lambda b, h: (b, 0, _COL_LR // 128)),
                  pl.BlockSpec((1, 128, 2 * GLA_DK), lambda b, h: (h, 0, 0)),
                  pl.BlockSpec((1, 1, 2 * GLA_DK), lambda b, h: (h, 0, 0)),
                  pl.BlockSpec((1, GLA_DV), lambda b, h: (0, 0))],
        out_specs=pl.BlockSpec((1, seq_len, GLA_DV), lambda b, h: (b, 0, h)),
        out_shape=jax.ShapeDtypeStruct((bsz, seq_len, GLA_HEADS * GLA_DV), BF16),
        scratch_shapes=[pltpu.VMEM((seq_len, 2 * GLA_DK), F32),
                        pltpu.VMEM((seq_len, 2 * GLA_DK), BF16),
                        pltpu.VMEM((seq_len, GLA_DV), F32),
                        pltpu.VMEM((n_chunks, GLA_DV, 2 * GLA_DK), F32),
                        pltpu.VMEM((n_chunks, 8, 2 * GLA_DK), F32),
                        pltpu.VMEM((n_chunks, GLA_DV, 2 * GLA_DK), BF16),
                        pltpu.VMEM((8, GLA_DK), F32)],
        compiler_params=pltpu.CompilerParams(
            dimension_semantics=("arbitrary", "arbitrary"),
            vmem_limit_bytes=V7X_VMEM_LIMIT_BYTES),
        name="gla",
    )(proj, proj, proj, proj, proj, wc, bc, nw)


def _na_kernel(q_ref, k_ref, v_ref, t_ref, o_ref, *, seq_len):
    rows = (seq_len - N_META) // GRID_W
    hw = NA_HEAD_GROUP * NA_DH
    nq = NA_HEAD_GROUP * GRID_W
    n_keys = NA_WIN_H * GRID_W
    meta_blk = 128

    def block_diag(q4):
        n = q4.shape[0]
        qrep = jnp.concatenate([q4] * NA_HEAD_GROUP, axis=0)
        rb = lax.broadcasted_iota(jnp.int32, (NA_HEAD_GROUP * n, hw), 0) // n
        lb = lax.broadcasted_iota(jnp.int32, (NA_HEAD_GROUP * n, hw), 1) // NA_DH
        return jnp.where(rb == lb, qrep, jnp.zeros_like(qrep))

    def pick_diag(o, n):
        lb = lax.broadcasted_iota(jnp.int32, (n, hw), 1) // NA_DH
        out = jnp.zeros((n, hw), F32)
        for h in range(NA_HEAD_GROUP):
            out = jnp.where(lb == h, o[h * n:(h + 1) * n, :], out)
        return out

    k_meta = k_ref[0, 0:meta_blk, :]
    v_meta = v_ref[0, 0:meta_blk, :]

    def meta_scores(qbd):
        s = _dot(qbd, k_meta, _NT)
        lane = lax.broadcasted_iota(jnp.int32, s.shape, 1)
        return jnp.where(lane < N_META, s, MASK_VALUE)

    qm = block_diag(q_ref[0, 0:N_META, :] * (NA_DH ** -0.5))
    sm = meta_scores(qm)
    pm = jnp.exp(sm - jnp.max(sm, axis=-1, keepdims=True))
    om = _dot(pm.astype(BF16), v_meta) / jnp.sum(pm, axis=-1, keepdims=True)
    o_ref[0, 0:N_META, :] = pick_diag(om, N_META).astype(o_ref.dtype)

    def row_body(r, carry):
        rs = jnp.clip(r - NA_WIN_H // 2, 0, rows - NA_WIN_H)
        delta = r - rs
        q_start = pl.multiple_of(N_META + r * GRID_W, 16)
        k_start = pl.multiple_of(N_META + rs * GRID_W, 16)
        qbd = block_diag(q_ref[0, pl.ds(q_start, GRID_W), :] * (NA_DH ** -0.5))
        kw = k_ref[0, pl.ds(k_start, n_keys), :]
        vw = v_ref[0, pl.ds(k_start, n_keys), :]
        sw = _dot(qbd, kw, _NT)
        bias = jnp.concatenate(
            [t_ref[0, 2 * j - delta + NA_WIN_H - 1] for j in range(NA_WIN_H // 2)], axis=-1)
        sw = sw + bias
        smeta = meta_scores(qbd)
        mx = jnp.maximum(jnp.max(sw, axis=-1, keepdims=True), jnp.max(smeta, axis=-1, keepdims=True))
        pw = jnp.exp(sw - mx)
        pmeta = jnp.exp(smeta - mx)
        den = jnp.sum(pw, axis=-1, keepdims=True) + jnp.sum(pmeta, axis=-1, keepdims=True)
        o = (_dot(pw.astype(BF16), vw) + _dot(pmeta.astype(BF16), v_meta)) / den
        o_ref[0, pl.ds(q_start, GRID_W), :] = pick_diag(o, GRID_W).astype(o_ref.dtype)
        return carry

    lax.fori_loop(0, rows, row_body, 0)


def _na_bias_table(rel_bias):
    col = np.arange(GRID_W)
    cs = np.clip(col - NA_WIN_W // 2, 0, GRID_W - NA_WIN_W)
    inside = (col[None, :] >= cs[:, None]) & (col[None, :] < cs[:, None] + NA_WIN_W)
    dc = np.clip(col[None, :] - col[:, None], -(NA_WIN_W - 1), NA_WIN_W - 1) + NA_WIN_W - 1
    t = jnp.where(inside[None, None], rel_bias.astype(F32)[:, :, dc], MASK_VALUE)
    t = jnp.concatenate([t[:, :-1], t[:, 1:]], axis=-1)
    n_dr = t.shape[1]
    t = t.reshape(NA_HEADS // NA_HEAD_GROUP, NA_HEAD_GROUP, n_dr, GRID_W, 2 * GRID_W)
    return t.transpose(0, 2, 1, 3, 4).reshape(NA_HEADS // NA_HEAD_GROUP, n_dr,
                                               NA_HEAD_GROUP * GRID_W, 2 * GRID_W)


def _na(proj, table):
    bsz, seq_len, _ = proj.shape
    hw = NA_HEAD_GROUP * NA_DH
    n_groups = NA_HEADS // NA_HEAD_GROUP
    kern = functools.partial(_na_kernel, seq_len=seq_len)
    blk = lambda off: pl.BlockSpec((1, seq_len, hw), lambda g, b: (b, 0, off + g))
    return pl.pallas_call(
        kern,
        grid=(n_groups, bsz),
        in_specs=[blk(_COL_NQ // hw), blk(_COL_NK // hw), blk(_COL_NV // hw),
                  pl.BlockSpec((1,) + table.shape[1:], lambda g, b: (g, 0, 0, 0))],
        out_specs=pl.BlockSpec((1, seq_len, hw), lambda g, b: (b, 0, g)),
        out_shape=jax.ShapeDtypeStruct((bsz, seq_len, NA_HEADS * NA_DH), BF16),
        compiler_params=pltpu.CompilerParams(
            dimension_semantics=("arbitrary", "arbitrary"),
            vmem_limit_bytes=V7X_VMEM_LIMIT_BYTES),
        name="na",
    )(proj, proj, proj, table)


def _layer_norm(z, w, b):
    mu = jnp.mean(z, axis=-1, keepdims=True)
    zc = z - mu
    var = jnp.mean(zc * zc, axis=-1, keepdims=True)
    return zc * lax.rsqrt(var + LN_EPS) * w + b


def _out_ln_kernel(y_ref, w_ref, h_ref, lw_ref, lb_ref, o_ref, ob_ref, *, alpha):
    z = alpha * h_ref[...] + _dot(y_ref[...], w_ref[...])
    o = _layer_norm(z, lw_ref[...], lb_ref[...])
    o_ref[...] = o
    ob_ref[...] = o.astype(ob_ref.dtype)


def _out_ln(y, w, h, lw, lb, alpha, tm):
    m, k = y.shape
    n = w.shape[1]
    row = lambda width: pl.BlockSpec((tm, width), lambda i: (i, 0))
    full = lambda shape: pl.BlockSpec(shape, lambda i: (0, 0))
    return pl.pallas_call(
        functools.partial(_out_ln_kernel, alpha=alpha),
        grid=(m // tm,),
        in_specs=[row(k), full((k, n)), row(n), full((1, n)), full((1, n))],
        out_specs=[row(n), row(n)],
        out_shape=[jax.ShapeDtypeStruct((m, n), F32), jax.ShapeDtypeStruct((m, n), BF16)],
        compiler_params=pltpu.CompilerParams(
            dimension_semantics=("arbitrary",),
            vmem_limit_bytes=V7X_VMEM_LIMIT_BYTES),
        name="out_ln",
    )(y, w, h, lw, lb)


def _ffn_ln_kernel(a_ref, w1_ref, w2_ref, h_ref, lw_ref, lb_ref, o_ref, ob_ref, *, alpha):
    f = pl.program_id(1)

    @pl.when(f == 0)
    def _():
        o_ref[...] = alpha * h_ref[...]

    u = jnp.square(jnp.maximum(_dot(a_ref[...], w1_ref[...]), 0.0)).astype(BF16)
    o_ref[...] += _dot(u, w2_ref[...])

    @pl.when(f == pl.num_programs(1) - 1)
    def _():
        o = _layer_norm(o_ref[...], lw_ref[...], lb_ref[...])
        o_ref[...] = o
        ob_ref[...] = o.astype(ob_ref.dtype)


def _ffn_ln(a, w1, w2, h, lw, lb, alpha, tm, tf):
    m, d = a.shape
    d_ff = w1.shape[1]
    row = pl.BlockSpec((tm, d), lambda i, f: (i, 0))
    vec = pl.BlockSpec((1, d), lambda i, f: (0, 0))
    return pl.pallas_call(
        functools.partial(_ffn_ln_kernel, alpha=alpha),
        grid=(m // tm, d_ff // tf),
        in_specs=[row, pl.BlockSpec((d, tf), lambda i, f: (0, f)),
                  pl.BlockSpec((tf, d), lambda i, f: (f, 0)), row, vec, vec],
        out_specs=[row, row],
        out_shape=[jax.ShapeDtypeStruct((m, d), F32), jax.ShapeDtypeStruct((m, d), BF16)],
        compiler_params=pltpu.CompilerParams(
            dimension_semantics=("arbitrary", "arbitrary"),
            vmem_limit_bytes=V7X_VMEM_LIMIT_BYTES),
        name="ffn_ln",
    )(a, w1, w2, h, lw, lb)


_GLA_QK = GLA_HEADS * GLA_DK
_GLA_VR = GLA_HEADS * GLA_DV
_NA_W = NA_HEADS * NA_DH
_COL_GQ = 0
_COL_GK = _COL_GQ + _GLA_QK
_COL_GV = _COL_GK + _GLA_QK
_COL_GR = _COL_GV + _GLA_VR
_COL_NQ = _COL_GR + _GLA_VR
_COL_NK = _COL_NQ + _NA_W
_COL_NV = _COL_NK + _NA_W
_COL_LR = _COL_NV + _NA_W
_PROJ_TN = 1280
_PROJ_COLS = -(-(_COL_LR + 2 * GLA_RANK) // _PROJ_TN) * _PROJ_TN
_ROW_TILE = 688
_OUT_ROW_TILE = 344
_FFN_TF = 512


def _fused_w_in(w_in):
    d = w_in.shape[0]
    lr0 = _COL_GR + _GLA_VR
    lr1 = lr0 + 2 * GLA_RANK
    pad = jnp.zeros((d, _PROJ_COLS - _COL_LR - 2 * GLA_RANK), w_in.dtype)
    return jnp.concatenate([w_in[:, :lr0], w_in[:, lr1:], w_in[:, lr0:lr1], pad], axis=1).astype(BF16)


def _gla_gate_params(w_up, b_up):
    wc = jnp.zeros((GLA_HEADS, 128, 2 * GLA_DK), F32)
    for z in range(2):
        w = w_up[z].astype(F32).reshape(GLA_RANK, GLA_HEADS, GLA_DK).transpose(1, 0, 2)
        wc = wc.at[:, z * GLA_RANK:(z + 1) * GLA_RANK, z * GLA_DK:(z + 1) * GLA_DK].set(w)
    bc = b_up.astype(F32).reshape(2, GLA_HEADS, GLA_DK).transpose(1, 0, 2).reshape(GLA_HEADS, 1, 2 * GLA_DK)
    return wc.astype(BF16), bc


def kernel(x, meta, w_in, gla_w_up, gla_b_up, gla_norm_w, na_rel_bias, w_out, ln1_w, ln1_b,
           w_ff1, w_ff2, ln2_w, ln2_b):
    bsz, seq, d = x.shape
    depth = w_in.shape[0]
    alpha = (2 * depth) ** 0.25
    seq_len = seq + N_META
    m = bsz * seq_len
    meta_b = jnp.broadcast_to(meta.astype(x.dtype)[None], (bsz, N_META, d))
    h = jnp.concatenate([meta_b, x], axis=1).reshape(m, d).astype(F32)
    hb = h.astype(BF16)
    for l in range(depth):
        proj = _proj(hb, _fused_w_in(w_in[l]), _ROW_TILE, _PROJ_TN).reshape(bsz, seq_len, _PROJ_COLS)
        wc, bc = _gla_gate_params(gla_w_up[l], gla_b_up[l])
        y_gla = _gla(proj, wc, bc, gla_norm_w[l].astype(F32).reshape(1, GLA_DV))
        y_na = _na(proj, _na_bias_table(na_rel_bias[l]))
        y = jnp.concatenate([y_gla, y_na], axis=-1).reshape(m, d)
        h, hb = _out_ln(y, w_out[l].astype(BF16), h, ln1_w[l].astype(F32).reshape(1, d),
                        ln1_b[l].astype(F32).reshape(1, d), alpha, _OUT_ROW_TILE)
        h, hb = _ffn_ln(hb, w_ff1[l].astype(BF16), w_ff2[l].astype(BF16), h,
                        ln2_w[l].astype(F32).reshape(1, d), ln2_b[l].astype(F32).reshape(1, d),
                        alpha, _ROW_TILE, _FFN_TF)
    return h.reshape(bsz, seq_len, d)[:, N_META:].astype(x.dtype)
```

```python
import functools

import numpy as np
import jax
import jax.numpy as jnp
from jax import lax
from jax.experimental import pallas as pl
from jax.experimental.pallas import tpu as pltpu

N_META = 16
GRID_W = 64
GLA_HEADS = 4
GLA_DK = 128
GLA_DV = 256
GLA_RANK = 16
GLA_TAU = 16.0
_GLA_BLOCK = 256
_GLA_SAFE_RANGE = 80.0
NA_HEADS = 16
NA_DH = 64
NA_WIN_H = 8
NA_WIN_W = 16
NA_HEAD_GROUP = 4
LN_EPS = 1e-5
MASK_VALUE = -1e30

V7X_VMEM_LIMIT_BYTES = 58 * 1024 * 1024

F32 = jnp.float32
BF16 = jnp.bfloat16

_NT = (((1,), (1,)), ((), ()))
_TN = (((0,), (0,)), ((), ()))


def _dot(a, b, dims=None):
    if dims is None:
        return jnp.dot(a, b, preferred_element_type=F32)
    return lax.dot_general(a, b, dims, preferred_element_type=F32)


def _proj_kernel(a_ref, w_ref, o_ref):
    o_ref[...] = _dot(a_ref[...], w_ref[...]).astype(o_ref.dtype)


def _proj(a, w, layer, tm, tn):
    m, k = a.shape
    n = w.shape[2]
    return pl.pallas_call(
        _proj_kernel,
        grid=(n // tn, m // tm),
        in_specs=[pl.BlockSpec((tm, k), lambda j, i: (i, 0)),
                  pl.BlockSpec((None, k, tn), lambda j, i: (layer, 0, j))],
        out_specs=pl.BlockSpec((tm, tn), lambda j, i: (i, j)),
        out_shape=jax.ShapeDtypeStruct((m, n), BF16),
        compiler_params=pltpu.CompilerParams(
            dimension_semantics=("arbitrary", "arbitrary"),
            vmem_limit_bytes=V7X_VMEM_LIMIT_BYTES),
        name="proj",
    )(a, w)


def _gla_decays(rows, c, g_ref):
    g2 = g_ref[rows, :]
    t_idx = lax.broadcasted_iota(jnp.int32, (c, c), 0)
    s_idx = lax.broadcasted_iota(jnp.int32, (c, c), 1)
    tri = jnp.where(s_idx <= t_idx, 1.0, 0.0).astype(BF16)
    g_hi = g2.astype(BF16)
    g_lo = (g2 - g_hi.astype(F32)).astype(BF16)
    p = _dot(tri, g_hi) + _dot(tri, g_lo)
    cb = p[c - 1:c, GLA_DK:] - p[:, GLA_DK:] + g2[:, GLA_DK:]
    return p[:, :GLA_DK], cb


def _gla_chunk_terms(rows, ci, size, q_ref, k_ref, v_ref, g_ref, qi_ref, oi_ref, ut_ref, e_ref, rng_ref):
    c = size
    bf, cb = _gla_decays(rows, c, g_ref)
    t_idx = lax.broadcasted_iota(jnp.int32, (c, c), 0)
    s_idx = lax.broadcasted_iota(jnp.int32, (c, c), 1)
    lower = s_idx <= t_idx
    upper = s_idx >= t_idx
    bf_end = bf[c - 1:c, :]
    cb_end = cb[0:1, :]
    bf_mid = bf[c // 2 - 1:c // 2, :]
    cb_mid = cb[c // 2:c // 2 + 1, :]
    rng_ref[...] = jnp.maximum(rng_ref[...], jnp.broadcast_to(jnp.maximum(-bf_end, -cb_end), rng_ref.shape))
    q = q_ref[0, rows, :].astype(F32) * (GLA_DK ** -0.5)
    k = k_ref[0, rows, :].astype(F32)
    v = v_ref[0, rows, :]
    qi_f = q * jnp.exp(bf)
    ks_f = k * jnp.exp(bf_end - bf)
    qi_b = q * jnp.exp(cb)
    ks_b = k * jnp.exp(cb_end - cb)
    sc_f = _dot((qi_f * jnp.exp(-bf_mid)).astype(BF16), (ks_f * jnp.exp(bf_mid - bf_end)).astype(BF16), _NT)
    sc_b = _dot((qi_b * jnp.exp(-cb_mid)).astype(BF16), (ks_b * jnp.exp(cb_mid - cb_end)).astype(BF16), _NT)
    sc = jnp.where(lower, sc_f, 0.0) + jnp.where(upper, sc_b, 0.0)
    oi_ref[rows, :] = _dot(sc.astype(BF16), v)
    qi_ref[rows, :] = jnp.concatenate([qi_f, qi_b], axis=-1).astype(BF16)
    ut_ref[ci] = _dot(v, jnp.concatenate([ks_f, ks_b], axis=-1).astype(BF16), _TN)
    e_ref[ci] = jnp.broadcast_to(jnp.exp(jnp.concatenate([bf_end, cb_end], axis=-1)), (8, 2 * GLA_DK))


def _gla_exact_intra(rows, size, q_ref, k_ref, v_ref, g_ref, oi_ref):
    c = size
    bf, cb = _gla_decays(rows, c, g_ref)
    q = q_ref[0, rows, :].astype(F32) * (GLA_DK ** -0.5)
    k = k_ref[0, rows, :].astype(F32)
    row = lax.broadcasted_iota(jnp.int32, (c, 1), 0)
    lane = lax.broadcasted_iota(jnp.int32, (c, c), 1)

    def body(t, sct):
        pick = row == t
        take = lambda a: jnp.sum(jnp.where(pick, a, 0.0), axis=0, keepdims=True)
        dec = (jnp.exp(jnp.where(row <= t, take(bf) - bf, MASK_VALUE))
               + jnp.exp(jnp.where(row >= t, take(cb) - cb, MASK_VALUE)))
        col = jnp.sum(k * take(q) * dec, axis=-1, keepdims=True)
        return jnp.where(lane == t, col, sct)

    sct = lax.fori_loop(0, c, body, jnp.zeros((c, c), F32))
    oi_ref[rows, :] = _dot(sct.astype(BF16), v_ref[0, rows, :], _TN)


def _gla_chunk_out(rows, ci, r_ref, nw_ref, qi_ref, oi_ref, st_ref, o_ref):
    o = oi_ref[rows, :] + _dot(qi_ref[rows, :], st_ref[ci], _NT)
    o = o * lax.rsqrt(jnp.mean(o * o, axis=-1, keepdims=True) + LN_EPS) * nw_ref[...]
    r = r_ref[0, rows, :].astype(F32)
    o_ref[0, rows, :] = (o * (r / (1.0 + jnp.exp(-r)))).astype(o_ref.dtype)


def _gla_kernel(q_ref, k_ref, v_ref, r_ref, lr_ref, wc_ref, bc_ref, nw_ref, o_ref,
                g_ref, qi_ref, oi_ref, ut_ref, e_ref, st_ref, rng_ref, *, seq_len, gate_rows):
    c = _GLA_BLOCK
    n_full = (seq_len - N_META) // c
    n_chunks = n_full + 1
    rng_ref[...] = jnp.zeros_like(rng_ref)

    for i in range(seq_len // gate_rows):
        rows = pl.ds(i * gate_rows, gate_rows)
        x = _dot(lr_ref[0, rows, :], wc_ref[0]) + bc_ref[0]
        g_ref[rows, :] = (jnp.minimum(x, 0.0) - jnp.log(1.0 + jnp.exp(-jnp.abs(x)))) * (1.0 / GLA_TAU)

    def chunk_rows(ci):
        return pl.ds(pl.multiple_of(N_META + (ci - 1) * c, 16), c)

    terms = functools.partial(_gla_chunk_terms, q_ref=q_ref, k_ref=k_ref, v_ref=v_ref, g_ref=g_ref,
                              qi_ref=qi_ref, oi_ref=oi_ref, ut_ref=ut_ref, e_ref=e_ref, rng_ref=rng_ref)
    terms(pl.ds(0, N_META), 0, N_META)

    def terms_body(ci, carry):
        terms(chunk_rows(ci), ci, c)
        return carry

    lax.fori_loop(1, n_chunks, terms_body, 0, unroll=2)

    @pl.when(jnp.max(rng_ref[...]) > _GLA_SAFE_RANGE)
    def _():
        exact = functools.partial(_gla_exact_intra, q_ref=q_ref, k_ref=k_ref, v_ref=v_ref, g_ref=g_ref,
                                  oi_ref=oi_ref)
        exact(pl.ds(0, N_META), N_META)

        def exact_body(ci, carry):
            exact(chunk_rows(ci), c)
            return carry

        lax.fori_loop(1, n_chunks, exact_body, 0)

    def scan_body(i, carry):
        s_f, s_b = carry
        j = n_chunks - 1 - i
        st_ref[i, :, :GLA_DK] = s_f.astype(BF16)
        st_ref[j, :, GLA_DK:] = s_b.astype(BF16)
        s_f = s_f * e_ref[i, 0:1, :GLA_DK] + ut_ref[i, :, :GLA_DK]
        s_b = s_b * e_ref[j, 0:1, GLA_DK:] + ut_ref[j, :, GLA_DK:]
        return s_f, s_b

    zero = jnp.zeros((GLA_DV, GLA_DK), F32)
    lax.fori_loop(0, n_chunks, scan_body, (zero, zero))

    out = functools.partial(_gla_chunk_out, r_ref=r_ref, nw_ref=nw_ref, qi_ref=qi_ref, oi_ref=oi_ref,
                            st_ref=st_ref, o_ref=o_ref)
    out(pl.ds(0, N_META), 0)

    def out_body(ci, carry):
        out(chunk_rows(ci), ci)
        return carry

    lax.fori_loop(1, n_chunks, out_body, 0, unroll=2)


def _gla(proj, wc, bc, nw, layer):
    bsz, seq_len, _ = proj.shape
    assert (seq_len - N_META) % _GLA_BLOCK == 0 and seq_len % _ROW_TILE == 0
    n_chunks = (seq_len - N_META) // _GLA_BLOCK + 1
    kern = functools.partial(_gla_kernel, seq_len=seq_len, gate_rows=_ROW_TILE)
    qk_blk = lambda off: pl.BlockSpec((1, seq_len, GLA_DK), lambda b, h: (b, 0, off + h))
    vr_blk = lambda off: pl.BlockSpec((1, seq_len, GLA_DV), lambda b, h: (b, 0, off + h))
    return pl.pallas_call(
        kern,
        grid=(bsz, GLA_HEADS),
        in_specs=[qk_blk(_COL_GQ // GLA_DK), qk_blk(_COL_GK // GLA_DK),
                  vr_blk(_COL_GV // GLA_DV), vr_blk(_COL_GR // GLA_DV),
                  pl.BlockSpec((1, seq_len, 128), lambda b, h: (b, 0, _COL_LR // 128)),
                  pl.BlockSpec((None, 1, 128, 2 * GLA_DK), lambda b, h: (layer, h, 0, 0)),
                  pl.BlockSpec((None, 1, 1, 2 * GLA_DK), lambda b, h: (layer, h, 0, 0)),
                  pl.BlockSpec((None, 1, GLA_DV), lambda b, h: (layer, 0, 0))],
        out_specs=pl.BlockSpec((1, seq_len, GLA_DV), lambda b, h: (b, 0, h)),
        out_shape=jax.ShapeDtypeStruct((bsz, seq_len, GLA_HEADS * GLA_DV), BF16),
        scratch_shapes=[pltpu.VMEM((seq_len, 2 * GLA_DK), F32),
                        pltpu.VMEM((seq_len, 2 * GLA_DK), BF16),
                        pltpu.VMEM((seq_len, GLA_DV), F32),
                        pltpu.VMEM((n_chunks, GLA_DV, 2 * GLA_DK), F32),
                        pltpu.VMEM((n_chunks, 8, 2 * GLA_DK), F32),
                        pltpu.VMEM((n_chunks, GLA_DV, 2 * GLA_DK), BF16),
                        pltpu.VMEM((8, GLA_DK), F32)],
        compiler_params=pltpu.CompilerParams(
            dimension_semantics=("arbitrary", "arbitrary"),
            vmem_limit_bytes=V7X_VMEM_LIMIT_BYTES),
        name="gla",
    )(proj, proj, proj, proj, proj, wc, bc, nw)


def _na_kernel(q_ref, k_ref, v_ref, t_ref, o_ref, *, seq_len):
    rows = (seq_len - N_META) // GRID_W
    hw = NA_HEAD_GROUP * NA_DH
    nq = NA_HEAD_GROUP * GRID_W
    n_keys = NA_WIN_H * GRID_W
    meta_blk = 128

    def block_diag(q4):
        n = q4.shape[0]
        qrep = jnp.concatenate([q4] * NA_HEAD_GROUP, axis=0)
        rb = lax.broadcasted_iota(jnp.int32, (NA_HEAD_GROUP * n, hw), 0) // n
        lb = lax.broadcasted_iota(jnp.int32, (NA_HEAD_GROUP * n, hw), 1) // NA_DH
        return jnp.where(rb == lb, qrep, jnp.zeros_like(qrep))

    def pick_diag(o, n):
        lb = lax.broadcasted_iota(jnp.int32, (n, hw), 1) // NA_DH
        out = jnp.zeros((n, hw), F32)
        for h in range(NA_HEAD_GROUP):
            out = jnp.where(lb == h, o[h * n:(h + 1) * n, :], out)
        return out

    k_meta = k_ref[0, 0:meta_blk, :]
    v_meta = v_ref[0, 0:meta_blk, :]

    def meta_scores(qbd):
        s = _dot(qbd, k_meta, _NT)
        lane = lax.broadcasted_iota(jnp.int32, s.shape, 1)
        return jnp.where(lane < N_META, s, MASK_VALUE)

    qm = block_diag(q_ref[0, 0:N_META, :] * (NA_DH ** -0.5))
    sm = meta_scores(qm)
    pm = jnp.exp(sm - jnp.max(sm, axis=-1, keepdims=True))
    om = _dot(pm.astype(BF16), v_meta) / jnp.sum(pm, axis=-1, keepdims=True)
    o_ref[0, 0:N_META, :] = pick_diag(om, N_META).astype(o_ref.dtype)

    def row_body(r, carry):
        rs = jnp.clip(r - NA_WIN_H // 2, 0, rows - NA_WIN_H)
        delta = r - rs
        q_start = pl.multiple_of(N_META + r * GRID_W, 16)
        k_start = pl.multiple_of(N_META + rs * GRID_W, 16)
        qbd = block_diag(q_ref[0, pl.ds(q_start, GRID_W), :] * (NA_DH ** -0.5))
        kw = k_ref[0, pl.ds(k_start, n_keys), :]
        vw = v_ref[0, pl.ds(k_start, n_keys), :]
        sw = _dot(qbd, kw, _NT)
        bias = jnp.concatenate(
            [t_ref[2 * j - delta + NA_WIN_H - 1] for j in range(NA_WIN_H // 2)], axis=-1)
        sw = sw + bias
        smeta = meta_scores(qbd)
        blocks = lambda a: [a[:, i:i + meta_blk] for i in range(0, n_keys, meta_blk)]
        mx = jnp.max(functools.reduce(jnp.maximum, blocks(sw), smeta), axis=-1, keepdims=True)
        pw = jnp.exp(sw - mx)
        pmeta = jnp.exp(smeta - mx)
        den = jnp.sum(functools.reduce(jnp.add, blocks(pw), pmeta), axis=-1, keepdims=True)
        o = (_dot(pw.astype(BF16), vw) + _dot(pmeta.astype(BF16), v_meta)) / den
        o_ref[0, pl.ds(q_start, GRID_W), :] = pick_diag(o, GRID_W).astype(o_ref.dtype)
        return carry

    lax.fori_loop(0, rows, row_body, 0, unroll=4)


def _na_bias_table(rel_bias):
    depth = rel_bias.shape[0]
    col = np.arange(GRID_W)
    cs = np.clip(col - NA_WIN_W // 2, 0, GRID_W - NA_WIN_W)
    inside = (col[None, :] >= cs[:, None]) & (col[None, :] < cs[:, None] + NA_WIN_W)
    dc = col[None, :] - col[:, None] + NA_WIN_W - 1
    onehot = (np.arange(2 * NA_WIN_W - 1)[:, None, None] == dc[None]) & inside[None]
    t = jnp.einsum("lhdj,jqk->lhdqk", rel_bias.astype(F32), jnp.asarray(onehot, F32),
                   precision=lax.Precision.HIGHEST)
    t = jnp.where(inside, t, MASK_VALUE)
    t = jnp.concatenate([t[:, :, :-1], t[:, :, 1:]], axis=-1)
    n_dr = t.shape[2]
    t = t.reshape(depth, NA_HEADS // NA_HEAD_GROUP, NA_HEAD_GROUP, n_dr, GRID_W, 2 * GRID_W)
    return t.transpose(0, 1, 3, 2, 4, 5).reshape(depth, NA_HEADS // NA_HEAD_GROUP, n_dr,
                                                  NA_HEAD_GROUP * GRID_W, 2 * GRID_W)


def _na(proj, table, layer):
    bsz, seq_len, _ = proj.shape
    hw = NA_HEAD_GROUP * NA_DH
    n_groups = NA_HEADS // NA_HEAD_GROUP
    kern = functools.partial(_na_kernel, seq_len=seq_len)
    blk = lambda off: pl.BlockSpec((1, seq_len, hw), lambda g, b: (b, 0, off + g))
    return pl.pallas_call(
        kern,
        grid=(n_groups, bsz),
        in_specs=[blk(_COL_NQ // hw), blk(_COL_NK // hw), blk(_COL_NV // hw),
                  pl.BlockSpec((None, None) + table.shape[2:], lambda g, b: (layer, g, 0, 0, 0))],
        out_specs=pl.BlockSpec((1, seq_len, hw), lambda g, b: (b, 0, g)),
        out_shape=jax.ShapeDtypeStruct((bsz, seq_len, NA_HEADS * NA_DH), BF16),
        compiler_params=pltpu.CompilerParams(
            dimension_semantics=("arbitrary", "arbitrary"),
            vmem_limit_bytes=V7X_VMEM_LIMIT_BYTES),
        name="na",
    )(proj, proj, proj, table)


def _layer_norm(z, w, b):
    mu = jnp.mean(z, axis=-1, keepdims=True)
    zc = z - mu
    var = jnp.mean(zc * zc, axis=-1, keepdims=True)
    return zc * lax.rsqrt(var + LN_EPS) * w + b


def _out_ln_kernel(yg_ref, yn_ref, w_ref, h_ref, lw_ref, lb_ref, o_ref, ob_ref, *, alpha):
    kg = yg_ref.shape[1]
    mix = _dot(yg_ref[...], w_ref[0:kg, :]) + _dot(yn_ref[...], w_ref[kg:, :])
    o = _layer_norm(alpha * h_ref[...] + mix, lw_ref[...], lb_ref[...])
    o_ref[...] = o
    ob_ref[...] = o.astype(ob_ref.dtype)


def _out_ln(yg, yn, w, h, lw, lb, layer, alpha, tm):
    m, kg = yg.shape
    kn = yn.shape[1]
    n = w.shape[2]
    row = lambda width: pl.BlockSpec((tm, width), lambda i: (i, 0))
    per_layer = lambda shape: pl.BlockSpec((None,) + shape, lambda i: (layer, 0, 0))
    return pl.pallas_call(
        functools.partial(_out_ln_kernel, alpha=alpha),
        grid=(m // tm,),
        in_specs=[row(kg), row(kn), per_layer((kg + kn, n)), row(n), per_layer((1, n)), per_layer((1, n))],
        out_specs=[row(n), row(n)],
        out_shape=[jax.ShapeDtypeStruct((m, n), F32), jax.ShapeDtypeStruct((m, n), BF16)],
        compiler_params=pltpu.CompilerParams(
            dimension_semantics=("arbitrary",),
            vmem_limit_bytes=V7X_VMEM_LIMIT_BYTES),
        name="out_ln",
    )(yg, yn, w, h, lw, lb)


def _ffn_ln_kernel(a_ref, w1_ref, w2_ref, h_ref, lw_ref, lb_ref, o_ref, ob_ref, *, alpha):
    f = pl.program_id(1)

    @pl.when(f == 0)
    def _():
        o_ref[...] = alpha * h_ref[...]

    u = jnp.square(jnp.maximum(_dot(a_ref[...], w1_ref[...]), 0.0)).astype(BF16)
    o_ref[...] += _dot(u, w2_ref[...])

    @pl.when(f == pl.num_programs(1) - 1)
    def _():
        o = _layer_norm(o_ref[...], lw_ref[...], lb_ref[...])
        o_ref[...] = o
        ob_ref[...] = o.astype(ob_ref.dtype)


def _ffn_ln(a, w1, w2, h, lw, lb, layer, alpha, tm, tf):
    m, d = a.shape
    d_ff = w1.shape[2]
    row = pl.BlockSpec((tm, d), lambda i, f: (i, 0))
    vec = pl.BlockSpec((None, 1, d), lambda i, f: (layer, 0, 0))
    return pl.pallas_call(
        functools.partial(_ffn_ln_kernel, alpha=alpha),
        grid=(m // tm, d_ff // tf),
        in_specs=[row, pl.BlockSpec((None, d, tf), lambda i, f: (layer, 0, f)),
                  pl.BlockSpec((None, tf, d), lambda i, f: (layer, f, 0)), row, vec, vec],
        out_specs=[row, row],
        out_shape=[jax.ShapeDtypeStruct((m, d), F32), jax.ShapeDtypeStruct((m, d), BF16)],
        compiler_params=pltpu.CompilerParams(
            dimension_semantics=("arbitrary", "arbitrary"),
            vmem_limit_bytes=V7X_VMEM_LIMIT_BYTES),
        name="ffn_ln",
    )(a, w1, w2, h, lw, lb)


_GLA_QK = GLA_HEADS * GLA_DK
_GLA_VR = GLA_HEADS * GLA_DV
_NA_W = NA_HEADS * NA_DH
_COL_GQ = 0
_COL_GK = _COL_GQ + _GLA_QK
_COL_GV = _COL_GK + _GLA_QK
_COL_GR = _COL_GV + _GLA_VR
_COL_NQ = _COL_GR + _GLA_VR
_COL_NK = _COL_NQ + _NA_W
_COL_NV = _COL_NK + _NA_W
_COL_LR = _COL_NV + _NA_W
_PROJ_TN = 1280
_PROJ_COLS = -(-(_COL_LR + 2 * GLA_RANK) // _PROJ_TN) * _PROJ_TN
_ROW_TILE = 688
_OUT_ROW_TILE = 344
_FFN_TF = 512


def _fused_w_in(w_in):
    depth, d, _ = w_in.shape
    lr0 = _COL_GR + _GLA_VR
    lr1 = lr0 + 2 * GLA_RANK
    pad = jnp.zeros((depth, d, _PROJ_COLS - _COL_LR - 2 * GLA_RANK), BF16)
    cast = lambda a: a.astype(BF16)
    return jnp.concatenate([cast(w_in[:, :, :lr0]), cast(w_in[:, :, lr1:]), cast(w_in[:, :, lr0:lr1]), pad],
                           axis=2)


def _gla_gate_params(w_up, b_up):
    depth = w_up.shape[0]
    wc = jnp.zeros((depth, GLA_HEADS, 128, 2 * GLA_DK), F32)
    for z in range(2):
        w = w_up[:, z].astype(F32).reshape(depth, GLA_RANK, GLA_HEADS, GLA_DK).transpose(0, 2, 1, 3)
        wc = wc.at[:, :, z * GLA_RANK:(z + 1) * GLA_RANK, z * GLA_DK:(z + 1) * GLA_DK].set(w)
    bc = b_up.astype(F32).reshape(depth, 2, GLA_HEADS, GLA_DK).transpose(0, 2, 1, 3)
    return wc.astype(BF16), bc.reshape(depth, GLA_HEADS, 1, 2 * GLA_DK)


def kernel(x, meta, w_in, gla_w_up, gla_b_up, gla_norm_w, na_rel_bias, w_out, ln1_w, ln1_b,
           w_ff1, w_ff2, ln2_w, ln2_b):
    bsz, seq, d = x.shape
    depth = w_in.shape[0]
    alpha = (2 * depth) ** 0.25
    seq_len = seq + N_META
    m = bsz * seq_len
    meta_b = jnp.broadcast_to(meta.astype(x.dtype)[None], (bsz, N_META, d))
    h = jnp.concatenate([meta_b, x], axis=1).reshape(m, d).astype(F32)
    hb = h.astype(BF16)
    w_in_b = _fused_w_in(w_in)
    w_out_b, w_ff1_b, w_ff2_b = w_out.astype(BF16), w_ff1.astype(BF16), w_ff2.astype(BF16)
    wc, bc = _gla_gate_params(gla_w_up, gla_b_up)
    nw = gla_norm_w.astype(F32).reshape(depth, 1, GLA_DV)
    table = _na_bias_table(na_rel_bias)
    vec = lambda p: p.astype(F32).reshape(depth, 1, d)
    ln1_w, ln1_b, ln2_w, ln2_b = vec(ln1_w), vec(ln1_b), vec(ln2_w), vec(ln2_b)
    for l in range(depth):
        proj = _proj(hb, w_in_b, l, _ROW_TILE, _PROJ_TN).reshape(bsz, seq_len, _PROJ_COLS)
        y_gla = _gla(proj, wc, bc, nw, l).reshape(m, GLA_HEADS * GLA_DV)
        y_na = _na(proj, table, l).reshape(m, NA_HEADS * NA_DH)
        h, hb = _out_ln(y_gla, y_na, w_out_b, h, ln1_w, ln1_b, l, alpha, _OUT_ROW_TILE)
        h, hb = _ffn_ln(hb, w_ff1_b, w_ff2_b, h, ln2_w, ln2_b, l, alpha, _ROW_TILE, _FFN_TF)
    return h.reshape(bsz, seq_len, d)[:, N_META:].astype(x.dtype)
```

```python
import functools

import numpy as np
import jax
import jax.numpy as jnp
from jax import lax
from jax.experimental import pallas as pl
from jax.experimental.pallas import tpu as pltpu

N_META = 16
GRID_W = 64
GLA_HEADS = 4
GLA_DK = 128
GLA_DV = 256
GLA_RANK = 16
GLA_TAU = 16.0
_GLA_BLOCK = 256
_GLA_SAFE_RANGE = 80.0
NA_HEADS = 16
NA_DH = 64
NA_WIN_H = 8
NA_WIN_W = 16
NA_HEAD_GROUP = 4
LN_EPS = 1e-5
MASK_VALUE = -1e30

V7X_VMEM_LIMIT_BYTES = 58 * 1024 * 1024

F32 = jnp.float32
BF16 = jnp.bfloat16

_NT = (((1,), (1,)), ((), ()))
_TN = (((0,), (0,)), ((), ()))


def _dot(a, b, dims=None):
    if dims is None:
        return jnp.dot(a, b, preferred_element_type=F32)
    return lax.dot_general(a, b, dims, preferred_element_type=F32)


def _proj_kernel(a_ref, w_ref, o_ref, wb_ref):
    @pl.when(pl.program_id(1) == 0)
    def _():
        wb_ref[...] = w_ref[...].astype(BF16)

    o_ref[...] = _dot(a_ref[...], wb_ref[...]).astype(o_ref.dtype)


def _proj(a, w, layer, tm, tn):
    m, k = a.shape
    n = w.shape[2]
    return pl.pallas_call(
        _proj_kernel,
        grid=(pl.cdiv(n, tn), m // tm),
        in_specs=[pl.BlockSpec((tm, k), lambda j, i: (i, 0)),
                  pl.BlockSpec((None, k, tn), lambda j, i: (layer, 0, j))],
        out_specs=pl.BlockSpec((tm, tn), lambda j, i: (i, j)),
        out_shape=jax.ShapeDtypeStruct((m, n), BF16),
        scratch_shapes=[pltpu.VMEM((k, tn), BF16)],
        compiler_params=pltpu.CompilerParams(
            dimension_semantics=("arbitrary", "arbitrary"),
            vmem_limit_bytes=V7X_VMEM_LIMIT_BYTES),
        name="proj",
    )(a, w)


def _gla_decays(rows, c, g_ref):
    g2 = g_ref[rows, :]
    t_idx = lax.broadcasted_iota(jnp.int32, (c, c), 0)
    s_idx = lax.broadcasted_iota(jnp.int32, (c, c), 1)
    tri = jnp.where(s_idx <= t_idx, 1.0, 0.0).astype(BF16)
    g_hi = g2.astype(BF16)
    g_lo = (g2 - g_hi.astype(F32)).astype(BF16)
    p = _dot(tri, g_hi) + _dot(tri, g_lo)
    cb = p[c - 1:c, GLA_DK:] - p[:, GLA_DK:] + g2[:, GLA_DK:]
    return p[:, :GLA_DK], cb


def _gla_chunk_terms(rows, ci, size, q_ref, k_ref, v_ref, g_ref, qi_ref, oi_ref, ut_ref, e_ref, rng_ref):
    c = size
    bf, cb = _gla_decays(rows, c, g_ref)
    t_idx = lax.broadcasted_iota(jnp.int32, (c, c), 0)
    s_idx = lax.broadcasted_iota(jnp.int32, (c, c), 1)
    lower = s_idx <= t_idx
    upper = s_idx >= t_idx
    bf_end = bf[c - 1:c, :]
    cb_end = cb[0:1, :]
    bf_mid = bf[c // 2 - 1:c // 2, :]
    cb_mid = cb[c // 2:c // 2 + 1, :]
    rng_ref[...] = jnp.maximum(rng_ref[...], jnp.broadcast_to(jnp.maximum(-bf_end, -cb_end), rng_ref.shape))
    q = q_ref[0, rows, :].astype(F32) * (GLA_DK ** -0.5)
    k = k_ref[0, rows, :].astype(F32)
    v = v_ref[0, rows, :]
    qi_f = q * jnp.exp(bf)
    ks_f = k * jnp.exp(bf_end - bf)
    qi_b = q * jnp.exp(cb)
    ks_b = k * jnp.exp(cb_end - cb)
    sc_f = _dot((qi_f * jnp.exp(-bf_mid)).astype(BF16), (ks_f * jnp.exp(bf_mid - bf_end)).astype(BF16), _NT)
    sc_b = _dot((qi_b * jnp.exp(-cb_mid)).astype(BF16), (ks_b * jnp.exp(cb_mid - cb_end)).astype(BF16), _NT)
    sc = jnp.where(lower, sc_f, 0.0) + jnp.where(upper, sc_b, 0.0)
    oi_ref[rows, :] = _dot(sc.astype(BF16), v)
    qi_ref[rows, :] = jnp.concatenate([qi_f, qi_b], axis=-1).astype(BF16)
    ut_ref[ci] = _dot(v, jnp.concatenate([ks_f, ks_b], axis=-1).astype(BF16), _TN)
    e_ref[ci] = jnp.broadcast_to(jnp.exp(jnp.concatenate([bf_end, cb_end], axis=-1)), (8, 2 * GLA_DK))


def _gla_exact_intra(rows, size, q_ref, k_ref, v_ref, g_ref, oi_ref):
    c = size
    bf, cb = _gla_decays(rows, c, g_ref)
    q = q_ref[0, rows, :].astype(F32) * (GLA_DK ** -0.5)
    k = k_ref[0, rows, :].astype(F32)
    row = lax.broadcasted_iota(jnp.int32, (c, 1), 0)
    lane = lax.broadcasted_iota(jnp.int32, (c, c), 1)

    def body(t, sct):
        pick = row == t
        take = lambda a: jnp.sum(jnp.where(pick, a, 0.0), axis=0, keepdims=True)
        dec = (jnp.exp(jnp.where(row <= t, take(bf) - bf, MASK_VALUE))
               + jnp.exp(jnp.where(row >= t, take(cb) - cb, MASK_VALUE)))
        col = jnp.sum(k * take(q) * dec, axis=-1, keepdims=True)
        return jnp.where(lane == t, col, sct)

    sct = lax.fori_loop(0, c, body, jnp.zeros((c, c), F32))
    oi_ref[rows, :] = _dot(sct.astype(BF16), v_ref[0, rows, :], _TN)


def _gla_chunk_out(rows, ci, r_ref, nw_ref, qi_ref, oi_ref, st_ref, o_ref):
    o = oi_ref[rows, :] + _dot(qi_ref[rows, :], st_ref[ci], _NT)
    o = o * lax.rsqrt(jnp.mean(o * o, axis=-1, keepdims=True) + LN_EPS) * nw_ref[...]
    r = r_ref[0, rows, :].astype(F32)
    o_ref[0, rows, :] = (o * (r / (1.0 + jnp.exp(-r)))).astype(o_ref.dtype)


def _gla_kernel(q_ref, k_ref, v_ref, r_ref, lr_ref, wc_ref, bc_ref, nw_ref, o_ref,
                g_ref, qi_ref, oi_ref, ut_ref, e_ref, st_ref, rng_ref, *, seq_len, gate_rows):
    c = _GLA_BLOCK
    n_full = (seq_len - N_META) // c
    n_chunks = n_full + 1
    rng_ref[...] = jnp.zeros_like(rng_ref)

    for i in range(seq_len // gate_rows):
        rows = pl.ds(i * gate_rows, gate_rows)
        x = _dot(lr_ref[0, rows, :], wc_ref[0]) + bc_ref[0]
        g_ref[rows, :] = (jnp.minimum(x, 0.0) - jnp.log(1.0 + jnp.exp(-jnp.abs(x)))) * (1.0 / GLA_TAU)

    def chunk_rows(ci):
        return pl.ds(pl.multiple_of(N_META + (ci - 1) * c, 16), c)

    terms = functools.partial(_gla_chunk_terms, q_ref=q_ref, k_ref=k_ref, v_ref=v_ref, g_ref=g_ref,
                              qi_ref=qi_ref, oi_ref=oi_ref, ut_ref=ut_ref, e_ref=e_ref, rng_ref=rng_ref)
    terms(pl.ds(0, N_META), 0, N_META)

    def terms_body(ci, carry):
        terms(chunk_rows(ci), ci, c)
        return carry

    lax.fori_loop(1, n_chunks, terms_body, 0, unroll=2)

    @pl.when(jnp.max(rng_ref[...]) > _GLA_SAFE_RANGE)
    def _():
        exact = functools.partial(_gla_exact_intra, q_ref=q_ref, k_ref=k_ref, v_ref=v_ref, g_ref=g_ref,
                                  oi_ref=oi_ref)
        exact(pl.ds(0, N_META), N_META)

        def exact_body(ci, carry):
            exact(chunk_rows(ci), c)
            return carry

        lax.fori_loop(1, n_chunks, exact_body, 0)

    def scan_body(i, carry):
        s_f, s_b = carry
        j = n_chunks - 1 - i
        st_ref[i, :, :GLA_DK] = s_f.astype(BF16)
        st_ref[j, :, GLA_DK:] = s_b.astype(BF16)
        s_f = s_f * e_ref[i, 0:1, :GLA_DK] + ut_ref[i, :, :GLA_DK]
        s_b = s_b * e_ref[j, 0:1, GLA_DK:] + ut_ref[j, :, GLA_DK:]
        return s_f, s_b

    zero = jnp.zeros((GLA_DV, GLA_DK), F32)
    lax.fori_loop(0, n_chunks, scan_body, (zero, zero))

    out = functools.partial(_gla_chunk_out, r_ref=r_ref, nw_ref=nw_ref, qi_ref=qi_ref, oi_ref=oi_ref,
                            st_ref=st_ref, o_ref=o_ref)
    out(pl.ds(0, N_META), 0)

    def out_body(ci, carry):
        out(chunk_rows(ci), ci)
        return carry

    lax.fori_loop(1, n_chunks, out_body, 0, unroll=2)


def _gla(proj, wc, bc, nw, layer):
    bsz, seq_len, _ = proj.shape
    assert (seq_len - N_META) % _GLA_BLOCK == 0 and seq_len % _ROW_TILE == 0
    n_chunks = (seq_len - N_META) // _GLA_BLOCK + 1
    kern = functools.partial(_gla_kernel, seq_len=seq_len, gate_rows=_ROW_TILE)
    qk_blk = lambda off: pl.BlockSpec((1, seq_len, GLA_DK), lambda b, h: (b, 0, off + h))
    vr_blk = lambda off: pl.BlockSpec((1, seq_len, GLA_DV), lambda b, h: (b, 0, off + h))
    return pl.pallas_call(
        kern,
        grid=(bsz, GLA_HEADS),
        in_specs=[qk_blk(_COL_GQ // GLA_DK), qk_blk(_COL_GK // GLA_DK),
                  vr_blk(_COL_GV // GLA_DV), vr_blk(_COL_GR // GLA_DV),
                  pl.BlockSpec((1, seq_len, 128), lambda b, h: (b, 0, _COL_LR // 128)),
                  pl.BlockSpec((None, 1, 128, 2 * GLA_DK), lambda b, h: (layer, h, 0, 0)),
                  pl.BlockSpec((None, 1, 1, 2 * GLA_DK), lambda b, h: (layer, h, 0, 0)),
                  pl.BlockSpec((None, 1, GLA_DV), lambda b, h: (layer, 0, 0))],
        out_specs=pl.BlockSpec((1, seq_len, GLA_DV), lambda b, h: (b, 0, h)),
        out_shape=jax.ShapeDtypeStruct((bsz, seq_len, GLA_HEADS * GLA_DV), BF16),
        scratch_shapes=[pltpu.VMEM((seq_len, 2 * GLA_DK), F32),
                        pltpu.VMEM((seq_len, 2 * GLA_DK), BF16),
                        pltpu.VMEM((seq_len, GLA_DV), F32),
                        pltpu.VMEM((n_chunks, GLA_DV, 2 * GLA_DK), F32),
                        pltpu.VMEM((n_chunks, 8, 2 * GLA_DK), F32),
                        pltpu.VMEM((n_chunks, GLA_DV, 2 * GLA_DK), BF16),
                        pltpu.VMEM((8, GLA_DK), F32)],
        compiler_params=pltpu.CompilerParams(
            dimension_semantics=("arbitrary", "arbitrary"),
            vmem_limit_bytes=V7X_VMEM_LIMIT_BYTES),
        name="gla",
    )(proj, proj, proj, proj, proj, wc, bc, nw)


def _na_kernel(q0, q1, q2, k0, k1, k2, v0, v1, v2, t_ref, o_ref, q_ref, k_ref, v_ref, *, seq_len, shift,
               row_tile):
    def realign(blocks, dst, scale):
        for i in range(seq_len // row_tile):
            rws = pl.ds(i * row_tile, row_tile)
            x = jnp.concatenate([pltpu.bitcast(b[0, rws, :], jnp.uint32) for b in blocks], axis=-1)
            x = pltpu.bitcast(x[:, shift:shift + dst.shape[1]], BF16)
            dst[rws, :] = x if scale is None else x * scale

    realign((q0, q1, q2), q_ref, NA_DH ** -0.5)
    realign((k0, k1, k2), k_ref, None)
    realign((v0, v1, v2), v_ref, None)
    rows = (seq_len - N_META) // GRID_W
    hw = NA_HEAD_GROUP * NA_DH
    nq = NA_HEAD_GROUP * GRID_W
    n_keys = NA_WIN_H * GRID_W
    meta_blk = 128

    def block_diag(q4):
        n = q4.shape[0]
        qrep = jnp.concatenate([q4] * NA_HEAD_GROUP, axis=0)
        rb = lax.broadcasted_iota(jnp.int32, (NA_HEAD_GROUP * n, hw), 0) // n
        lb = lax.broadcasted_iota(jnp.int32, (NA_HEAD_GROUP * n, hw), 1) // NA_DH
        return jnp.where(rb == lb, qrep, jnp.zeros_like(qrep))

    def pick_diag(o, n):
        lb = lax.broadcasted_iota(jnp.int32, (n, hw), 1) // NA_DH
        out = jnp.zeros((n, hw), F32)
        for h in range(NA_HEAD_GROUP):
            out = jnp.where(lb == h, o[h * n:(h + 1) * n, :], out)
        return out

    k_meta = k_ref[0:meta_blk, :]
    v_meta = v_ref[0:meta_blk, :]

    def meta_scores(qbd):
        s = _dot(qbd, k_meta, _NT)
        lane = lax.broadcasted_iota(jnp.int32, s.shape, 1)
        return jnp.where(lane < N_META, s, MASK_VALUE)

    qm = block_diag(q_ref[0:N_META, :])
    sm = meta_scores(qm)
    pm = jnp.exp(sm - jnp.max(sm, axis=-1, keepdims=True))
    om = _dot(pm.astype(BF16), v_meta) / jnp.sum(pm, axis=-1, keepdims=True)
    o_ref[0, 0:N_META, :] = pick_diag(om, N_META).astype(o_ref.dtype)

    def row_body(r, carry):
        rs = jnp.clip(r - NA_WIN_H // 2, 0, rows - NA_WIN_H)
        delta = r - rs
        q_start = pl.multiple_of(N_META + r * GRID_W, 16)
        k_start = pl.multiple_of(N_META + rs * GRID_W, 16)
        qbd = block_diag(q_ref[pl.ds(q_start, GRID_W), :])
        kw = k_ref[pl.ds(k_start, n_keys), :]
        vw = v_ref[pl.ds(k_start, n_keys), :]
        sw = _dot(qbd, kw, _NT)
        bias = jnp.concatenate(
            [t_ref[2 * j - delta + NA_WIN_H - 1] for j in range(NA_WIN_H // 2)], axis=-1)
        sw = sw + bias
        smeta = meta_scores(qbd)
        blocks = lambda a: [a[:, i:i + meta_blk] for i in range(0, n_keys, meta_blk)]
        mx = jnp.max(functools.reduce(jnp.maximum, blocks(sw), smeta), axis=-1, keepdims=True)
        pw = jnp.exp(sw - mx)
        pmeta = jnp.exp(smeta - mx)
        den = jnp.sum(functools.reduce(jnp.add, blocks(pw), pmeta), axis=-1, keepdims=True)
        o = (_dot(pw.astype(BF16), vw) + _dot(pmeta.astype(BF16), v_meta)) / den
        o_ref[0, pl.ds(q_start, GRID_W), :] = pick_diag(o, GRID_W).astype(o_ref.dtype)
        return carry

    lax.fori_loop(0, rows, row_body, 0, unroll=4)


def _na_bias_table(rel_bias):
    depth = rel_bias.shape[0]
    col = np.arange(GRID_W)
    cs = np.clip(col - NA_WIN_W // 2, 0, GRID_W - NA_WIN_W)
    inside = (col[None, :] >= cs[:, None]) & (col[None, :] < cs[:, None] + NA_WIN_W)
    dc = col[None, :] - col[:, None] + NA_WIN_W - 1
    onehot = (np.arange(2 * NA_WIN_W - 1)[:, None, None] == dc[None]) & inside[None]
    t = jnp.einsum("lhdj,jqk->lhdqk", rel_bias.astype(F32), jnp.asarray(onehot, F32),
                   precision=lax.Precision.HIGHEST)
    t = jnp.where(inside, t, MASK_VALUE)
    t = jnp.concatenate([t[:, :, :-1], t[:, :, 1:]], axis=-1)
    n_dr = t.shape[2]
    t = t.reshape(depth, NA_HEADS // NA_HEAD_GROUP, NA_HEAD_GROUP, n_dr, GRID_W, 2 * GRID_W)
    return t.transpose(0, 1, 3, 2, 4, 5).reshape(depth, NA_HEADS // NA_HEAD_GROUP, n_dr,
                                                  NA_HEAD_GROUP * GRID_W, 2 * GRID_W)


def _na(proj, table, layer):
    bsz, seq_len, _ = proj.shape
    hw = NA_HEAD_GROUP * NA_DH
    n_groups = NA_HEADS // NA_HEAD_GROUP
    lane = 128
    shift = _COL_NQ % lane
    assert _COL_NK % lane == shift and _COL_NV % lane == shift and hw % lane == 0 and seq_len % _ROW_TILE == 0
    kern = functools.partial(_na_kernel, seq_len=seq_len, shift=shift, row_tile=_ROW_TILE)

    def lane_blocks(col):
        first = col // lane
        return [pl.BlockSpec((1, seq_len, lane), lambda g, b, j=j: (b, 0, first + g * (hw // lane) + j))
                for j in range(hw // lane + 1)]

    return pl.pallas_call(
        kern,
        grid=(n_groups, bsz),
        in_specs=lane_blocks(_COL_NQ) + lane_blocks(_COL_NK) + lane_blocks(_COL_NV)
        + [pl.BlockSpec((None, None) + table.shape[2:], lambda g, b: (layer, g, 0, 0, 0))],
        out_specs=pl.BlockSpec((1, seq_len, hw), lambda g, b: (b, 0, g)),
        out_shape=jax.ShapeDtypeStruct((bsz, seq_len, NA_HEADS * NA_DH), BF16),
        scratch_shapes=[pltpu.VMEM((seq_len, hw), BF16)] * 3,
        compiler_params=pltpu.CompilerParams(
            dimension_semantics=("arbitrary", "arbitrary"),
            vmem_limit_bytes=V7X_VMEM_LIMIT_BYTES),
        name="na",
    )(*([proj] * 9), table)


def _layer_norm(z, w, b):
    mu = jnp.mean(z, axis=-1, keepdims=True)
    zc = z - mu
    var = jnp.mean(zc * zc, axis=-1, keepdims=True)
    return zc * lax.rsqrt(var + LN_EPS) * w + b


def _out_ln_kernel(yg_ref, yn_ref, w_ref, h_ref, lw_ref, lb_ref, o_ref, ob_ref, *, alpha):
    kg = yg_ref.shape[1]
    mix = _dot(yg_ref[...], w_ref[0:kg, :]) + _dot(yn_ref[...], w_ref[kg:, :])
    o = _layer_norm(alpha * h_ref[...] + mix, lw_ref[...], lb_ref[...])
    o_ref[...] = o
    ob_ref[...] = o.astype(ob_ref.dtype)


def _out_ln(yg, yn, w, h, lw, lb, layer, alpha, tm):
    m, kg = yg.shape
    kn = yn.shape[1]
    n = w.shape[2]
    row = lambda width: pl.BlockSpec((tm, width), lambda i: (i, 0))
    per_layer = lambda shape: pl.BlockSpec((None,) + shape, lambda i: (layer, 0, 0))
    return pl.pallas_call(
        functools.partial(_out_ln_kernel, alpha=alpha),
        grid=(m // tm,),
        in_specs=[row(kg), row(kn), per_layer((kg + kn, n)), row(n), per_layer((1, n)), per_layer((1, n))],
        out_specs=[row(n), row(n)],
        out_shape=[jax.ShapeDtypeStruct((m, n), F32), jax.ShapeDtypeStruct((m, n), BF16)],
        compiler_params=pltpu.CompilerParams(
            dimension_semantics=("arbitrary",),
            vmem_limit_bytes=V7X_VMEM_LIMIT_BYTES),
        name="out_ln",
    )(yg, yn, w, h, lw, lb)


def _ffn_ln_kernel(a_ref, w1_ref, w2_ref, h_ref, lw_ref, lb_ref, o_ref, ob_ref, *, alpha):
    f = pl.program_id(1)

    @pl.when(f == 0)
    def _():
        o_ref[...] = alpha * h_ref[...]

    u = jnp.square(jnp.maximum(_dot(a_ref[...], w1_ref[...]), 0.0)).astype(BF16)
    o_ref[...] += _dot(u, w2_ref[...])

    @pl.when(f == pl.num_programs(1) - 1)
    def _():
        o = _layer_norm(o_ref[...], lw_ref[...], lb_ref[...])
        o_ref[...] = o
        ob_ref[...] = o.astype(ob_ref.dtype)


def _ffn_ln(a, w1, w2, h, lw, lb, layer, alpha, tm, tf):
    m, d = a.shape
    d_ff = w1.shape[2]
    row = pl.BlockSpec((tm, d), lambda i, f: (i, 0))
    vec = pl.BlockSpec((None, 1, d), lambda i, f: (layer, 0, 0))
    return pl.pallas_call(
        functools.partial(_ffn_ln_kernel, alpha=alpha),
        grid=(m // tm, d_ff // tf),
        in_specs=[row, pl.BlockSpec((None, d, tf), lambda i, f: (layer, 0, f)),
                  pl.BlockSpec((None, tf, d), lambda i, f: (layer, f, 0)), row, vec, vec],
        out_specs=[row, row],
        out_shape=[jax.ShapeDtypeStruct((m, d), F32), jax.ShapeDtypeStruct((m, d), BF16)],
        compiler_params=pltpu.CompilerParams(
            dimension_semantics=("arbitrary", "arbitrary"),
            vmem_limit_bytes=V7X_VMEM_LIMIT_BYTES),
        name="ffn_ln",
    )(a, w1, w2, h, lw, lb)


_GLA_QK = GLA_HEADS * GLA_DK
_GLA_VR = GLA_HEADS * GLA_DV
_NA_W = NA_HEADS * NA_DH
_COL_GQ = 0
_COL_GK = _COL_GQ + _GLA_QK
_COL_GV = _COL_GK + _GLA_QK
_COL_GR = _COL_GV + _GLA_VR
_COL_LR = _COL_GR + _GLA_VR
_COL_NQ = _COL_LR + 2 * GLA_RANK
_COL_NK = _COL_NQ + _NA_W
_COL_NV = _COL_NK + _NA_W
_PROJ_COLS = _COL_NV + _NA_W
_PROJ_TN = 1280
_ROW_TILE = 688
_OUT_ROW_TILE = 344
_FFN_TF = 512


def _gla_gate_params(w_up, b_up):
    depth = w_up.shape[0]
    wc = jnp.zeros((depth, GLA_HEADS, 128, 2 * GLA_DK), F32)
    for z in range(2):
        w = w_up[:, z].astype(F32).reshape(depth, GLA_RANK, GLA_HEADS, GLA_DK).transpose(0, 2, 1, 3)
        wc = wc.at[:, :, z * GLA_RANK:(z + 1) * GLA_RANK, z * GLA_DK:(z + 1) * GLA_DK].set(w)
    bc = b_up.astype(F32).reshape(depth, 2, GLA_HEADS, GLA_DK).transpose(0, 2, 1, 3)
    return wc.astype(BF16), bc.reshape(depth, GLA_HEADS, 1, 2 * GLA_DK)


def kernel(x, meta, w_in, gla_w_up, gla_b_up, gla_norm_w, na_rel_bias, w_out, ln1_w, ln1_b,
           w_ff1, w_ff2, ln2_w, ln2_b):
    bsz, seq, d = x.shape
    depth = w_in.shape[0]
    alpha = (2 * depth) ** 0.25
    seq_len = seq + N_META
    m = bsz * seq_len
    meta_b = jnp.broadcast_to(meta.astype(x.dtype)[None], (bsz, N_META, d))
    h = jnp.concatenate([meta_b, x], axis=1).reshape(m, d).astype(F32)
    hb = h.astype(BF16)
    assert w_in.shape[2] == _PROJ_COLS
    w_out_b, w_ff1_b, w_ff2_b = w_out.astype(BF16), w_ff1.astype(BF16), w_ff2.astype(BF16)
    wc, bc = _gla_gate_params(gla_w_up, gla_b_up)
    nw = gla_norm_w.astype(F32).reshape(depth, 1, GLA_DV)
    table = _na_bias_table(na_rel_bias)
    vec = lambda p: p.astype(F32).reshape(depth, 1, d)
    ln1_w, ln1_b, ln2_w, ln2_b = vec(ln1_w), vec(ln1_b), vec(ln2_w), vec(ln2_b)
    for l in range(depth):
        proj = _proj(hb, w_in, l, _ROW_TILE, _PROJ_TN).reshape(bsz, seq_len, _PROJ_COLS)
        y_gla = _gla(proj, wc, bc, nw, l).reshape(m, GLA_HEADS * GLA_DV)
        y_na = _na(proj, table, l).reshape(m, NA_HEADS * NA_DH)
        h, hb = _out_ln(y_gla, y_na, w_out_b, h, ln1_w, ln1_b, l, alpha, _OUT_ROW_TILE)
        h, hb = _ffn_ln(hb, w_ff1_b, w_ff2_b, h, ln2_w, ln2_b, l, alpha, _ROW_TILE, _FFN_TF)
    return h.reshape(bsz, seq_len, d)[:, N_META:].astype(x.dtype)
```

```python
import functools

import numpy as np
import jax
import jax.numpy as jnp
from jax import lax
from jax.experimental import pallas as pl
from jax.experimental.pallas import tpu as pltpu

N_META = 16
GRID_W = 64
GLA_HEADS = 4
GLA_DK = 128
GLA_DV = 256
GLA_RANK = 16
GLA_TAU = 16.0
_GLA_BLOCK = 256
_GLA_SAFE_RANGE = 80.0
NA_HEADS = 16
NA_DH = 64
NA_WIN_H = 8
NA_WIN_W = 16
NA_HEAD_GROUP = 4
LN_EPS = 1e-5
MASK_VALUE = -1e30

V7X_VMEM_LIMIT_BYTES = 58 * 1024 * 1024

F32 = jnp.float32
BF16 = jnp.bfloat16

_NT = (((1,), (1,)), ((), ()))
_TN = (((0,), (0,)), ((), ()))


def _dot(a, b, dims=None):
    if dims is None:
        return jnp.dot(a, b, preferred_element_type=F32)
    return lax.dot_general(a, b, dims, preferred_element_type=F32)


def _proj_kernel(a_ref, w_ref, o_ref, wb_ref):
    @pl.when(pl.program_id(1) == 0)
    def _():
        wb_ref[...] = w_ref[...].astype(BF16)

    o_ref[...] = _dot(a_ref[...], wb_ref[...], _NT).astype(o_ref.dtype)


def _proj(a, wt, layer, n_tiles, tn, first, skip_from, skip, tm, name):
    m, k = a.shape
    assert first % 16 == 0 and tn % 16 == 0 and skip % 16 == 0

    def w_index(j, i):
        return (layer, pl.multiple_of(first + j * tn + jnp.where(j >= skip_from, skip, 0), 16), 0)

    return pl.pallas_call(
        _proj_kernel,
        grid=(n_tiles, m // tm),
        in_specs=[pl.BlockSpec((tm, k), lambda j, i: (i, 0)),
                  pl.BlockSpec((pl.Squeezed(), pl.Element(tn), pl.Element(k)), w_index)],
        out_specs=pl.BlockSpec((tm, tn), lambda j, i: (i, j)),
        out_shape=jax.ShapeDtypeStruct((m, n_tiles * tn), BF16),
        scratch_shapes=[pltpu.VMEM((tn, k), BF16)],
        compiler_params=pltpu.CompilerParams(
            dimension_semantics=("arbitrary", "arbitrary"),
            vmem_limit_bytes=V7X_VMEM_LIMIT_BYTES),
        name=name,
    )(a, wt)


def _gla_decays(rows, c, g_ref):
    g2 = g_ref[rows, :]
    t_idx = lax.broadcasted_iota(jnp.int32, (c, c), 0)
    s_idx = lax.broadcasted_iota(jnp.int32, (c, c), 1)
    tri = jnp.where(s_idx <= t_idx, 1.0, 0.0).astype(BF16)
    g_hi = g2.astype(BF16)
    g_lo = (g2 - g_hi.astype(F32)).astype(BF16)
    p = _dot(tri, g_hi) + _dot(tri, g_lo)
    cb = p[c - 1:c, GLA_DK:] - p[:, GLA_DK:] + g2[:, GLA_DK:]
    return p[:, :GLA_DK], cb


def _gla_chunk_terms(rows, ci, size, q_ref, k_ref, v_ref, g_ref, qi_ref, oi_ref, ut_ref, e_ref, rng_ref):
    c = size
    bf, cb = _gla_decays(rows, c, g_ref)
    t_idx = lax.broadcasted_iota(jnp.int32, (c, c), 0)
    s_idx = lax.broadcasted_iota(jnp.int32, (c, c), 1)
    lower = s_idx <= t_idx
    upper = s_idx >= t_idx
    bf_end = bf[c - 1:c, :]
    cb_end = cb[0:1, :]
    bf_mid = bf[c // 2 - 1:c // 2, :]
    cb_mid = cb[c // 2:c // 2 + 1, :]
    rng_ref[...] = jnp.maximum(rng_ref[...], jnp.broadcast_to(jnp.maximum(-bf_end, -cb_end), rng_ref.shape))
    q = q_ref[0, rows, :].astype(F32) * (GLA_DK ** -0.5)
    k = k_ref[0, rows, :].astype(F32)
    v = v_ref[0, rows, :]
    qi_f = q * jnp.exp(bf)
    ks_f = k * jnp.exp(bf_end - bf)
    qi_b = q * jnp.exp(cb)
    ks_b = k * jnp.exp(cb_end - cb)
    sc_f = _dot((qi_f * jnp.exp(-bf_mid)).astype(BF16), (ks_f * jnp.exp(bf_mid - bf_end)).astype(BF16), _NT)
    sc_b = _dot((qi_b * jnp.exp(-cb_mid)).astype(BF16), (ks_b * jnp.exp(cb_mid - cb_end)).astype(BF16), _NT)
    sc = jnp.where(lower, sc_f, 0.0) + jnp.where(upper, sc_b, 0.0)
    oi_ref[rows, :] = _dot(sc.astype(BF16), v)
    qi_ref[rows, :] = jnp.concatenate([qi_f, qi_b], axis=-1).astype(BF16)
    ut_ref[ci] = _dot(v, jnp.concatenate([ks_f, ks_b], axis=-1).astype(BF16), _TN)
    e_ref[ci] = jnp.broadcast_to(jnp.exp(jnp.concatenate([bf_end, cb_end], axis=-1)), (8, 2 * GLA_DK))


def _gla_exact_intra(rows, size, q_ref, k_ref, v_ref, g_ref, oi_ref):
    c = size
    bf, cb = _gla_decays(rows, c, g_ref)
    q = q_ref[0, rows, :].astype(F32) * (GLA_DK ** -0.5)
    k = k_ref[0, rows, :].astype(F32)
    row = lax.broadcasted_iota(jnp.int32, (c, 1), 0)
    lane = lax.broadcasted_iota(jnp.int32, (c, c), 1)

    def body(t, sct):
        pick = row == t
        take = lambda a: jnp.sum(jnp.where(pick, a, 0.0), axis=0, keepdims=True)
        dec = (jnp.exp(jnp.where(row <= t, take(bf) - bf, MASK_VALUE))
               + jnp.exp(jnp.where(row >= t, take(cb) - cb, MASK_VALUE)))
        col = jnp.sum(k * take(q) * dec, axis=-1, keepdims=True)
        return jnp.where(lane == t, col, sct)

    sct = lax.fori_loop(0, c, body, jnp.zeros((c, c), F32))
    oi_ref[rows, :] = _dot(sct.astype(BF16), v_ref[0, rows, :], _TN)


def _gla_chunk_out(rows, ci, r_ref, nw_ref, qi_ref, oi_ref, st_ref, o_ref):
    o = oi_ref[rows, :] + _dot(qi_ref[rows, :], st_ref[ci], _NT)
    o = o * lax.rsqrt(jnp.mean(o * o, axis=-1, keepdims=True) + LN_EPS) * nw_ref[...]
    r = r_ref[0, rows, :].astype(F32)
    o_ref[0, rows, :] = (o * (r / (1.0 + jnp.exp(-r)))).astype(o_ref.dtype)


def _gla_kernel(q_ref, k_ref, v_ref, r_ref, lr_ref, wc_ref, bc_ref, nw_ref, o_ref,
                g_ref, qi_ref, oi_ref, ut_ref, e_ref, st_ref, rng_ref, *, seq_len, gate_rows):
    c = _GLA_BLOCK
    n_full = (seq_len - N_META) // c
    n_chunks = n_full + 1
    rng_ref[...] = jnp.zeros_like(rng_ref)

    for i in range(seq_len // gate_rows):
        rows = pl.ds(i * gate_rows, gate_rows)
        x = _dot(lr_ref[0, rows, :], wc_ref[0]) + bc_ref[0]
        g_ref[rows, :] = (jnp.minimum(x, 0.0) - jnp.log(1.0 + jnp.exp(-jnp.abs(x)))) * (1.0 / GLA_TAU)

    def chunk_rows(ci):
        return pl.ds(pl.multiple_of(N_META + (ci - 1) * c, 16), c)

    terms = functools.partial(_gla_chunk_terms, q_ref=q_ref, k_ref=k_ref, v_ref=v_ref, g_ref=g_ref,
                              qi_ref=qi_ref, oi_ref=oi_ref, ut_ref=ut_ref, e_ref=e_ref, rng_ref=rng_ref)
    terms(pl.ds(0, N_META), 0, N_META)

    def terms_body(ci, carry):
        terms(chunk_rows(ci), ci, c)
        return carry

    lax.fori_loop(1, n_chunks, terms_body, 0, unroll=2)

    @pl.when(jnp.max(rng_ref[...]) > _GLA_SAFE_RANGE)
    def _():
        exact = functools.partial(_gla_exact_intra, q_ref=q_ref, k_ref=k_ref, v_ref=v_ref, g_ref=g_ref,
                                  oi_ref=oi_ref)
        exact(pl.ds(0, N_META), N_META)

        def exact_body(ci, carry):
            exact(chunk_rows(ci), c)
            return carry

        lax.fori_loop(1, n_chunks, exact_body, 0)

    def scan_body(i, carry):
        s_f, s_b = carry
        j = n_chunks - 1 - i
        st_ref[i, :, :GLA_DK] = s_f.astype(BF16)
        st_ref[j, :, GLA_DK:] = s_b.astype(BF16)
        s_f = s_f * e_ref[i, 0:1, :GLA_DK] + ut_ref[i, :, :GLA_DK]
        s_b = s_b * e_ref[j, 0:1, GLA_DK:] + ut_ref[j, :, GLA_DK:]
        return s_f, s_b

    zero = jnp.zeros((GLA_DV, GLA_DK), F32)
    lax.fori_loop(0, n_chunks, scan_body, (zero, zero))

    out = functools.partial(_gla_chunk_out, r_ref=r_ref, nw_ref=nw_ref, qi_ref=qi_ref, oi_ref=oi_ref,
                            st_ref=st_ref, o_ref=o_ref)
    out(pl.ds(0, N_META), 0)

    def out_body(ci, carry):
        out(chunk_rows(ci), ci)
        return carry

    lax.fori_loop(1, n_chunks, out_body, 0, unroll=2)


def _gla(proj, lr, wc, bc, nw, layer):
    bsz, seq_len, _ = proj.shape
    assert (seq_len - N_META) % _GLA_BLOCK == 0 and seq_len % _ROW_TILE == 0
    n_chunks = (seq_len - N_META) // _GLA_BLOCK + 1
    kern = functools.partial(_gla_kernel, seq_len=seq_len, gate_rows=_ROW_TILE)
    qk_blk = lambda off: pl.BlockSpec((1, seq_len, GLA_DK), lambda b, h: (b, 0, off + h))
    vr_blk = lambda off: pl.BlockSpec((1, seq_len, GLA_DV), lambda b, h: (b, 0, off + h))
    return pl.pallas_call(
        kern,
        grid=(bsz, GLA_HEADS),
        in_specs=[qk_blk(_COL_GQ // GLA_DK), qk_blk(_COL_GK // GLA_DK),
                  vr_blk(_COL_GV // GLA_DV), vr_blk(_COL_GR // GLA_DV),
                  pl.BlockSpec((1, seq_len, 128), lambda b, h: (b, 0, 0)),
                  pl.BlockSpec((None, 1, 128, 2 * GLA_DK), lambda b, h: (layer, h, 0, 0)),
                  pl.BlockSpec((None, 1, 1, 2 * GLA_DK), lambda b, h: (layer, h, 0, 0)),
                  pl.BlockSpec((None, 1, GLA_DV), lambda b, h: (layer, 0, 0))],
        out_specs=pl.BlockSpec((1, seq_len, GLA_DV), lambda b, h: (b, 0, h)),
        out_shape=jax.ShapeDtypeStruct((bsz, seq_len, GLA_HEADS * GLA_DV), BF16),
        scratch_shapes=[pltpu.VMEM((seq_len, 2 * GLA_DK), F32),
                        pltpu.VMEM((seq_len, 2 * GLA_DK), BF16),
                        pltpu.VMEM((seq_len, GLA_DV), F32),
                        pltpu.VMEM((n_chunks, GLA_DV, 2 * GLA_DK), F32),
                        pltpu.VMEM((n_chunks, 8, 2 * GLA_DK), F32),
                        pltpu.VMEM((n_chunks, GLA_DV, 2 * GLA_DK), BF16),
                        pltpu.VMEM((8, GLA_DK), F32)],
        compiler_params=pltpu.CompilerParams(
            dimension_semantics=("arbitrary", "arbitrary"),
            vmem_limit_bytes=V7X_VMEM_LIMIT_BYTES),
        name="gla",
    )(proj, proj, proj, proj, lr, wc, bc, nw)


def _na_kernel(q_ref, k_ref, v_ref, t_ref, o_ref, *, seq_len):
    q_ref, k_ref, v_ref = q_ref.at[0], k_ref.at[0], v_ref.at[0]
    rows = (seq_len - N_META) // GRID_W
    hw = NA_HEAD_GROUP * NA_DH
    nq = NA_HEAD_GROUP * GRID_W
    n_keys = NA_WIN_H * GRID_W
    meta_blk = 128

    def block_diag(q4):
        n = q4.shape[0]
        qrep = jnp.concatenate([q4] * NA_HEAD_GROUP, axis=0)
        rb = lax.broadcasted_iota(jnp.int32, (NA_HEAD_GROUP * n, hw), 0) // n
        lb = lax.broadcasted_iota(jnp.int32, (NA_HEAD_GROUP * n, hw), 1) // NA_DH
        return jnp.where(rb == lb, qrep, jnp.zeros_like(qrep))

    def pick_diag(o, n):
        lb = lax.broadcasted_iota(jnp.int32, (n, hw), 1) // NA_DH
        out = jnp.zeros((n, hw), F32)
        for h in range(NA_HEAD_GROUP):
            out = jnp.where(lb == h, o[h * n:(h + 1) * n, :], out)
        return out

    k_meta = k_ref[0:meta_blk, :]
    v_meta = v_ref[0:meta_blk, :]

    def meta_scores(qbd):
        s = _dot(qbd, k_meta, _NT)
        lane = lax.broadcasted_iota(jnp.int32, s.shape, 1)
        return jnp.where(lane < N_META, s, MASK_VALUE)

    qm = block_diag(q_ref[0:N_META, :] * (NA_DH ** -0.5))
    sm = meta_scores(qm)
    pm = jnp.exp(sm - jnp.max(sm, axis=-1, keepdims=True))
    om = _dot(pm.astype(BF16), v_meta) / jnp.sum(pm, axis=-1, keepdims=True)
    o_ref[0, 0:N_META, :] = pick_diag(om, N_META).astype(o_ref.dtype)

    def row_body(r, carry):
        rs = jnp.clip(r - NA_WIN_H // 2, 0, rows - NA_WIN_H)
        delta = r - rs
        q_start = pl.multiple_of(N_META + r * GRID_W, 16)
        k_start = pl.multiple_of(N_META + rs * GRID_W, 16)
        qbd = block_diag(q_ref[pl.ds(q_start, GRID_W), :] * (NA_DH ** -0.5))
        kw = k_ref[pl.ds(k_start, n_keys), :]
        vw = v_ref[pl.ds(k_start, n_keys), :]
        sw = _dot(qbd, kw, _NT)
        bias = jnp.concatenate(
            [t_ref[2 * j - delta + NA_WIN_H - 1] for j in range(NA_WIN_H // 2)], axis=-1)
        sw = sw + bias
        smeta = meta_scores(qbd)
        blocks = lambda a: [a[:, i:i + meta_blk] for i in range(0, n_keys, meta_blk)]
        mx = jnp.max(functools.reduce(jnp.maximum, blocks(sw), smeta), axis=-1, keepdims=True)
        pw = jnp.exp(sw - mx)
        pmeta = jnp.exp(smeta - mx)
        den = jnp.sum(functools.reduce(jnp.add, blocks(pw), pmeta), axis=-1, keepdims=True)
        o = (_dot(pw.astype(BF16), vw) + _dot(pmeta.astype(BF16), v_meta)) / den
        o_ref[0, pl.ds(q_start, GRID_W), :] = pick_diag(o, GRID_W).astype(o_ref.dtype)
        return carry

    lax.fori_loop(0, rows, row_body, 0, unroll=4)


def _na_bias_table(rel_bias):
    depth = rel_bias.shape[0]
    col = np.arange(GRID_W)
    cs = np.clip(col - NA_WIN_W // 2, 0, GRID_W - NA_WIN_W)
    inside = (col[None, :] >= cs[:, None]) & (col[None, :] < cs[:, None] + NA_WIN_W)
    dc = col[None, :] - col[:, None] + NA_WIN_W - 1
    onehot = (np.arange(2 * NA_WIN_W - 1)[:, None, None] == dc[None]) & inside[None]
    t = jnp.einsum("lhdj,jqk->lhdqk", rel_bias.astype(F32), jnp.asarray(onehot, F32),
                   precision=lax.Precision.HIGHEST)
    t = jnp.where(inside, t, MASK_VALUE)
    t = jnp.concatenate([t[:, :, :-1], t[:, :, 1:]], axis=-1)
    n_dr = t.shape[2]
    t = t.reshape(depth, NA_HEADS // NA_HEAD_GROUP, NA_HEAD_GROUP, n_dr, GRID_W, 2 * GRID_W)
    return t.transpose(0, 1, 3, 2, 4, 5).reshape(depth, NA_HEADS // NA_HEAD_GROUP, n_dr,
                                                  NA_HEAD_GROUP * GRID_W, 2 * GRID_W)


def _na(proj, table, layer):
    bsz, seq_len, _ = proj.shape
    hw = NA_HEAD_GROUP * NA_DH
    n_groups = NA_HEADS // NA_HEAD_GROUP
    kern = functools.partial(_na_kernel, seq_len=seq_len)
    blk = lambda off: pl.BlockSpec((1, seq_len, hw), lambda g, b: (b, 0, off + g))
    return pl.pallas_call(
        kern,
        grid=(n_groups, bsz),
        in_specs=[blk(_COL_NQ // hw), blk(_COL_NK // hw), blk(_COL_NV // hw),
                  pl.BlockSpec((None, None) + table.shape[2:], lambda g, b: (layer, g, 0, 0, 0))],
        out_specs=pl.BlockSpec((1, seq_len, hw), lambda g, b: (b, 0, g)),
        out_shape=jax.ShapeDtypeStruct((bsz, seq_len, NA_HEADS * NA_DH), BF16),
        compiler_params=pltpu.CompilerParams(
            dimension_semantics=("arbitrary", "arbitrary"),
            vmem_limit_bytes=V7X_VMEM_LIMIT_BYTES),
        name="na",
    )(proj, proj, proj, table)


def _layer_norm(z, w, b):
    mu = jnp.mean(z, axis=-1, keepdims=True)
    zc = z - mu
    var = jnp.mean(zc * zc, axis=-1, keepdims=True)
    return zc * lax.rsqrt(var + LN_EPS) * w + b


def _out_ln_kernel(yg_ref, yn_ref, w_ref, h_ref, lw_ref, lb_ref, o_ref, ob_ref, *, alpha):
    kg = yg_ref.shape[1]
    mix = _dot(yg_ref[...], w_ref[0:kg, :]) + _dot(yn_ref[...], w_ref[kg:, :])
    o = _layer_norm(alpha * h_ref[...] + mix, lw_ref[...], lb_ref[...])
    o_ref[...] = o
    ob_ref[...] = o.astype(ob_ref.dtype)


def _out_ln(yg, yn, w, h, lw, lb, layer, alpha, tm):
    m, kg = yg.shape
    kn = yn.shape[1]
    n = w.shape[2]
    row = lambda width: pl.BlockSpec((tm, width), lambda i: (i, 0))
    per_layer = lambda shape: pl.BlockSpec((None,) + shape, lambda i: (layer, 0, 0))
    return pl.pallas_call(
        functools.partial(_out_ln_kernel, alpha=alpha),
        grid=(m // tm,),
        in_specs=[row(kg), row(kn), per_layer((kg + kn, n)), row(n), per_layer((1, n)), per_layer((1, n))],
        out_specs=[row(n), row(n)],
        out_shape=[jax.ShapeDtypeStruct((m, n), F32), jax.ShapeDtypeStruct((m, n), BF16)],
        compiler_params=pltpu.CompilerParams(
            dimension_semantics=("arbitrary",),
            vmem_limit_bytes=V7X_VMEM_LIMIT_BYTES),
        name="out_ln",
    )(yg, yn, w, h, lw, lb)


def _ffn_ln_kernel(a_ref, w1_ref, w2_ref, h_ref, lw_ref, lb_ref, o_ref, ob_ref, *, alpha):
    f = pl.program_id(1)

    @pl.when(f == 0)
    def _():
        o_ref[...] = alpha * h_ref[...]

    u = jnp.square(jnp.maximum(_dot(a_ref[...], w1_ref[...]), 0.0)).astype(BF16)
    o_ref[...] += _dot(u, w2_ref[...])

    @pl.when(f == pl.num_programs(1) - 1)
    def _():
        o = _layer_norm(o_ref[...], lw_ref[...], lb_ref[...])
        o_ref[...] = o
        ob_ref[...] = o.astype(ob_ref.dtype)


def _ffn_ln(a, w1, w2, h, lw, lb, layer, alpha, tm, tf):
    m, d = a.shape
    d_ff = w1.shape[2]
    row = pl.BlockSpec((tm, d), lambda i, f: (i, 0))
    vec = pl.BlockSpec((None, 1, d), lambda i, f: (layer, 0, 0))
    return pl.pallas_call(
        functools.partial(_ffn_ln_kernel, alpha=alpha),
        grid=(m // tm, d_ff // tf),
        in_specs=[row, pl.BlockSpec((None, d, tf), lambda i, f: (layer, 0, f)),
                  pl.BlockSpec((None, tf, d), lambda i, f: (layer, f, 0)), row, vec, vec],
        out_specs=[row, row],
        out_shape=[jax.ShapeDtypeStruct((m, d), F32), jax.ShapeDtypeStruct((m, d), BF16)],
        compiler_params=pltpu.CompilerParams(
            dimension_semantics=("arbitrary", "arbitrary"),
            vmem_limit_bytes=V7X_VMEM_LIMIT_BYTES),
        name="ffn_ln",
    )(a, w1, w2, h, lw, lb)


_GLA_QK = GLA_HEADS * GLA_DK
_GLA_VR = GLA_HEADS * GLA_DV
_NA_W = NA_HEADS * NA_DH
_COL_GQ = 0
_COL_GK = _COL_GQ + _GLA_QK
_COL_GV = _COL_GK + _GLA_QK
_COL_GR = _COL_GV + _GLA_VR
_COL_NQ = _COL_GR + _GLA_VR
_COL_NK = _COL_NQ + _NA_W
_COL_NV = _COL_NK + _NA_W
_PROJ_COLS = _COL_NV + _NA_W
_W_IN_LR = _COL_NQ
_W_IN_COLS = _PROJ_COLS + 2 * GLA_RANK
_PROJ_TN = 1024
_LR_TN = 128
_ROW_TILE = 688
_OUT_ROW_TILE = 344
_FFN_TF = 512


def _gla_gate_params(w_up, b_up):
    depth = w_up.shape[0]
    wc = jnp.zeros((depth, GLA_HEADS, 128, 2 * GLA_DK), F32)
    for z in range(2):
        w = w_up[:, z].astype(F32).reshape(depth, GLA_RANK, GLA_HEADS, GLA_DK).transpose(0, 2, 1, 3)
        wc = wc.at[:, :, z * GLA_RANK:(z + 1) * GLA_RANK, z * GLA_DK:(z + 1) * GLA_DK].set(w)
    bc = b_up.astype(F32).reshape(depth, 2, GLA_HEADS, GLA_DK).transpose(0, 2, 1, 3)
    return wc.astype(BF16), bc.reshape(depth, GLA_HEADS, 1, 2 * GLA_DK)


def kernel(x, meta, w_in, gla_w_up, gla_b_up, gla_norm_w, na_rel_bias, w_out, ln1_w, ln1_b,
           w_ff1, w_ff2, ln2_w, ln2_b):
    bsz, seq, d = x.shape
    depth = w_in.shape[0]
    alpha = (2 * depth) ** 0.25
    seq_len = seq + N_META
    m = bsz * seq_len
    meta_b = jnp.broadcast_to(meta.astype(x.dtype)[None], (bsz, N_META, d))
    h = jnp.concatenate([meta_b, x], axis=1).reshape(m, d).astype(F32)
    hb = h.astype(BF16)
    assert w_in.shape[2] == _W_IN_COLS and _W_IN_LR % _PROJ_TN == 0 and _PROJ_COLS % _PROJ_TN == 0
    w_in_t = jnp.swapaxes(w_in, 1, 2)
    w_out_b, w_ff1_b, w_ff2_b = w_out.astype(BF16), w_ff1.astype(BF16), w_ff2.astype(BF16)
    wc, bc = _gla_gate_params(gla_w_up, gla_b_up)
    nw = gla_norm_w.astype(F32).reshape(depth, 1, GLA_DV)
    table = _na_bias_table(na_rel_bias)
    vec = lambda p: p.astype(F32).reshape(depth, 1, d)
    ln1_w, ln1_b, ln2_w, ln2_b = vec(ln1_w), vec(ln1_b), vec(ln2_w), vec(ln2_b)
    for l in range(depth):
        proj = _proj(hb, w_in_t, l, _PROJ_COLS // _PROJ_TN, _PROJ_TN, 0, _W_IN_LR // _PROJ_TN, 2 * GLA_RANK,
                     _ROW_TILE, "proj").reshape(bsz, seq_len, _PROJ_COLS)
        lr = _proj(hb, w_in_t, l, 1, _LR_TN, _W_IN_LR, 1, 0, _ROW_TILE, "proj_lr").reshape(bsz, seq_len, _LR_TN)
        y_gla = _gla(proj, lr, wc, bc, nw, l).reshape(m, GLA_HEADS * GLA_DV)
        y_na = _na(proj, table, l).reshape(m, NA_HEADS * NA_DH)
        h, hb = _out_ln(y_gla, y_na, w_out_b, h, ln1_w, ln1_b, l, alpha, _OUT_ROW_TILE)
        h, hb = _ffn_ln(hb, w_ff1_b, w_ff2_b, h, ln2_w, ln2_b, l, alpha, _ROW_TILE, _FFN_TF)
    return h.reshape(bsz, seq_len, d)[:, N_META:].astype(x.dtype)
```

```python
import functools

import numpy as np
import jax
import jax.numpy as jnp
from jax import lax
from jax.experimental import pallas as pl
from jax.experimental.pallas import tpu as pltpu

N_META = 16
GRID_W = 64
GLA_HEADS = 4
GLA_DK = 128
GLA_DV = 256
GLA_RANK = 16
GLA_TAU = 16.0
_GLA_BLOCK = 256
_GLA_SAFE_RANGE = 80.0
NA_HEADS = 16
NA_DH = 64
NA_WIN_H = 8
NA_WIN_W = 16
NA_HEAD_GROUP = 4
LN_EPS = 1e-5
MASK_VALUE = -1e30

V7X_VMEM_LIMIT_BYTES = 58 * 1024 * 1024

F32 = jnp.float32
BF16 = jnp.bfloat16

_NT = (((1,), (1,)), ((), ()))
_TN = (((0,), (0,)), ((), ()))


def _dot(a, b, dims=None):
    if dims is None:
        return jnp.dot(a, b, preferred_element_type=F32)
    return lax.dot_general(a, b, dims, preferred_element_type=F32)


def _proj_kernel(a_ref, w_ref, o_ref, wb_ref):
    @pl.when(pl.program_id(1) == 0)
    def _():
        wb_ref[...] = w_ref[...].astype(BF16)

    o_ref[...] = _dot(a_ref[...], wb_ref[...], _NT).astype(o_ref.dtype)


def _proj(a, wt, layer, n_tiles, tn, first, skip_from, skip, tm, name):
    m, k = a.shape
    assert first % 16 == 0 and tn % 16 == 0 and skip % 16 == 0

    def w_index(j, i):
        return (layer, pl.multiple_of(first + j * tn + jnp.where(j >= skip_from, skip, 0), 16), 0)

    return pl.pallas_call(
        _proj_kernel,
        grid=(n_tiles, m // tm),
        in_specs=[pl.BlockSpec((tm, k), lambda j, i: (i, 0)),
                  pl.BlockSpec((pl.Squeezed(), pl.Element(tn), pl.Element(k)), w_index)],
        out_specs=pl.BlockSpec((tm, tn), lambda j, i: (i, j)),
        out_shape=jax.ShapeDtypeStruct((m, n_tiles * tn), BF16),
        scratch_shapes=[pltpu.VMEM((tn, k), BF16)],
        compiler_params=pltpu.CompilerParams(
            dimension_semantics=("arbitrary", "arbitrary"),
            vmem_limit_bytes=V7X_VMEM_LIMIT_BYTES),
        name=name,
    )(a, wt)


def _gla_decays(rows, c, g_ref):
    g2 = g_ref[rows, :]
    t_idx = lax.broadcasted_iota(jnp.int32, (c, c), 0)
    s_idx = lax.broadcasted_iota(jnp.int32, (c, c), 1)
    tri = jnp.where(s_idx <= t_idx, 1.0, 0.0).astype(BF16)
    g_hi = g2.astype(BF16)
    g_lo = (g2 - g_hi.astype(F32)).astype(BF16)
    p = _dot(tri, g_hi) + _dot(tri, g_lo)
    cb = p[c - 1:c, GLA_DK:] - p[:, GLA_DK:] + g2[:, GLA_DK:]
    return p[:, :GLA_DK], cb


def _gla_chunk_terms(rows, ci, size, q_ref, k_ref, v_ref, g_ref, qi_ref, oi_ref, ut_ref, e_ref, rng_ref):
    c = size
    bf, cb = _gla_decays(rows, c, g_ref)
    t_idx = lax.broadcasted_iota(jnp.int32, (c, c), 0)
    s_idx = lax.broadcasted_iota(jnp.int32, (c, c), 1)
    lower = s_idx <= t_idx
    upper = s_idx >= t_idx
    bf_end = bf[c - 1:c, :]
    cb_end = cb[0:1, :]
    bf_mid = bf[c // 2 - 1:c // 2, :]
    cb_mid = cb[c // 2:c // 2 + 1, :]
    rng_ref[...] = jnp.maximum(rng_ref[...], jnp.broadcast_to(jnp.maximum(-bf_end, -cb_end), rng_ref.shape))
    q = q_ref[0, rows, :].astype(F32) * (GLA_DK ** -0.5)
    k = k_ref[0, rows, :].astype(F32)
    v = v_ref[0, rows, :]
    qi_f = q * jnp.exp(bf)
    ks_f = k * jnp.exp(bf_end - bf)
    qi_b = q * jnp.exp(cb)
    ks_b = k * jnp.exp(cb_end - cb)
    sc_f = _dot((qi_f * jnp.exp(-bf_mid)).astype(BF16), (ks_f * jnp.exp(bf_mid - bf_end)).astype(BF16), _NT)
    sc_b = _dot((qi_b * jnp.exp(-cb_mid)).astype(BF16), (ks_b * jnp.exp(cb_mid - cb_end)).astype(BF16), _NT)
    sc = jnp.where(lower, sc_f, 0.0) + jnp.where(upper, sc_b, 0.0)
    oi_ref[rows, :] = _dot(sc.astype(BF16), v)
    qi_ref[rows, :] = jnp.concatenate([qi_f, qi_b], axis=-1).astype(BF16)
    ut_ref[ci] = _dot(v, jnp.concatenate([ks_f, ks_b], axis=-1).astype(BF16), _TN)
    e_ref[ci] = jnp.broadcast_to(jnp.exp(jnp.concatenate([bf_end, cb_end], axis=-1)), (8, 2 * GLA_DK))


def _gla_exact_intra(rows, size, q_ref, k_ref, v_ref, g_ref, oi_ref):
    c = size
    bf, cb = _gla_decays(rows, c, g_ref)
    q = q_ref[0, rows, :].astype(F32) * (GLA_DK ** -0.5)
    k = k_ref[0, rows, :].astype(F32)
    row = lax.broadcasted_iota(jnp.int32, (c, 1), 0)
    lane = lax.broadcasted_iota(jnp.int32, (c, c), 1)

    def body(t, sct):
        pick = row == t
        take = lambda a: jnp.sum(jnp.where(pick, a, 0.0), axis=0, keepdims=True)
        dec = (jnp.exp(jnp.where(row <= t, take(bf) - bf, MASK_VALUE))
               + jnp.exp(jnp.where(row >= t, take(cb) - cb, MASK_VALUE)))
        col = jnp.sum(k * take(q) * dec, axis=-1, keepdims=True)
        return jnp.where(lane == t, col, sct)

    sct = lax.fori_loop(0, c, body, jnp.zeros((c, c), F32))
    oi_ref[rows, :] = _dot(sct.astype(BF16), v_ref[0, rows, :], _TN)


def _gla_chunk_out(rows, ci, r_ref, nw_ref, qi_ref, oi_ref, st_ref, o_ref):
    o = oi_ref[rows, :] + _dot(qi_ref[rows, :], st_ref[ci], _NT)
    o = o * lax.rsqrt(jnp.mean(o * o, axis=-1, keepdims=True) + LN_EPS) * nw_ref[...]
    r = r_ref[0, rows, :].astype(F32)
    o_ref[0, rows, :] = (o * (r / (1.0 + jnp.exp(-r)))).astype(o_ref.dtype)


def _gla_kernel(q_ref, k_ref, v_ref, r_ref, lr_ref, wc_ref, bc_ref, nw_ref, o_ref,
                g_ref, qi_ref, oi_ref, ut_ref, e_ref, st_ref, rng_ref, *, seq_len, gate_rows):
    c = _GLA_BLOCK
    n_full = (seq_len - N_META) // c
    n_chunks = n_full + 1
    rng_ref[...] = jnp.zeros_like(rng_ref)

    for i in range(seq_len // gate_rows):
        rows = pl.ds(i * gate_rows, gate_rows)
        x = _dot(lr_ref[0, rows, :], wc_ref[0]) + bc_ref[0]
        g_ref[rows, :] = (jnp.minimum(x, 0.0) - jnp.log(1.0 + jnp.exp(-jnp.abs(x)))) * (1.0 / GLA_TAU)

    def chunk_rows(ci):
        return pl.ds(pl.multiple_of(N_META + (ci - 1) * c, 16), c)

    terms = functools.partial(_gla_chunk_terms, q_ref=q_ref, k_ref=k_ref, v_ref=v_ref, g_ref=g_ref,
                              qi_ref=qi_ref, oi_ref=oi_ref, ut_ref=ut_ref, e_ref=e_ref, rng_ref=rng_ref)
    terms(pl.ds(0, N_META), 0, N_META)

    def terms_body(ci, carry):
        terms(chunk_rows(ci), ci, c)
        return carry

    lax.fori_loop(1, n_chunks, terms_body, 0, unroll=2)

    @pl.when(jnp.max(rng_ref[...]) > _GLA_SAFE_RANGE)
    def _():
        exact = functools.partial(_gla_exact_intra, q_ref=q_ref, k_ref=k_ref, v_ref=v_ref, g_ref=g_ref,
                                  oi_ref=oi_ref)
        exact(pl.ds(0, N_META), N_META)

        def exact_body(ci, carry):
            exact(chunk_rows(ci), c)
            return carry

        lax.fori_loop(1, n_chunks, exact_body, 0)

    def scan_body(i, carry):
        s_f, s_b = carry
        j = n_chunks - 1 - i
        st_ref[i, :, :GLA_DK] = s_f.astype(BF16)
        st_ref[j, :, GLA_DK:] = s_b.astype(BF16)
        s_f = s_f * e_ref[i, 0:1, :GLA_DK] + ut_ref[i, :, :GLA_DK]
        s_b = s_b * e_ref[j, 0:1, GLA_DK:] + ut_ref[j, :, GLA_DK:]
        return s_f, s_b

    zero = jnp.zeros((GLA_DV, GLA_DK), F32)
    lax.fori_loop(0, n_chunks, scan_body, (zero, zero))

    out = functools.partial(_gla_chunk_out, r_ref=r_ref, nw_ref=nw_ref, qi_ref=qi_ref, oi_ref=oi_ref,
                            st_ref=st_ref, o_ref=o_ref)
    out(pl.ds(0, N_META), 0)

    def out_body(ci, carry):
        out(chunk_rows(ci), ci)
        return carry

    lax.fori_loop(1, n_chunks, out_body, 0, unroll=2)


def _gla(proj, lr, wc, bc, nw, layer):
    bsz, seq_len, _ = proj.shape
    assert (seq_len - N_META) % _GLA_BLOCK == 0 and seq_len % _ROW_TILE == 0
    n_chunks = (seq_len - N_META) // _GLA_BLOCK + 1
    kern = functools.partial(_gla_kernel, seq_len=seq_len, gate_rows=_ROW_TILE)
    qk_blk = lambda off: pl.BlockSpec((1, seq_len, GLA_DK), lambda b, h: (b, 0, off + h))
    vr_blk = lambda off: pl.BlockSpec((1, seq_len, GLA_DV), lambda b, h: (b, 0, off + h))
    return pl.pallas_call(
        kern,
        grid=(bsz, GLA_HEADS),
        in_specs=[qk_blk(_COL_GQ // GLA_DK), qk_blk(_COL_GK // GLA_DK),
                  vr_blk(_COL_GV // GLA_DV), vr_blk(_COL_GR // GLA_DV),
                  pl.BlockSpec((1, seq_len, 128), lambda b, h: (b, 0, 0)),
                  pl.BlockSpec((None, 1, 128, 2 * GLA_DK), lambda b, h: (layer, h, 0, 0)),
                  pl.BlockSpec((None, 1, 1, 2 * GLA_DK), lambda b, h: (layer, h, 0, 0)),
                  pl.BlockSpec((None, 1, GLA_DV), lambda b, h: (layer, 0, 0))],
        out_specs=pl.BlockSpec((1, seq_len, GLA_DV), lambda b, h: (b, 0, h)),
        out_shape=jax.ShapeDtypeStruct((bsz, seq_len, GLA_HEADS * GLA_DV), BF16),
        scratch_shapes=[pltpu.VMEM((seq_len, 2 * GLA_DK), F32),
                        pltpu.VMEM((seq_len, 2 * GLA_DK), BF16),
                        pltpu.VMEM((seq_len, GLA_DV), F32),
                        pltpu.VMEM((n_chunks, GLA_DV, 2 * GLA_DK), F32),
                        pltpu.VMEM((n_chunks, 8, 2 * GLA_DK), F32),
                        pltpu.VMEM((n_chunks, GLA_DV, 2 * GLA_DK), BF16),
                        pltpu.VMEM((8, GLA_DK), F32)],
        compiler_params=pltpu.CompilerParams(
            dimension_semantics=("arbitrary", "arbitrary"),
            vmem_limit_bytes=V7X_VMEM_LIMIT_BYTES),
        name="gla",
    )(proj, proj, proj, proj, lr, wc, bc, nw)


def _na_kernel(q_ref, k_ref, v_ref, t_ref, o_ref, *, seq_len):
    q_ref, k_ref, v_ref = q_ref.at[0], k_ref.at[0], v_ref.at[0]
    rows = (seq_len - N_META) // GRID_W
    hw = NA_HEAD_GROUP * NA_DH
    nq = NA_HEAD_GROUP * GRID_W
    n_keys = NA_WIN_H * GRID_W
    meta_blk = 128

    def block_diag(q4):
        n = q4.shape[0]
        qrep = jnp.concatenate([q4] * NA_HEAD_GROUP, axis=0)
        rb = lax.broadcasted_iota(jnp.int32, (NA_HEAD_GROUP * n, hw), 0) // n
        lb = lax.broadcasted_iota(jnp.int32, (NA_HEAD_GROUP * n, hw), 1) // NA_DH
        return jnp.where(rb == lb, qrep, jnp.zeros_like(qrep))

    def pick_diag(o, n):
        lb = lax.broadcasted_iota(jnp.int32, (n, hw), 1) // NA_DH
        out = jnp.zeros((n, hw), F32)
        for h in range(NA_HEAD_GROUP):
            out = jnp.where(lb == h, o[h * n:(h + 1) * n, :], out)
        return out

    k_meta = k_ref[0:meta_blk, :]
    v_meta = v_ref[0:meta_blk, :]

    def meta_scores(qbd):
        s = _dot(qbd, k_meta, _NT)
        lane = lax.broadcasted_iota(jnp.int32, s.shape, 1)
        return jnp.where(lane < N_META, s, MASK_VALUE)

    qm = block_diag(q_ref[0:N_META, :] * (NA_DH ** -0.5))
    sm = meta_scores(qm)
    pm = jnp.exp(sm - jnp.max(sm, axis=-1, keepdims=True))
    om = _dot(pm.astype(BF16), v_meta) / jnp.sum(pm, axis=-1, keepdims=True)
    o_ref[0, 0:N_META, :] = pick_diag(om, N_META).astype(o_ref.dtype)

    def row_body(r, carry):
        rs = jnp.clip(r - NA_WIN_H // 2, 0, rows - NA_WIN_H)
        delta = r - rs
        q_start = pl.multiple_of(N_META + r * GRID_W, 16)
        k_start = pl.multiple_of(N_META + rs * GRID_W, 16)
        qbd = block_diag(q_ref[pl.ds(q_start, GRID_W), :] * (NA_DH ** -0.5))
        kw = k_ref[pl.ds(k_start, n_keys), :]
        vw = v_ref[pl.ds(k_start, n_keys), :]
        sw = _dot(qbd, kw, _NT)
        bias = jnp.concatenate(
            [t_ref[2 * j - delta + NA_WIN_H - 1] for j in range(NA_WIN_H // 2)], axis=-1)
        sw = sw + bias
        smeta = meta_scores(qbd)
        blocks = lambda a: [a[:, i:i + meta_blk] for i in range(0, n_keys, meta_blk)]
        mx = jnp.max(functools.reduce(jnp.maximum, blocks(sw), smeta), axis=-1, keepdims=True)
        pw = jnp.exp(sw - mx)
        pmeta = jnp.exp(smeta - mx)
        den = jnp.sum(functools.reduce(jnp.add, blocks(pw), pmeta), axis=-1, keepdims=True)
        o = (_dot(pw.astype(BF16), vw) + _dot(pmeta.astype(BF16), v_meta)) / den
        o_ref[0, pl.ds(q_start, GRID_W), :] = pick_diag(o, GRID_W).astype(o_ref.dtype)
        return carry

    lax.fori_loop(0, rows, row_body, 0, unroll=4)


def _na_bias_table(rel_bias):
    depth = rel_bias.shape[0]
    col = np.arange(GRID_W)
    cs = np.clip(col - NA_WIN_W // 2, 0, GRID_W - NA_WIN_W)
    inside = (col[None, :] >= cs[:, None]) & (col[None, :] < cs[:, None] + NA_WIN_W)
    dc = col[None, :] - col[:, None] + NA_WIN_W - 1
    onehot = (np.arange(2 * NA_WIN_W - 1)[:, None, None] == dc[None]) & inside[None]
    t = jnp.einsum("lhdj,jqk->lhdqk", rel_bias.astype(F32), jnp.asarray(onehot, F32),
                   precision=lax.Precision.HIGHEST)
    t = jnp.where(inside, t, MASK_VALUE)
    t = jnp.concatenate([t[:, :, :-1], t[:, :, 1:]], axis=-1)
    n_dr = t.shape[2]
    t = t.reshape(depth, NA_HEADS // NA_HEAD_GROUP, NA_HEAD_GROUP, n_dr, GRID_W, 2 * GRID_W)
    return t.transpose(0, 1, 3, 2, 4, 5).reshape(depth, NA_HEADS // NA_HEAD_GROUP, n_dr,
                                                  NA_HEAD_GROUP * GRID_W, 2 * GRID_W)


def _na(proj, table, layer):
    bsz, seq_len, _ = proj.shape
    hw = NA_HEAD_GROUP * NA_DH
    n_groups = NA_HEADS // NA_HEAD_GROUP
    kern = functools.partial(_na_kernel, seq_len=seq_len)
    blk = lambda off: pl.BlockSpec((1, seq_len, hw), lambda g, b: (b, 0, off + g))
    return pl.pallas_call(
        kern,
        grid=(n_groups, bsz),
        in_specs=[blk(_COL_NQ // hw), blk(_COL_NK // hw), blk(_COL_NV // hw),
                  pl.BlockSpec((None, None) + table.shape[2:], lambda g, b: (layer, g, 0, 0, 0))],
        out_specs=pl.BlockSpec((1, seq_len, hw), lambda g, b: (b, 0, g)),
        out_shape=jax.ShapeDtypeStruct((bsz, seq_len, NA_HEADS * NA_DH), BF16),
        compiler_params=pltpu.CompilerParams(
            dimension_semantics=("arbitrary", "arbitrary"),
            vmem_limit_bytes=V7X_VMEM_LIMIT_BYTES),
        name="na",
    )(proj, proj, proj, table)


def _layer_norm(z, w, b):
    mu = jnp.mean(z, axis=-1, keepdims=True)
    zc = z - mu
    var = jnp.mean(zc * zc, axis=-1, keepdims=True)
    return zc * lax.rsqrt(var + LN_EPS) * w + b


def _out_ln_kernel(yg_ref, yn_ref, w_ref, h_ref, lw_ref, lb_ref, o_ref, ob_ref, *, alpha):
    kg = yg_ref.shape[1]
    mix = _dot(yg_ref[...], w_ref[0:kg, :]) + _dot(yn_ref[...], w_ref[kg:, :])
    o = _layer_norm(alpha * h_ref[...] + mix, lw_ref[...], lb_ref[...])
    o_ref[...] = o
    ob_ref[...] = o.astype(ob_ref.dtype)


def _out_ln(yg, yn, w, h, lw, lb, layer, alpha, tm):
    m, kg = yg.shape
    kn = yn.shape[1]
    n = w.shape[2]
    row = lambda width: pl.BlockSpec((tm, width), lambda i: (i, 0))
    per_layer = lambda shape: pl.BlockSpec((None,) + shape, lambda i: (layer, 0, 0))
    return pl.pallas_call(
        functools.partial(_out_ln_kernel, alpha=alpha),
        grid=(m // tm,),
        in_specs=[row(kg), row(kn), per_layer((kg + kn, n)), row(n), per_layer((1, n)), per_layer((1, n))],
        out_specs=[row(n), row(n)],
        out_shape=[jax.ShapeDtypeStruct((m, n), F32), jax.ShapeDtypeStruct((m, n), BF16)],
        compiler_params=pltpu.CompilerParams(
            dimension_semantics=("arbitrary",),
            vmem_limit_bytes=V7X_VMEM_LIMIT_BYTES),
        name="out_ln",
    )(yg, yn, w, h, lw, lb)


def _ffn_ln_kernel(a_ref, w1_ref, w2_ref, h_ref, lw_ref, lb_ref, o_ref, ob_ref=None, *, alpha):
    f = pl.program_id(1)

    @pl.when(f == 0)
    def _():
        o_ref[...] = alpha * h_ref[...]

    u = jnp.square(jnp.maximum(_dot(a_ref[...], w1_ref[...]), 0.0)).astype(BF16)
    o_ref[...] += _dot(u, w2_ref[...])

    @pl.when(f == pl.num_programs(1) - 1)
    def _():
        o = _layer_norm(o_ref[...], lw_ref[...], lb_ref[...])
        o_ref[...] = o
        if ob_ref is not None:
            ob_ref[...] = o.astype(ob_ref.dtype)


def _ffn_ln(a, w1, w2, h, lw, lb, layer, alpha, tm, tf, skip_rows):
    bsz, seq_len, d = a.shape
    d_ff = w1.shape[2]
    rows = seq_len - skip_rows
    assert rows % tm == 0 and skip_rows % 16 == 0 and tm % 16 == 0 and d_ff % tf == 0
    nt = rows // tm
    row_in = pl.BlockSpec((pl.Squeezed(), pl.Element(tm), pl.Element(d)),
                          lambda i, f: (i // nt, pl.multiple_of(skip_rows + (i % nt) * tm, 16), 0))
    row_out = pl.BlockSpec((None, tm, d), lambda i, f: (i // nt, i % nt, 0))
    vec = pl.BlockSpec((None, 1, d), lambda i, f: (layer, 0, 0))
    out_shape = [jax.ShapeDtypeStruct((bsz, rows, d), F32)]
    if not skip_rows:
        out_shape.append(jax.ShapeDtypeStruct((bsz, rows, d), BF16))
    return pl.pallas_call(
        functools.partial(_ffn_ln_kernel, alpha=alpha),
        grid=(bsz * nt, d_ff // tf),
        in_specs=[row_in, pl.BlockSpec((None, d, tf), lambda i, f: (layer, 0, f)),
                  pl.BlockSpec((None, tf, d), lambda i, f: (layer, f, 0)), row_in, vec, vec],
        out_specs=[row_out] * len(out_shape),
        out_shape=out_shape,
        compiler_params=pltpu.CompilerParams(
            dimension_semantics=("arbitrary", "arbitrary"),
            vmem_limit_bytes=V7X_VMEM_LIMIT_BYTES),
        name="ffn_ln",
    )(a, w1, w2, h, lw, lb)


_GLA_QK = GLA_HEADS * GLA_DK
_GLA_VR = GLA_HEADS * GLA_DV
_NA_W = NA_HEADS * NA_DH
_COL_GQ = 0
_COL_GK = _COL_GQ + _GLA_QK
_COL_GV = _COL_GK + _GLA_QK
_COL_GR = _COL_GV + _GLA_VR
_COL_NQ = _COL_GR + _GLA_VR
_COL_NK = _COL_NQ + _NA_W
_COL_NV = _COL_NK + _NA_W
_PROJ_COLS = _COL_NV + _NA_W
_W_IN_LR = _COL_NQ
_W_IN_COLS = _PROJ_COLS + 2 * GLA_RANK
_PROJ_TN = 1024
_LR_TN = 128
_ROW_TILE = 688
_PROJ_ROW_TILE = 1376
_OUT_ROW_TILE = 344
_FFN_TF = 512
_LAST_ROW_TILE = 512
_LAST_FFN_TF = 1024


def _gla_gate_params(w_up, b_up):
    depth = w_up.shape[0]
    wc = jnp.zeros((depth, GLA_HEADS, 128, 2 * GLA_DK), F32)
    for z in range(2):
        w = w_up[:, z].astype(F32).reshape(depth, GLA_RANK, GLA_HEADS, GLA_DK).transpose(0, 2, 1, 3)
        wc = wc.at[:, :, z * GLA_RANK:(z + 1) * GLA_RANK, z * GLA_DK:(z + 1) * GLA_DK].set(w)
    bc = b_up.astype(F32).reshape(depth, 2, GLA_HEADS, GLA_DK).transpose(0, 2, 1, 3)
    return wc.astype(BF16), bc.reshape(depth, GLA_HEADS, 1, 2 * GLA_DK)


def kernel(x, meta, w_in, gla_w_up, gla_b_up, gla_norm_w, na_rel_bias, w_out, ln1_w, ln1_b,
           w_ff1, w_ff2, ln2_w, ln2_b):
    bsz, seq, d = x.shape
    depth = w_in.shape[0]
    alpha = (2 * depth) ** 0.25
    seq_len = seq + N_META
    m = bsz * seq_len
    meta_b = jnp.broadcast_to(meta.astype(x.dtype)[None], (bsz, N_META, d))
    h = jnp.concatenate([meta_b, x], axis=1).reshape(m, d).astype(F32)
    hb = h.astype(BF16)
    assert w_in.shape[2] == _W_IN_COLS and _W_IN_LR % _PROJ_TN == 0 and _PROJ_COLS % _PROJ_TN == 0
    w_in_t = jnp.swapaxes(w_in, 1, 2)
    w_out_b, w_ff1_b, w_ff2_b = w_out.astype(BF16), w_ff1.astype(BF16), w_ff2.astype(BF16)
    wc, bc = _gla_gate_params(gla_w_up, gla_b_up)
    nw = gla_norm_w.astype(F32).reshape(depth, 1, GLA_DV)
    table = _na_bias_table(na_rel_bias)
    vec = lambda p: p.astype(F32).reshape(depth, 1, d)
    ln1_w, ln1_b, ln2_w, ln2_b = vec(ln1_w), vec(ln1_b), vec(ln2_w), vec(ln2_b)
    for l in range(depth):
        proj = _proj(hb, w_in_t, l, _PROJ_COLS // _PROJ_TN, _PROJ_TN, 0, _W_IN_LR // _PROJ_TN, 2 * GLA_RANK,
                     _PROJ_ROW_TILE, "proj").reshape(bsz, seq_len, _PROJ_COLS)
        lr = _proj(hb, w_in_t, l, 1, _LR_TN, _W_IN_LR, 1, 0, _ROW_TILE, "proj_lr").reshape(bsz, seq_len, _LR_TN)
        y_gla = _gla(proj, lr, wc, bc, nw, l).reshape(m, GLA_HEADS * GLA_DV)
        y_na = _na(proj, table, l).reshape(m, NA_HEADS * NA_DH)
        h, hb = _out_ln(y_gla, y_na, w_out_b, h, ln1_w, ln1_b, l, alpha, _OUT_ROW_TILE)
        hb3, h3 = hb.reshape(bsz, seq_len, d), h.reshape(bsz, seq_len, d)
        if l + 1 < depth:
            h, hb = _ffn_ln(hb3, w_ff1_b, w_ff2_b, h3, ln2_w, ln2_b, l, alpha, _ROW_TILE, _FFN_TF, 0)
            h, hb = h.reshape(m, d), hb.reshape(m, d)
        else:
            out, = _ffn_ln(hb3, w_ff1_b, w_ff2_b, h3, ln2_w, ln2_b, l, alpha, _LAST_ROW_TILE, _LAST_FFN_TF,
                           N_META)
    return out.astype(x.dtype)
```

```python
import functools

import numpy as np
import jax
import jax.numpy as jnp
from jax import lax
from jax.experimental import pallas as pl
from jax.experimental.pallas import tpu as pltpu

N_META = 16
GRID_W = 64
GLA_HEADS = 4
GLA_DK = 128
GLA_DV = 256
GLA_RANK = 16
GLA_TAU = 16.0
_GLA_BLOCK = 256
_GLA_GROUP = 4
_GLA_SAFE_RANGE = 80.0
NA_HEADS = 16
NA_DH = 64
NA_WIN_H = 8
NA_WIN_W = 16
NA_HEAD_GROUP = 4
_NA_ROW_GROUP = 4
LN_EPS = 1e-5
MASK_VALUE = -1e30

V7X_VMEM_LIMIT_BYTES = 58 * 1024 * 1024

F32 = jnp.float32
BF16 = jnp.bfloat16

_NT = (((1,), (1,)), ((), ()))
_TN = (((0,), (0,)), ((), ()))


def _dot(a, b, dims=None):
    if dims is None:
        return jnp.dot(a, b, preferred_element_type=F32)
    return lax.dot_general(a, b, dims, preferred_element_type=F32)


def _proj_kernel(a_ref, w_ref, o_ref, wb_ref):
    @pl.when(pl.program_id(1) == 0)
    def _():
        wb_ref[...] = w_ref[...].astype(BF16)

    o_ref[...] = _dot(a_ref[...], wb_ref[...], _NT).astype(o_ref.dtype)


def _proj(a, wt, layer, n_tiles, tn, first, skip_from, skip, tm, name):
    m, k = a.shape
    assert first % 16 == 0 and tn % 16 == 0 and skip % 16 == 0

    def w_index(j, i):
        return (layer, pl.multiple_of(first + j * tn + jnp.where(j >= skip_from, skip, 0), 16), 0)

    return pl.pallas_call(
        _proj_kernel,
        grid=(n_tiles, m // tm),
        in_specs=[pl.BlockSpec((tm, k), lambda j, i: (i, 0)),
                  pl.BlockSpec((pl.Squeezed(), pl.Element(tn), pl.Element(k)), w_index)],
        out_specs=pl.BlockSpec((tm, tn), lambda j, i: (i, j)),
        out_shape=jax.ShapeDtypeStruct((m, n_tiles * tn), BF16),
        scratch_shapes=[pltpu.VMEM((tn, k), BF16)],
        compiler_params=pltpu.CompilerParams(
            dimension_semantics=("arbitrary", "arbitrary"),
            vmem_limit_bytes=V7X_VMEM_LIMIT_BYTES),
        name=name,
    )(a, wt)


def _gla_decays(rows, c, g_ref):
    g2 = g_ref[rows, :]
    t_idx = lax.broadcasted_iota(jnp.int32, (c, c), 0)
    s_idx = lax.broadcasted_iota(jnp.int32, (c, c), 1)
    tri = jnp.where(s_idx <= t_idx, 1.0, 0.0).astype(BF16)
    g_hi = g2.astype(BF16)
    g_lo = (g2 - g_hi.astype(F32)).astype(BF16)
    p = _dot(tri, g_hi) + _dot(tri, g_lo)
    cb = p[c - 1:c, GLA_DK:] - p[:, GLA_DK:] + g2[:, GLA_DK:]
    return p[:, :GLA_DK], cb


def _gla_group_terms(chunks, q_ref, k_ref, v_ref, g_ref, qi_ref, oi_ref, ut_ref, e_ref):
    decays = [_gla_decays(rows, c, g_ref) for rows, _, c in chunks]
    operands, ranges = [], []
    for (rows, ci, c), (bf, cb) in zip(chunks, decays):
        bf_end = bf[c - 1:c, :]
        cb_end = cb[0:1, :]
        bf_mid = bf[c // 2 - 1:c // 2, :]
        cb_mid = cb[c // 2:c // 2 + 1, :]
        q = q_ref[0, rows, :].astype(F32) * (GLA_DK ** -0.5)
        k = k_ref[0, rows, :].astype(F32)
        qi_f = q * jnp.exp(bf)
        ks_f = k * jnp.exp(bf_end - bf)
        qi_b = q * jnp.exp(cb)
        ks_b = k * jnp.exp(cb_end - cb)
        qi_ref[rows, :] = jnp.concatenate([qi_f, qi_b], axis=-1).astype(BF16)
        e_ref[ci] = jnp.broadcast_to(jnp.exp(jnp.concatenate([bf_end, cb_end], axis=-1)), (8, 2 * GLA_DK))
        operands.append(((qi_f * jnp.exp(-bf_mid)).astype(BF16), (ks_f * jnp.exp(bf_mid - bf_end)).astype(BF16),
                         (qi_b * jnp.exp(-cb_mid)).astype(BF16), (ks_b * jnp.exp(cb_mid - cb_end)).astype(BF16),
                         jnp.concatenate([ks_f, ks_b], axis=-1).astype(BF16)))
        ranges.append(jnp.maximum(-bf_end, -cb_end))
    scores = [(_dot(qt_f, kt_f, _NT), _dot(qt_b, kt_b, _NT)) for qt_f, kt_f, qt_b, kt_b, _ in operands]
    for (rows, ci, c), (sc_f, sc_b) in zip(chunks, scores):
        t_idx = lax.broadcasted_iota(jnp.int32, (c, c), 0)
        s_idx = lax.broadcasted_iota(jnp.int32, (c, c), 1)
        sc = jnp.where(s_idx <= t_idx, sc_f, 0.0) + jnp.where(s_idx >= t_idx, sc_b, 0.0)
        oi_ref[rows, :] = _dot(sc.astype(BF16), v_ref[0, rows, :])
    for (rows, ci, c), ops in zip(chunks, operands):
        ut_ref[ci] = _dot(v_ref[0, rows, :], ops[4], _TN)
    return ranges


def _gla_exact_intra(rows, size, q_ref, k_ref, v_ref, g_ref, oi_ref):
    c = size
    bf, cb = _gla_decays(rows, c, g_ref)
    q = q_ref[0, rows, :].astype(F32) * (GLA_DK ** -0.5)
    k = k_ref[0, rows, :].astype(F32)
    row = lax.broadcasted_iota(jnp.int32, (c, 1), 0)
    lane = lax.broadcasted_iota(jnp.int32, (c, c), 1)

    def body(t, sct):
        pick = row == t
        take = lambda a: jnp.sum(jnp.where(pick, a, 0.0), axis=0, keepdims=True)
        dec = (jnp.exp(jnp.where(row <= t, take(bf) - bf, MASK_VALUE))
               + jnp.exp(jnp.where(row >= t, take(cb) - cb, MASK_VALUE)))
        col = jnp.sum(k * take(q) * dec, axis=-1, keepdims=True)
        return jnp.where(lane == t, col, sct)

    sct = lax.fori_loop(0, c, body, jnp.zeros((c, c), F32))
    oi_ref[rows, :] = _dot(sct.astype(BF16), v_ref[0, rows, :], _TN)


def _gla_group_out(chunks, r_ref, nw_ref, qi_ref, oi_ref, st_ref, o_ref):
    inter = [_dot(qi_ref[rows, :], st_ref[ci], _NT) for rows, ci in chunks]
    for (rows, ci), o_inter in zip(chunks, inter):
        o = oi_ref[rows, :] + o_inter
        o = o * lax.rsqrt(jnp.mean(o * o, axis=-1, keepdims=True) + LN_EPS) * nw_ref[...]
        r = r_ref[0, rows, :].astype(F32)
        o_ref[0, rows, :] = (o * (r / (1.0 + jnp.exp(-r)))).astype(o_ref.dtype)


def _gla_kernel(q_ref, k_ref, v_ref, r_ref, lr_ref, wc_ref, bc_ref, nw_ref, o_ref,
                g_ref, qi_ref, oi_ref, ut_ref, e_ref, st_ref, *, seq_len, gate_rows):
    c = _GLA_BLOCK
    n_full = (seq_len - N_META) // c
    n_chunks = n_full + 1

    for i in range(seq_len // gate_rows):
        rows = pl.ds(i * gate_rows, gate_rows)
        x = _dot(lr_ref[0, rows, :], wc_ref[0]) + bc_ref[0]
        g_ref[rows, :] = (jnp.minimum(x, 0.0) - jnp.log(1.0 + jnp.exp(-jnp.abs(x)))) * (1.0 / GLA_TAU)

    def chunk_rows(ci):
        return pl.ds(pl.multiple_of(N_META + (ci - 1) * c, 16), c)

    terms = functools.partial(_gla_group_terms, q_ref=q_ref, k_ref=k_ref, v_ref=v_ref, g_ref=g_ref,
                              qi_ref=qi_ref, oi_ref=oi_ref, ut_ref=ut_ref, e_ref=e_ref)
    full = [(pl.ds(N_META + (ci - 1) * c, c), ci, c) for ci in range(1, n_chunks)]
    decay_range = terms([(pl.ds(0, N_META), 0, N_META)])
    for i in range(0, n_full, _GLA_GROUP):
        decay_range += terms(full[i:i + _GLA_GROUP])

    @pl.when(jnp.max(functools.reduce(jnp.maximum, decay_range)) > _GLA_SAFE_RANGE)
    def _():
        exact = functools.partial(_gla_exact_intra, q_ref=q_ref, k_ref=k_ref, v_ref=v_ref, g_ref=g_ref,
                                  oi_ref=oi_ref)
        exact(pl.ds(0, N_META), N_META)

        def exact_body(ci, carry):
            exact(chunk_rows(ci), c)
            return carry

        lax.fori_loop(1, n_chunks, exact_body, 0)

    def scan_body(i, carry):
        s_f, s_b = carry
        j = n_chunks - 1 - i
        st_ref[i, :, :GLA_DK] = s_f.astype(BF16)
        st_ref[j, :, GLA_DK:] = s_b.astype(BF16)
        s_f = s_f * e_ref[i, 0:1, :GLA_DK] + ut_ref[i, :, :GLA_DK]
        s_b = s_b * e_ref[j, 0:1, GLA_DK:] + ut_ref[j, :, GLA_DK:]
        return s_f, s_b

    zero = jnp.zeros((GLA_DV, GLA_DK), F32)
    lax.fori_loop(0, n_chunks, scan_body, (zero, zero))

    out = functools.partial(_gla_group_out, r_ref=r_ref, nw_ref=nw_ref, qi_ref=qi_ref, oi_ref=oi_ref,
                            st_ref=st_ref, o_ref=o_ref)
    out([(pl.ds(0, N_META), 0)])
    for i in range(0, n_full, _GLA_GROUP):
        out([(rows, ci) for rows, ci, _ in full[i:i + _GLA_GROUP]])


def _gla(proj, lr, wc, bc, nw, layer):
    bsz, seq_len, _ = proj.shape
    assert (seq_len - N_META) % _GLA_BLOCK == 0 and seq_len % _ROW_TILE == 0
    n_chunks = (seq_len - N_META) // _GLA_BLOCK + 1
    kern = functools.partial(_gla_kernel, seq_len=seq_len, gate_rows=_ROW_TILE)
    qk_blk = lambda off: pl.BlockSpec((1, seq_len, GLA_DK), lambda b, h: (b, 0, off + h))
    vr_blk = lambda off: pl.BlockSpec((1, seq_len, GLA_DV), lambda b, h: (b, 0, off + h))
    return pl.pallas_call(
        kern,
        grid=(bsz, GLA_HEADS),
        in_specs=[qk_blk(_COL_GQ // GLA_DK), qk_blk(_COL_GK // GLA_DK),
                  vr_blk(_COL_GV // GLA_DV), vr_blk(_COL_GR // GLA_DV),
                  pl.BlockSpec((1, seq_len, 128), lambda b, h: (b, 0, 0)),
                  pl.BlockSpec((None, 1, 128, 2 * GLA_DK), lambda b, h: (layer, h, 0, 0)),
                  pl.BlockSpec((None, 1, 1, 2 * GLA_DK), lambda b, h: (layer, h, 0, 0)),
                  pl.BlockSpec((None, 1, GLA_DV), lambda b, h: (layer, 0, 0))],
        out_specs=pl.BlockSpec((1, seq_len, GLA_DV), lambda b, h: (b, 0, h)),
        out_shape=jax.ShapeDtypeStruct((bsz, seq_len, GLA_HEADS * GLA_DV), BF16),
        scratch_shapes=[pltpu.VMEM((seq_len, 2 * GLA_DK), F32),
                        pltpu.VMEM((seq_len, 2 * GLA_DK), BF16),
                        pltpu.VMEM((seq_len, GLA_DV), F32),
                        pltpu.VMEM((n_chunks, GLA_DV, 2 * GLA_DK), F32),
                        pltpu.VMEM((n_chunks, 8, 2 * GLA_DK), F32),
                        pltpu.VMEM((n_chunks, GLA_DV, 2 * GLA_DK), BF16)],
        compiler_params=pltpu.CompilerParams(
            dimension_semantics=("arbitrary", "arbitrary"),
            vmem_limit_bytes=V7X_VMEM_LIMIT_BYTES),
        name="gla",
    )(proj, proj, proj, proj, lr, wc, bc, nw)


def _na_kernel(q_ref, k_ref, v_ref, t_ref, o_ref, *, seq_len):
    q_ref, k_ref, v_ref = q_ref.at[0], k_ref.at[0], v_ref.at[0]
    rows = (seq_len - N_META) // GRID_W
    hw = NA_HEAD_GROUP * NA_DH
    nq = NA_HEAD_GROUP * GRID_W
    n_keys = NA_WIN_H * GRID_W
    meta_blk = 128

    def block_diag(q4):
        n = q4.shape[0]
        qrep = jnp.concatenate([q4] * NA_HEAD_GROUP, axis=0)
        rb = lax.broadcasted_iota(jnp.int32, (NA_HEAD_GROUP * n, hw), 0) // n
        lb = lax.broadcasted_iota(jnp.int32, (NA_HEAD_GROUP * n, hw), 1) // NA_DH
        return jnp.where(rb == lb, qrep, jnp.zeros_like(qrep))

    def pick_diag(o, n):
        lb = lax.broadcasted_iota(jnp.int32, (n, hw), 1) // NA_DH
        out = jnp.zeros((n, hw), F32)
        for h in range(NA_HEAD_GROUP):
            out = jnp.where(lb == h, o[h * n:(h + 1) * n, :], out)
        return out

    k_meta = k_ref[0:meta_blk, :]
    v_meta = v_ref[0:meta_blk, :]

    def meta_scores(qbd):
        s = _dot(qbd, k_meta, _NT)
        lane = lax.broadcasted_iota(jnp.int32, s.shape, 1)
        return jnp.where(lane < N_META, s, MASK_VALUE)

    qm = block_diag(q_ref[0:N_META, :] * (NA_DH ** -0.5))
    sm = meta_scores(qm)
    pm = jnp.exp(sm - jnp.max(sm, axis=-1, keepdims=True))
    om = _dot(pm.astype(BF16), v_meta) / jnp.sum(pm, axis=-1, keepdims=True)
    o_ref[0, 0:N_META, :] = pick_diag(om, N_META).astype(o_ref.dtype)

    def group_body(gi, carry):
        scores = []
        for j in range(_NA_ROW_GROUP):
            r = gi * _NA_ROW_GROUP + j
            rs = jnp.clip(r - NA_WIN_H // 2, 0, rows - NA_WIN_H)
            q_start = pl.multiple_of(N_META + r * GRID_W, 16)
            k_start = pl.multiple_of(N_META + rs * GRID_W, 16)
            qbd = block_diag(q_ref[pl.ds(q_start, GRID_W), :] * (NA_DH ** -0.5))
            sw = _dot(qbd, k_ref[pl.ds(k_start, n_keys), :], _NT)
            scores.append((r - rs, q_start, k_start, sw, meta_scores(qbd)))
        probs = []
        for delta, q_start, k_start, sw, smeta in scores:
            bias = jnp.concatenate(
                [t_ref[2 * j - delta + NA_WIN_H - 1] for j in range(NA_WIN_H // 2)], axis=-1)
            sw = sw + bias
            blocks = lambda a: [a[:, i:i + meta_blk] for i in range(0, n_keys, meta_blk)]
            mx = jnp.max(functools.reduce(jnp.maximum, blocks(sw), smeta), axis=-1, keepdims=True)
            pw = jnp.exp(sw - mx)
            pmeta = jnp.exp(smeta - mx)
            den = jnp.sum(functools.reduce(jnp.add, blocks(pw), pmeta), axis=-1, keepdims=True)
            probs.append((q_start, k_start, pw.astype(BF16), pmeta.astype(BF16), den))
        outs = [(q_start, _dot(pw, v_ref[pl.ds(k_start, n_keys), :]) + _dot(pmeta, v_meta), den)
                for q_start, k_start, pw, pmeta, den in probs]
        for q_start, o, den in outs:
            o_ref[0, pl.ds(q_start, GRID_W), :] = pick_diag(o / den, GRID_W).astype(o_ref.dtype)
        return carry

    lax.fori_loop(0, rows // _NA_ROW_GROUP, group_body, 0)


def _na_bias_table(rel_bias):
    depth = rel_bias.shape[0]
    col = np.arange(GRID_W)
    cs = np.clip(col - NA_WIN_W // 2, 0, GRID_W - NA_WIN_W)
    inside = (col[None, :] >= cs[:, None]) & (col[None, :] < cs[:, None] + NA_WIN_W)
    dc = col[None, :] - col[:, None] + NA_WIN_W - 1
    onehot = (np.arange(2 * NA_WIN_W - 1)[:, None, None] == dc[None]) & inside[None]
    t = jnp.einsum("lhdj,jqk->lhdqk", rel_bias.astype(F32), jnp.asarray(onehot, F32),
                   precision=lax.Precision.HIGHEST)
    t = jnp.where(inside, t, MASK_VALUE)
    t = jnp.concatenate([t[:, :, :-1], t[:, :, 1:]], axis=-1)
    n_dr = t.shape[2]
    t = t.reshape(depth, NA_HEADS // NA_HEAD_GROUP, NA_HEAD_GROUP, n_dr, GRID_W, 2 * GRID_W)
    return t.transpose(0, 1, 3, 2, 4, 5).reshape(depth, NA_HEADS // NA_HEAD_GROUP, n_dr,
                                                  NA_HEAD_GROUP * GRID_W, 2 * GRID_W)


def _na(proj, table, layer):
    bsz, seq_len, _ = proj.shape
    hw = NA_HEAD_GROUP * NA_DH
    n_groups = NA_HEADS // NA_HEAD_GROUP
    kern = functools.partial(_na_kernel, seq_len=seq_len)
    blk = lambda off: pl.BlockSpec((1, seq_len, hw), lambda g, b: (b, 0, off + g))
    return pl.pallas_call(
        kern,
        grid=(n_groups, bsz),
        in_specs=[blk(_COL_NQ // hw), blk(_COL_NK // hw), blk(_COL_NV // hw),
                  pl.BlockSpec((None, None) + table.shape[2:], lambda g, b: (layer, g, 0, 0, 0))],
        out_specs=pl.BlockSpec((1, seq_len, hw), lambda g, b: (b, 0, g)),
        out_shape=jax.ShapeDtypeStruct((bsz, seq_len, NA_HEADS * NA_DH), BF16),
        compiler_params=pltpu.CompilerParams(
            dimension_semantics=("arbitrary", "arbitrary"),
            vmem_limit_bytes=V7X_VMEM_LIMIT_BYTES),
        name="na",
    )(proj, proj, proj, table)


def _layer_norm(z, w, b):
    mu = jnp.mean(z, axis=-1, keepdims=True)
    zc = z - mu
    var = jnp.mean(zc * zc, axis=-1, keepdims=True)
    return zc * lax.rsqrt(var + LN_EPS) * w + b


def _out_ln_kernel(yg_ref, yn_ref, w_ref, h_ref, lw_ref, lb_ref, o_ref, ob_ref, *, alpha):
    kg = yg_ref.shape[1]
    mix = _dot(yg_ref[...], w_ref[0:kg, :]) + _dot(yn_ref[...], w_ref[kg:, :])
    o = _layer_norm(alpha * h_ref[...] + mix, lw_ref[...], lb_ref[...])
    o_ref[...] = o
    ob_ref[...] = o.astype(ob_ref.dtype)


def _out_ln(yg, yn, w, h, lw, lb, layer, alpha, tm):
    m, kg = yg.shape
    kn = yn.shape[1]
    n = w.shape[2]
    row = lambda width: pl.BlockSpec((tm, width), lambda i: (i, 0))
    per_layer = lambda shape: pl.BlockSpec((None,) + shape, lambda i: (layer, 0, 0))
    return pl.pallas_call(
        functools.partial(_out_ln_kernel, alpha=alpha),
        grid=(m // tm,),
        in_specs=[row(kg), row(kn), per_layer((kg + kn, n)), row(n), per_layer((1, n)), per_layer((1, n))],
        out_specs=[row(n), row(n)],
        out_shape=[jax.ShapeDtypeStruct((m, n), F32), jax.ShapeDtypeStruct((m, n), BF16)],
        compiler_params=pltpu.CompilerParams(
            dimension_semantics=("arbitrary",),
            vmem_limit_bytes=V7X_VMEM_LIMIT_BYTES),
        name="out_ln",
    )(yg, yn, w, h, lw, lb)


def _ffn_ln_kernel(a_ref, w1_ref, w2_ref, h_ref, lw_ref, lb_ref, o_ref, ob_ref=None, *, alpha):
    f = pl.program_id(1)

    @pl.when(f == 0)
    def _():
        o_ref[...] = alpha * h_ref[...]

    u = jnp.square(jnp.maximum(_dot(a_ref[...], w1_ref[...]), 0.0)).astype(BF16)
    o_ref[...] += _dot(u, w2_ref[...])

    @pl.when(f == pl.num_programs(1) - 1)
    def _():
        o = _layer_norm(o_ref[...], lw_ref[...], lb_ref[...])
        o_ref[...] = o
        if ob_ref is not None:
            ob_ref[...] = o.astype(ob_ref.dtype)


def _ffn_ln(a, w1, w2, h, lw, lb, layer, alpha, tm, tf, skip_rows):
    bsz, seq_len, d = a.shape
    d_ff = w1.shape[2]
    rows = seq_len - skip_rows
    assert rows % tm == 0 and skip_rows % 16 == 0 and tm % 16 == 0 and d_ff % tf == 0
    nt = rows // tm
    row_in = pl.BlockSpec((pl.Squeezed(), pl.Element(tm), pl.Element(d)),
                          lambda i, f: (i // nt, pl.multiple_of(skip_rows + (i % nt) * tm, 16), 0))
    row_out = pl.BlockSpec((None, tm, d), lambda i, f: (i // nt, i % nt, 0))
    vec = pl.BlockSpec((None, 1, d), lambda i, f: (layer, 0, 0))
    out_shape = [jax.ShapeDtypeStruct((bsz, rows, d), F32)]
    if not skip_rows:
        out_shape.append(jax.ShapeDtypeStruct((bsz, rows, d), BF16))
    return pl.pallas_call(
        functools.partial(_ffn_ln_kernel, alpha=alpha),
        grid=(bsz * nt, d_ff // tf),
        in_specs=[row_in, pl.BlockSpec((None, d, tf), lambda i, f: (layer, 0, f)),
                  pl.BlockSpec((None, tf, d), lambda i, f: (layer, f, 0)), row_in, vec, vec],
        out_specs=[row_out] * len(out_shape),
        out_shape=out_shape,
        compiler_params=pltpu.CompilerParams(
            dimension_semantics=("arbitrary", "arbitrary"),
            vmem_limit_bytes=V7X_VMEM_LIMIT_BYTES),
        name="ffn_ln",
    )(a, w1, w2, h, lw, lb)


_GLA_QK = GLA_HEADS * GLA_DK
_GLA_VR = GLA_HEADS * GLA_DV
_NA_W = NA_HEADS * NA_DH
_COL_GQ = 0
_COL_GK = _COL_GQ + _GLA_QK
_COL_GV = _COL_GK + _GLA_QK
_COL_GR = _COL_GV + _GLA_VR
_COL_NQ = _COL_GR + _GLA_VR
_COL_NK = _COL_NQ + _NA_W
_COL_NV = _COL_NK + _NA_W
_PROJ_COLS = _COL_NV + _NA_W
_W_IN_LR = _COL_NQ
_W_IN_COLS = _PROJ_COLS + 2 * GLA_RANK
_PROJ_TN = 1024
_LR_TN = 128
_ROW_TILE = 688
_PROJ_ROW_TILE = 1376
_OUT_ROW_TILE = 344
_FFN_TF = 512
_LAST_ROW_TILE = 512
_LAST_FFN_TF = 1024


def _gla_gate_params(w_up, b_up):
    depth = w_up.shape[0]
    wc = jnp.zeros((depth, GLA_HEADS, 128, 2 * GLA_DK), F32)
    for z in range(2):
        w = w_up[:, z].astype(F32).reshape(depth, GLA_RANK, GLA_HEADS, GLA_DK).transpose(0, 2, 1, 3)
        wc = wc.at[:, :, z * GLA_RANK:(z + 1) * GLA_RANK, z * GLA_DK:(z + 1) * GLA_DK].set(w)
    bc = b_up.astype(F32).reshape(depth, 2, GLA_HEADS, GLA_DK).transpose(0, 2, 1, 3)
    return wc.astype(BF16), bc.reshape(depth, GLA_HEADS, 1, 2 * GLA_DK)


def kernel(x, meta, w_in, gla_w_up, gla_b_up, gla_norm_w, na_rel_bias, w_out, ln1_w, ln1_b,
           w_ff1, w_ff2, ln2_w, ln2_b):
    bsz, seq, d = x.shape
    depth = w_in.shape[0]
    alpha = (2 * depth) ** 0.25
    seq_len = seq + N_META
    m = bsz * seq_len
    meta_b = jnp.broadcast_to(meta.astype(x.dtype)[None], (bsz, N_META, d))
    h = jnp.concatenate([meta_b, x], axis=1).reshape(m, d).astype(F32)
    hb = h.astype(BF16)
    assert w_in.shape[2] == _W_IN_COLS and _W_IN_LR % _PROJ_TN == 0 and _PROJ_COLS % _PROJ_TN == 0
    w_in_t = jnp.swapaxes(w_in, 1, 2)
    w_out_b, w_ff1_b, w_ff2_b = w_out.astype(BF16), w_ff1.astype(BF16), w_ff2.astype(BF16)
    wc, bc = _gla_gate_params(gla_w_up, gla_b_up)
    nw = gla_norm_w.astype(F32).reshape(depth, 1, GLA_DV)
    table = _na_bias_table(na_rel_bias)
    vec = lambda p: p.astype(F32).reshape(depth, 1, d)
    ln1_w, ln1_b, ln2_w, ln2_b = vec(ln1_w), vec(ln1_b), vec(ln2_w), vec(ln2_b)
    for l in range(depth):
        proj = _proj(hb, w_in_t, l, _PROJ_COLS // _PROJ_TN, _PROJ_TN, 0, _W_IN_LR // _PROJ_TN, 2 * GLA_RANK,
                     _PROJ_ROW_TILE, "proj").reshape(bsz, seq_len, _PROJ_COLS)
        lr = _proj(hb, w_in_t, l, 1, _LR_TN, _W_IN_LR, 1, 0, _ROW_TILE, "proj_lr").reshape(bsz, seq_len, _LR_TN)
        y_gla = _gla(proj, lr, wc, bc, nw, l).reshape(m, GLA_HEADS * GLA_DV)
        y_na = _na(proj, table, l).reshape(m, NA_HEADS * NA_DH)
        h, hb = _out_ln(y_gla, y_na, w_out_b, h, ln1_w, ln1_b, l, alpha, _OUT_ROW_TILE)
        hb3, h3 = hb.reshape(bsz, seq_len, d), h.reshape(bsz, seq_len, d)
        if l + 1 < depth:
            h, hb = _ffn_ln(hb3, w_ff1_b, w_ff2_b, h3, ln2_w, ln2_b, l, alpha, _ROW_TILE, _FFN_TF, 0)
            h, hb = h.reshape(m, d), hb.reshape(m, d)
        else:
            out, = _ffn_ln(hb3, w_ff1_b, w_ff2_b, h3, ln2_w, ln2_b, l, alpha, _LAST_ROW_TILE, _LAST_FFN_TF,
                           N_META)
    return out.astype(x.dtype)
```

```python
import functools

import numpy as np
import jax
import jax.numpy as jnp
from jax import lax
from jax.experimental import pallas as pl
from jax.experimental.pallas import tpu as pltpu

N_META = 16
GRID_W = 64
GLA_HEADS = 4
GLA_DK = 128
GLA_DV = 256
GLA_RANK = 16
GLA_TAU = 16.0
_GLA_BLOCK = 256
_GLA_GROUP = 4
_GLA_SAFE_RANGE = 80.0
NA_HEADS = 16
NA_DH = 64
NA_WIN_H = 8
NA_WIN_W = 16
NA_HEAD_GROUP = 4
_NA_ROW_GROUP = 4
LN_EPS = 1e-5
MASK_VALUE = -1e30

V7X_VMEM_LIMIT_BYTES = 58 * 1024 * 1024

F32 = jnp.float32
BF16 = jnp.bfloat16

_NT = (((1,), (1,)), ((), ()))
_TN = (((0,), (0,)), ((), ()))


def _dot(a, b, dims=None):
    if dims is None:
        return jnp.dot(a, b, preferred_element_type=F32)
    return lax.dot_general(a, b, dims, preferred_element_type=F32)


def _proj_kernel(*refs, n_cast):
    a_ref, w_ref = refs[:2]
    cast_src = refs[2:2 + n_cast]
    o_ref = refs[2 + n_cast]
    cast_dst = refs[3 + n_cast:3 + 2 * n_cast]
    wb_ref = refs[-1]

    @pl.when(pl.program_id(1) == 0)
    def _():
        wb_ref[...] = w_ref[...].astype(BF16)

    o_ref[...] = _dot(a_ref[...], wb_ref[...], _NT).astype(o_ref.dtype)
    for src, dst in zip(cast_src, cast_dst):
        dst[...] = src[...].astype(dst.dtype)


def _proj(a, wt, layer, n_tiles, tn, first, skip_from, skip, tm, name, cast=()):
    m, k = a.shape
    n_rows = m // tm
    assert first % 16 == 0 and tn % 16 == 0 and skip % 16 == 0

    def w_index(j, i):
        return (layer, pl.multiple_of(first + j * tn + jnp.where(j >= skip_from, skip, 0), 16), 0)

    cast_in, cast_out, cast_shapes = [], [], []
    for arr in cast:
        _, r, c = arr.shape
        n_slabs = max(s for s in range(1, n_tiles * n_rows + 1) if r % (16 * s) == 0)
        slab = lambda j, i, n_slabs=n_slabs: jnp.minimum(j * n_rows + i, n_slabs - 1)
        cast_in.append(pl.BlockSpec((None, r // n_slabs, c), lambda j, i, slab=slab: (layer, slab(j, i), 0)))
        cast_out.append(pl.BlockSpec((r // n_slabs, c), lambda j, i, slab=slab: (slab(j, i), 0)))
        cast_shapes.append(jax.ShapeDtypeStruct((r, c), BF16))

    return pl.pallas_call(
        functools.partial(_proj_kernel, n_cast=len(cast)),
        grid=(n_tiles, n_rows),
        in_specs=[pl.BlockSpec((tm, k), lambda j, i: (i, 0)),
                  pl.BlockSpec((pl.Squeezed(), pl.Element(tn), pl.Element(k)), w_index)] + cast_in,
        out_specs=[pl.BlockSpec((tm, tn), lambda j, i: (i, j))] + cast_out,
        out_shape=[jax.ShapeDtypeStruct((m, n_tiles * tn), BF16)] + cast_shapes,
        scratch_shapes=[pltpu.VMEM((tn, k), BF16)],
        compiler_params=pltpu.CompilerParams(
            dimension_semantics=("arbitrary", "arbitrary"),
            vmem_limit_bytes=V7X_VMEM_LIMIT_BYTES),
        name=name,
    )(a, wt, *cast)


def _gla_decays(rows, c, g_ref):
    g2 = g_ref[rows, :]
    t_idx = lax.broadcasted_iota(jnp.int32, (c, c), 0)
    s_idx = lax.broadcasted_iota(jnp.int32, (c, c), 1)
    tri = jnp.where(s_idx <= t_idx, 1.0, 0.0).astype(BF16)
    g_hi = g2.astype(BF16)
    g_lo = (g2 - g_hi.astype(F32)).astype(BF16)
    p = _dot(tri, g_hi) + _dot(tri, g_lo)
    cb = p[c - 1:c, GLA_DK:] - p[:, GLA_DK:] + g2[:, GLA_DK:]
    return p[:, :GLA_DK], cb


def _gla_group_terms(chunks, q_ref, k_ref, v_ref, g_ref, qi_ref, oi_ref, ut_ref, e_ref):
    decays = [_gla_decays(rows, c, g_ref) for rows, _, c in chunks]
    operands, ranges = [], []
    for (rows, ci, c), (bf, cb) in zip(chunks, decays):
        bf_end = bf[c - 1:c, :]
        cb_end = cb[0:1, :]
        bf_mid = bf[c // 2 - 1:c // 2, :]
        cb_mid = cb[c // 2:c // 2 + 1, :]
        q = q_ref[0, rows, :].astype(F32) * (GLA_DK ** -0.5)
        k = k_ref[0, rows, :].astype(F32)
        qi_f = q * jnp.exp(bf)
        ks_f = k * jnp.exp(bf_end - bf)
        qi_b = q * jnp.exp(cb)
        ks_b = k * jnp.exp(cb_end - cb)
        qi_ref[rows, :] = jnp.concatenate([qi_f, qi_b], axis=-1).astype(BF16)
        e_ref[ci] = jnp.broadcast_to(jnp.exp(jnp.concatenate([bf_end, cb_end], axis=-1)), (8, 2 * GLA_DK))
        operands.append(((qi_f * jnp.exp(-bf_mid)).astype(BF16), (ks_f * jnp.exp(bf_mid - bf_end)).astype(BF16),
                         (qi_b * jnp.exp(-cb_mid)).astype(BF16), (ks_b * jnp.exp(cb_mid - cb_end)).astype(BF16),
                         jnp.concatenate([ks_f, ks_b], axis=-1).astype(BF16)))
        ranges.append(jnp.maximum(-bf_end, -cb_end))
    scores = [(_dot(qt_f, kt_f, _NT), _dot(qt_b, kt_b, _NT)) for qt_f, kt_f, qt_b, kt_b, _ in operands]
    for (rows, ci, c), (sc_f, sc_b) in zip(chunks, scores):
        t_idx = lax.broadcasted_iota(jnp.int32, (c, c), 0)
        s_idx = lax.broadcasted_iota(jnp.int32, (c, c), 1)
        sc = jnp.where(s_idx <= t_idx, sc_f, 0.0) + jnp.where(s_idx >= t_idx, sc_b, 0.0)
        oi_ref[rows, :] = _dot(sc.astype(BF16), v_ref[0, rows, :])
    for (rows, ci, c), ops in zip(chunks, operands):
        ut_ref[ci] = _dot(v_ref[0, rows, :], ops[4], _TN)
    return ranges


def _gla_exact_intra(rows, size, q_ref, k_ref, v_ref, g_ref, oi_ref):
    c = size
    bf, cb = _gla_decays(rows, c, g_ref)
    q = q_ref[0, rows, :].astype(F32) * (GLA_DK ** -0.5)
    k = k_ref[0, rows, :].astype(F32)
    row = lax.broadcasted_iota(jnp.int32, (c, 1), 0)
    lane = lax.broadcasted_iota(jnp.int32, (c, c), 1)

    def body(t, sct):
        pick = row == t
        take = lambda a: jnp.sum(jnp.where(pick, a, 0.0), axis=0, keepdims=True)
        dec = (jnp.exp(jnp.where(row <= t, take(bf) - bf, MASK_VALUE))
               + jnp.exp(jnp.where(row >= t, take(cb) - cb, MASK_VALUE)))
        col = jnp.sum(k * take(q) * dec, axis=-1, keepdims=True)
        return jnp.where(lane == t, col, sct)

    sct = lax.fori_loop(0, c, body, jnp.zeros((c, c), F32))
    oi_ref[rows, :] = _dot(sct.astype(BF16), v_ref[0, rows, :], _TN)


def _gla_group_out(chunks, r_ref, nw_ref, qi_ref, oi_ref, st_ref, o_ref):
    inter = [_dot(qi_ref[rows, :], st_ref[ci], _NT) for rows, ci in chunks]
    for (rows, ci), o_inter in zip(chunks, inter):
        o = oi_ref[rows, :] + o_inter
        o = o * lax.rsqrt(jnp.mean(o * o, axis=-1, keepdims=True) + LN_EPS) * nw_ref[...]
        r = r_ref[0, rows, :].astype(F32)
        o_ref[0, rows, :] = (o * (r / (1.0 + jnp.exp(-r)))).astype(o_ref.dtype)


def _gla_kernel(q_ref, k_ref, v_ref, r_ref, lr_ref, wc_ref, bc_ref, nw_ref, o_ref,
                g_ref, qi_ref, oi_ref, ut_ref, e_ref, st_ref, *, seq_len, gate_rows):
    c = _GLA_BLOCK
    n_full = (seq_len - N_META) // c
    n_chunks = n_full + 1

    for i in range(seq_len // gate_rows):
        rows = pl.ds(i * gate_rows, gate_rows)
        x = _dot(lr_ref[0, rows, :], wc_ref[0]) + bc_ref[0]
        g_ref[rows, :] = (jnp.minimum(x, 0.0) - jnp.log(1.0 + jnp.exp(-jnp.abs(x)))) * (1.0 / GLA_TAU)

    def chunk_rows(ci):
        return pl.ds(pl.multiple_of(N_META + (ci - 1) * c, 16), c)

    terms = functools.partial(_gla_group_terms, q_ref=q_ref, k_ref=k_ref, v_ref=v_ref, g_ref=g_ref,
                              qi_ref=qi_ref, oi_ref=oi_ref, ut_ref=ut_ref, e_ref=e_ref)
    full = [(pl.ds(N_META + (ci - 1) * c, c), ci, c) for ci in range(1, n_chunks)]
    decay_range = terms([(pl.ds(0, N_META), 0, N_META)])
    for i in range(0, n_full, _GLA_GROUP):
        decay_range += terms(full[i:i + _GLA_GROUP])

    @pl.when(jnp.max(functools.reduce(jnp.maximum, decay_range)) > _GLA_SAFE_RANGE)
    def _():
        exact = functools.partial(_gla_exact_intra, q_ref=q_ref, k_ref=k_ref, v_ref=v_ref, g_ref=g_ref,
                                  oi_ref=oi_ref)
        exact(pl.ds(0, N_META), N_META)

        def exact_body(ci, carry):
            exact(chunk_rows(ci), c)
            return carry

        lax.fori_loop(1, n_chunks, exact_body, 0)

    def scan_body(i, carry):
        s_f, s_b = carry
        j = n_chunks - 1 - i
        st_ref[i, :, :GLA_DK] = s_f.astype(BF16)
        st_ref[j, :, GLA_DK:] = s_b.astype(BF16)
        s_f = s_f * e_ref[i, 0:1, :GLA_DK] + ut_ref[i, :, :GLA_DK]
        s_b = s_b * e_ref[j, 0:1, GLA_DK:] + ut_ref[j, :, GLA_DK:]
        return s_f, s_b

    zero = jnp.zeros((GLA_DV, GLA_DK), F32)
    lax.fori_loop(0, n_chunks, scan_body, (zero, zero))

    out = functools.partial(_gla_group_out, r_ref=r_ref, nw_ref=nw_ref, qi_ref=qi_ref, oi_ref=oi_ref,
                            st_ref=st_ref, o_ref=o_ref)
    out([(pl.ds(0, N_META), 0)])
    for i in range(0, n_full, _GLA_GROUP):
        out([(rows, ci) for rows, ci, _ in full[i:i + _GLA_GROUP]])


def _gla(proj, lr, wc, bc, nw, layer):
    bsz, seq_len, _ = proj.shape
    assert (seq_len - N_META) % _GLA_BLOCK == 0 and seq_len % _ROW_TILE == 0
    n_chunks = (seq_len - N_META) // _GLA_BLOCK + 1
    kern = functools.partial(_gla_kernel, seq_len=seq_len, gate_rows=_ROW_TILE)
    qk_blk = lambda off: pl.BlockSpec((1, seq_len, GLA_DK), lambda b, h: (b, 0, off + h))
    vr_blk = lambda off: pl.BlockSpec((1, seq_len, GLA_DV), lambda b, h: (b, 0, off + h))
    return pl.pallas_call(
        kern,
        grid=(bsz, GLA_HEADS),
        in_specs=[qk_blk(_COL_GQ // GLA_DK), qk_blk(_COL_GK // GLA_DK),
                  vr_blk(_COL_GV // GLA_DV), vr_blk(_COL_GR // GLA_DV),
                  pl.BlockSpec((1, seq_len, 128), lambda b, h: (b, 0, 0)),
                  pl.BlockSpec((None, 1, 128, 2 * GLA_DK), lambda b, h: (layer, h, 0, 0)),
                  pl.BlockSpec((None, 1, 1, 2 * GLA_DK), lambda b, h: (layer, h, 0, 0)),
                  pl.BlockSpec((None, 1, GLA_DV), lambda b, h: (layer, 0, 0))],
        out_specs=pl.BlockSpec((1, seq_len, GLA_DV), lambda b, h: (b, 0, h)),
        out_shape=jax.ShapeDtypeStruct((bsz, seq_len, GLA_HEADS * GLA_DV), BF16),
        scratch_shapes=[pltpu.VMEM((seq_len, 2 * GLA_DK), F32),
                        pltpu.VMEM((seq_len, 2 * GLA_DK), BF16),
                        pltpu.VMEM((seq_len, GLA_DV), F32),
                        pltpu.VMEM((n_chunks, GLA_DV, 2 * GLA_DK), F32),
                        pltpu.VMEM((n_chunks, 8, 2 * GLA_DK), F32),
                        pltpu.VMEM((n_chunks, GLA_DV, 2 * GLA_DK), BF16)],
        compiler_params=pltpu.CompilerParams(
            dimension_semantics=("arbitrary", "arbitrary"),
            vmem_limit_bytes=V7X_VMEM_LIMIT_BYTES),
        name="gla",
    )(proj, proj, proj, proj, lr, wc, bc, nw)


def _na_kernel(q_ref, k_ref, v_ref, t_ref, o_ref, *, seq_len):
    q_ref, k_ref, v_ref = q_ref.at[0], k_ref.at[0], v_ref.at[0]
    rows = (seq_len - N_META) // GRID_W
    hw = NA_HEAD_GROUP * NA_DH
    nq = NA_HEAD_GROUP * GRID_W
    n_keys = NA_WIN_H * GRID_W
    meta_blk = 128

    def block_diag(q4):
        n = q4.shape[0]
        qrep = jnp.concatenate([q4] * NA_HEAD_GROUP, axis=0)
        rb = lax.broadcasted_iota(jnp.int32, (NA_HEAD_GROUP * n, hw), 0) // n
        lb = lax.broadcasted_iota(jnp.int32, (NA_HEAD_GROUP * n, hw), 1) // NA_DH
        return jnp.where(rb == lb, qrep, jnp.zeros_like(qrep))

    def pick_diag(o, n):
        lb = lax.broadcasted_iota(jnp.int32, (n, hw), 1) // NA_DH
        out = jnp.zeros((n, hw), F32)
        for h in range(NA_HEAD_GROUP):
            out = jnp.where(lb == h, o[h * n:(h + 1) * n, :], out)
        return out

    k_meta = k_ref[0:meta_blk, :]
    v_meta = v_ref[0:meta_blk, :]

    def meta_scores(qbd):
        s = _dot(qbd, k_meta, _NT)
        lane = lax.broadcasted_iota(jnp.int32, s.shape, 1)
        return jnp.where(lane < N_META, s, MASK_VALUE)

    qm = block_diag(q_ref[0:N_META, :] * (NA_DH ** -0.5))
    sm = meta_scores(qm)
    pm = jnp.exp(sm - jnp.max(sm, axis=-1, keepdims=True))
    om = _dot(pm.astype(BF16), v_meta) / jnp.sum(pm, axis=-1, keepdims=True)
    o_ref[0, 0:N_META, :] = pick_diag(om, N_META).astype(o_ref.dtype)

    def group_body(gi, carry):
        scores = []
        for j in range(_NA_ROW_GROUP):
            r = gi * _NA_ROW_GROUP + j
            rs = jnp.clip(r - NA_WIN_H // 2, 0, rows - NA_WIN_H)
            q_start = pl.multiple_of(N_META + r * GRID_W, 16)
            k_start = pl.multiple_of(N_META + rs * GRID_W, 16)
            qbd = block_diag(q_ref[pl.ds(q_start, GRID_W), :] * (NA_DH ** -0.5))
            sw = _dot(qbd, k_ref[pl.ds(k_start, n_keys), :], _NT)
            scores.append((r - rs, q_start, k_start, sw, meta_scores(qbd)))
        probs = []
        for delta, q_start, k_start, sw, smeta in scores:
            bias = jnp.concatenate(
                [t_ref[2 * j - delta + NA_WIN_H - 1] for j in range(NA_WIN_H // 2)], axis=-1)
            sw = sw + bias
            blocks = lambda a: [a[:, i:i + meta_blk] for i in range(0, n_keys, meta_blk)]
            mx = jnp.max(functools.reduce(jnp.maximum, blocks(sw), smeta), axis=-1, keepdims=True)
            pw = jnp.exp(sw - mx)
            pmeta = jnp.exp(smeta - mx)
            den = jnp.sum(functools.reduce(jnp.add, blocks(pw), pmeta), axis=-1, keepdims=True)
            probs.append((q_start, k_start, pw.astype(BF16), pmeta.astype(BF16), den))
        outs = [(q_start, _dot(pw, v_ref[pl.ds(k_start, n_keys), :]) + _dot(pmeta, v_meta), den)
                for q_start, k_start, pw, pmeta, den in probs]
        for q_start, o, den in outs:
            o_ref[0, pl.ds(q_start, GRID_W), :] = pick_diag(o / den, GRID_W).astype(o_ref.dtype)
        return carry

    lax.fori_loop(0, rows // _NA_ROW_GROUP, group_body, 0)


def _na_bias_table(rel_bias):
    depth = rel_bias.shape[0]
    col = np.arange(GRID_W)
    cs = np.clip(col - NA_WIN_W // 2, 0, GRID_W - NA_WIN_W)
    inside = (col[None, :] >= cs[:, None]) & (col[None, :] < cs[:, None] + NA_WIN_W)
    dc = col[None, :] - col[:, None] + NA_WIN_W - 1
    onehot = (np.arange(2 * NA_WIN_W - 1)[:, None, None] == dc[None]) & inside[None]
    t = jnp.einsum("lhdj,jqk->lhdqk", rel_bias.astype(F32), jnp.asarray(onehot, F32),
                   precision=lax.Precision.HIGHEST)
    t = jnp.where(inside, t, MASK_VALUE)
    t = jnp.concatenate([t[:, :, :-1], t[:, :, 1:]], axis=-1)
    n_dr = t.shape[2]
    t = t.reshape(depth, NA_HEADS // NA_HEAD_GROUP, NA_HEAD_GROUP, n_dr, GRID_W, 2 * GRID_W)
    return t.transpose(0, 1, 3, 2, 4, 5).reshape(depth, NA_HEADS // NA_HEAD_GROUP, n_dr,
                                                  NA_HEAD_GROUP * GRID_W, 2 * GRID_W)


def _na(proj, table, layer):
    bsz, seq_len, _ = proj.shape
    hw = NA_HEAD_GROUP * NA_DH
    n_groups = NA_HEADS // NA_HEAD_GROUP
    kern = functools.partial(_na_kernel, seq_len=seq_len)
    blk = lambda off: pl.BlockSpec((1, seq_len, hw), lambda g, b: (b, 0, off + g))
    return pl.pallas_call(
        kern,
        grid=(n_groups, bsz),
        in_specs=[blk(_COL_NQ // hw), blk(_COL_NK // hw), blk(_COL_NV // hw),
                  pl.BlockSpec((None, None) + table.shape[2:], lambda g, b: (layer, g, 0, 0, 0))],
        out_specs=pl.BlockSpec((1, seq_len, hw), lambda g, b: (b, 0, g)),
        out_shape=jax.ShapeDtypeStruct((bsz, seq_len, NA_HEADS * NA_DH), BF16),
        compiler_params=pltpu.CompilerParams(
            dimension_semantics=("arbitrary", "arbitrary"),
            vmem_limit_bytes=V7X_VMEM_LIMIT_BYTES),
        name="na",
    )(proj, proj, proj, table)


def _layer_norm(z, w, b):
    mu = jnp.mean(z, axis=-1, keepdims=True)
    zc = z - mu
    var = jnp.mean(zc * zc, axis=-1, keepdims=True)
    return zc * lax.rsqrt(var + LN_EPS) * w + b


def _out_ln_kernel(yg_ref, yn_ref, w_ref, h_ref, lw_ref, lb_ref, o_ref, ob_ref, *, alpha):
    kg = yg_ref.shape[1]
    mix = _dot(yg_ref[...], w_ref[0:kg, :]) + _dot(yn_ref[...], w_ref[kg:, :])
    o = _layer_norm(alpha * h_ref[...] + mix, lw_ref[...], lb_ref[...])
    o_ref[...] = o
    ob_ref[...] = o.astype(ob_ref.dtype)


def _out_ln(yg, yn, w, h, lw, lb, layer, alpha, tm):
    m, kg = yg.shape
    kn = yn.shape[1]
    n = w.shape[1]
    row = lambda width: pl.BlockSpec((tm, width), lambda i: (i, 0))
    per_layer = lambda shape: pl.BlockSpec((None,) + shape, lambda i: (layer, 0, 0))
    return pl.pallas_call(
        functools.partial(_out_ln_kernel, alpha=alpha),
        grid=(m // tm,),
        in_specs=[row(kg), row(kn), pl.BlockSpec((kg + kn, n), lambda i: (0, 0)), row(n),
                  per_layer((1, n)), per_layer((1, n))],
        out_specs=[row(n), row(n)],
        out_shape=[jax.ShapeDtypeStruct((m, n), F32), jax.ShapeDtypeStruct((m, n), BF16)],
        compiler_params=pltpu.CompilerParams(
            dimension_semantics=("arbitrary",),
            vmem_limit_bytes=V7X_VMEM_LIMIT_BYTES),
        name="out_ln",
    )(yg, yn, w, h, lw, lb)


def _ffn_ln_kernel(a_ref, w1_ref, w2_ref, h_ref, lw_ref, lb_ref, o_ref, ob_ref=None, *, alpha):
    f = pl.program_id(1)

    @pl.when(f == 0)
    def _():
        o_ref[...] = alpha * h_ref[...]

    u = jnp.square(jnp.maximum(_dot(a_ref[...], w1_ref[...]), 0.0)).astype(BF16)
    o_ref[...] += _dot(u, w2_ref[...])

    @pl.when(f == pl.num_programs(1) - 1)
    def _():
        o = _layer_norm(o_ref[...], lw_ref[...], lb_ref[...])
        o_ref[...] = o
        if ob_ref is not None:
            ob_ref[...] = o.astype(ob_ref.dtype)


def _ffn_ln(a, w1, w2, h, lw, lb, layer, alpha, tm, tf, skip_rows):
    bsz, seq_len, d = a.shape
    d_ff = w1.shape[1]
    rows = seq_len - skip_rows
    assert rows % tm == 0 and skip_rows % 16 == 0 and tm % 16 == 0 and d_ff % tf == 0
    nt = rows // tm
    row_in = pl.BlockSpec((pl.Squeezed(), pl.Element(tm), pl.Element(d)),
                          lambda i, f: (i // nt, pl.multiple_of(skip_rows + (i % nt) * tm, 16), 0))
    row_out = pl.BlockSpec((None, tm, d), lambda i, f: (i // nt, i % nt, 0))
    vec = pl.BlockSpec((None, 1, d), lambda i, f: (layer, 0, 0))
    out_shape = [jax.ShapeDtypeStruct((bsz, rows, d), F32)]
    if not skip_rows:
        out_shape.append(jax.ShapeDtypeStruct((bsz, rows, d), BF16))
    return pl.pallas_call(
        functools.partial(_ffn_ln_kernel, alpha=alpha),
        grid=(bsz * nt, d_ff // tf),
        in_specs=[row_in, pl.BlockSpec((d, tf), lambda i, f: (0, f)),
                  pl.BlockSpec((tf, d), lambda i, f: (f, 0)), row_in, vec, vec],
        out_specs=[row_out] * len(out_shape),
        out_shape=out_shape,
        compiler_params=pltpu.CompilerParams(
            dimension_semantics=("arbitrary", "arbitrary"),
            vmem_limit_bytes=V7X_VMEM_LIMIT_BYTES),
        name="ffn_ln",
    )(a, w1, w2, h, lw, lb)


_GLA_QK = GLA_HEADS * GLA_DK
_GLA_VR = GLA_HEADS * GLA_DV
_NA_W = NA_HEADS * NA_DH
_COL_GQ = 0
_COL_GK = _COL_GQ + _GLA_QK
_COL_GV = _COL_GK + _GLA_QK
_COL_GR = _COL_GV + _GLA_VR
_COL_NQ = _COL_GR + _GLA_VR
_COL_NK = _COL_NQ + _NA_W
_COL_NV = _COL_NK + _NA_W
_PROJ_COLS = _COL_NV + _NA_W
_W_IN_LR = _COL_NQ
_W_IN_COLS = _PROJ_COLS + 2 * GLA_RANK
_PROJ_TN = 1024
_LR_TN = 128
_ROW_TILE = 688
_PROJ_ROW_TILE = 1376
_OUT_ROW_TILE = 344
_FFN_TF = 512
_LAST_ROW_TILE = 512
_LAST_FFN_TF = 1024


def _gla_gate_params(w_up, b_up):
    depth = w_up.shape[0]
    wc = jnp.zeros((depth, GLA_HEADS, 128, 2 * GLA_DK), F32)
    for z in range(2):
        w = w_up[:, z].astype(F32).reshape(depth, GLA_RANK, GLA_HEADS, GLA_DK).transpose(0, 2, 1, 3)
        wc = wc.at[:, :, z * GLA_RANK:(z + 1) * GLA_RANK, z * GLA_DK:(z + 1) * GLA_DK].set(w)
    bc = b_up.astype(F32).reshape(depth, 2, GLA_HEADS, GLA_DK).transpose(0, 2, 1, 3)
    return wc.astype(BF16), bc.reshape(depth, GLA_HEADS, 1, 2 * GLA_DK)


def kernel(x, meta, w_in, gla_w_up, gla_b_up, gla_norm_w, na_rel_bias, w_out, ln1_w, ln1_b,
           w_ff1, w_ff2, ln2_w, ln2_b):
    bsz, seq, d = x.shape
    depth = w_in.shape[0]
    alpha = (2 * depth) ** 0.25
    seq_len = seq + N_META
    m = bsz * seq_len
    meta_b = jnp.broadcast_to(meta.astype(x.dtype)[None], (bsz, N_META, d))
    h = jnp.concatenate([meta_b, x], axis=1).reshape(m, d).astype(F32)
    hb = h.astype(BF16)
    assert w_in.shape[2] == _W_IN_COLS and _W_IN_LR % _PROJ_TN == 0 and _PROJ_COLS % _PROJ_TN == 0
    w_in_t = jnp.swapaxes(w_in, 1, 2)
    wc, bc = _gla_gate_params(gla_w_up, gla_b_up)
    nw = gla_norm_w.astype(F32).reshape(depth, 1, GLA_DV)
    table = _na_bias_table(na_rel_bias)
    vec = lambda p: p.astype(F32).reshape(depth, 1, d)
    ln1_w, ln1_b, ln2_w, ln2_b = vec(ln1_w), vec(ln1_b), vec(ln2_w), vec(ln2_b)
    for l in range(depth):
        proj, w_out_b, w_ff1_b, w_ff2_b = _proj(
            hb, w_in_t, l, _PROJ_COLS // _PROJ_TN, _PROJ_TN, 0, _W_IN_LR // _PROJ_TN, 2 * GLA_RANK,
            _PROJ_ROW_TILE, "proj", cast=(w_out, w_ff1, w_ff2))
        proj = proj.reshape(bsz, seq_len, _PROJ_COLS)
        lr, = _proj(hb, w_in_t, l, 1, _LR_TN, _W_IN_LR, 1, 0, _ROW_TILE, "proj_lr")
        lr = lr.reshape(bsz, seq_len, _LR_TN)
        y_gla = _gla(proj, lr, wc, bc, nw, l).reshape(m, GLA_HEADS * GLA_DV)
        y_na = _na(proj, table, l).reshape(m, NA_HEADS * NA_DH)
        h, hb = _out_ln(y_gla, y_na, w_out_b, h, ln1_w, ln1_b, l, alpha, _OUT_ROW_TILE)
        hb3, h3 = hb.reshape(bsz, seq_len, d), h.reshape(bsz, seq_len, d)
        if l + 1 < depth:
            h, hb = _ffn_ln(hb3, w_ff1_b, w_ff2_b, h3, ln2_w, ln2_b, l, alpha, _ROW_TILE, _FFN_TF, 0)
            h, hb = h.reshape(m, d), hb.reshape(m, d)
        else:
            out, = _ffn_ln(hb3, w_ff1_b, w_ff2_b, h3, ln2_w, ln2_b, l, alpha, _LAST_ROW_TILE, _LAST_FFN_TF,
                           N_META)
    return out.astype(x.dtype)
```

```python
import functools

import numpy as np
import jax
import jax.numpy as jnp
from jax import lax
from jax.experimental import pallas as pl
from jax.experimental.pallas import tpu as pltpu

N_META = 16
GRID_W = 64
GLA_HEADS = 4
GLA_DK = 128
GLA_DV = 256
GLA_RANK = 16
GLA_TAU = 16.0
_GLA_BLOCK = 256
_GLA_GROUP = 4
_GLA_SAFE_RANGE = 80.0
NA_HEADS = 16
NA_DH = 64
NA_WIN_H = 8
NA_WIN_W = 16
NA_HEAD_GROUP = 4
_NA_ROW_GROUP = 4
LN_EPS = 1e-5
MASK_VALUE = -1e30

V7X_VMEM_LIMIT_BYTES = 58 * 1024 * 1024

F32 = jnp.float32
BF16 = jnp.bfloat16

_NT = (((1,), (1,)), ((), ()))
_TN = (((0,), (0,)), ((), ()))


def _dot(a, b, dims=None):
    if dims is None:
        return jnp.dot(a, b, preferred_element_type=F32)
    return lax.dot_general(a, b, dims, preferred_element_type=F32)


def _proj_kernel(*refs, n_cast):
    a_ref, w_ref = refs[:2]
    cast_src = refs[2:2 + n_cast]
    o_ref = refs[2 + n_cast]
    cast_dst = refs[3 + n_cast:3 + 2 * n_cast]
    wb_ref = refs[-1]

    @pl.when(pl.program_id(1) == 0)
    def _():
        wb_ref[...] = w_ref[...].astype(BF16)

    o_ref[...] = _dot(a_ref[...], wb_ref[...], _NT).astype(o_ref.dtype)
    for src, dst in zip(cast_src, cast_dst):
        dst[...] = src[...].astype(dst.dtype)


def _proj(a, wt, layer, n_tiles, tn, first, skip_from, skip, tm, name, cast=()):
    m, k = a.shape
    n_rows = m // tm
    assert first % 16 == 0 and tn % 16 == 0 and skip % 16 == 0

    def w_index(j, i):
        return (layer, pl.multiple_of(first + j * tn + jnp.where(j >= skip_from, skip, 0), 16), 0)

    cast_in, cast_out, cast_shapes = [], [], []
    for arr in cast:
        _, r, c = arr.shape
        n_slabs = max(s for s in range(1, n_tiles * n_rows + 1) if r % (16 * s) == 0)
        slab = lambda j, i, n_slabs=n_slabs: jnp.minimum(j * n_rows + i, n_slabs - 1)
        cast_in.append(pl.BlockSpec((None, r // n_slabs, c), lambda j, i, slab=slab: (layer, slab(j, i), 0)))
        cast_out.append(pl.BlockSpec((r // n_slabs, c), lambda j, i, slab=slab: (slab(j, i), 0)))
        cast_shapes.append(jax.ShapeDtypeStruct((r, c), BF16))

    return pl.pallas_call(
        functools.partial(_proj_kernel, n_cast=len(cast)),
        grid=(n_tiles, n_rows),
        in_specs=[pl.BlockSpec((tm, k), lambda j, i: (i, 0)),
                  pl.BlockSpec((pl.Squeezed(), pl.Element(tn), pl.Element(k)), w_index)] + cast_in,
        out_specs=[pl.BlockSpec((tm, tn), lambda j, i: (i, j))] + cast_out,
        out_shape=[jax.ShapeDtypeStruct((m, n_tiles * tn), BF16)] + cast_shapes,
        scratch_shapes=[pltpu.VMEM((tn, k), BF16)],
        compiler_params=pltpu.CompilerParams(
            dimension_semantics=("arbitrary", "arbitrary"),
            vmem_limit_bytes=V7X_VMEM_LIMIT_BYTES),
        name=name,
    )(a, wt, *cast)


def _gla_decays(rows, c, g_ref):
    g2 = g_ref[rows, :]
    t_idx = lax.broadcasted_iota(jnp.int32, (c, c), 0)
    s_idx = lax.broadcasted_iota(jnp.int32, (c, c), 1)
    tri = jnp.where(s_idx <= t_idx, 1.0, 0.0).astype(BF16)
    g_hi = g2.astype(BF16)
    g_lo = (g2 - g_hi.astype(F32)).astype(BF16)
    p = _dot(tri, g_hi) + _dot(tri, g_lo)
    cb = p[c - 1:c, GLA_DK:] - p[:, GLA_DK:] + g2[:, GLA_DK:]
    return p[:, :GLA_DK], cb


def _gla_group_terms(chunks, q_ref, k_ref, v_ref, g_ref, qi_ref, oi_ref, ut_ref, e_ref):
    decays = [_gla_decays(rows, c, g_ref) for rows, _, c in chunks]
    operands, ranges = [], []
    for (rows, ci, c), (bf, cb) in zip(chunks, decays):
        bf_end = bf[c - 1:c, :]
        cb_end = cb[0:1, :]
        bf_mid = bf[c // 2 - 1:c // 2, :]
        cb_mid = cb[c // 2:c // 2 + 1, :]
        q = q_ref[0, rows, :].astype(F32) * (GLA_DK ** -0.5)
        k = k_ref[0, rows, :].astype(F32)
        qi_f = q * jnp.exp(bf)
        ks_f = k * jnp.exp(bf_end - bf)
        qi_b = q * jnp.exp(cb)
        ks_b = k * jnp.exp(cb_end - cb)
        qi_ref[rows, :] = jnp.concatenate([qi_f, qi_b], axis=-1).astype(BF16)
        e_ref[ci] = jnp.broadcast_to(jnp.exp(jnp.concatenate([bf_end, cb_end], axis=-1)), (8, 2 * GLA_DK))
        operands.append(((qi_f * jnp.exp(-bf_mid)).astype(BF16), (ks_f * jnp.exp(bf_mid - bf_end)).astype(BF16),
                         (qi_b * jnp.exp(-cb_mid)).astype(BF16), (ks_b * jnp.exp(cb_mid - cb_end)).astype(BF16),
                         jnp.concatenate([ks_f, ks_b], axis=-1).astype(BF16)))
        ranges.append(jnp.maximum(-bf_end, -cb_end))
    scores = [(_dot(qt_f, kt_f, _NT), _dot(qt_b, kt_b, _NT)) for qt_f, kt_f, qt_b, kt_b, _ in operands]
    for (rows, ci, c), (sc_f, sc_b) in zip(chunks, scores):
        t_idx = lax.broadcasted_iota(jnp.int32, (c, c), 0)
        s_idx = lax.broadcasted_iota(jnp.int32, (c, c), 1)
        sc = jnp.where(s_idx <= t_idx, sc_f, 0.0) + jnp.where(s_idx >= t_idx, sc_b, 0.0)
        oi_ref[rows, :] = _dot(sc.astype(BF16), v_ref[0, rows, :])
    for (rows, ci, c), ops in zip(chunks, operands):
        ut_ref[ci] = _dot(v_ref[0, rows, :], ops[4], _TN)
    return ranges


def _gla_exact_intra(rows, size, q_ref, k_ref, v_ref, g_ref, oi_ref):
    c = size
    bf, cb = _gla_decays(rows, c, g_ref)
    q = q_ref[0, rows, :].astype(F32) * (GLA_DK ** -0.5)
    k = k_ref[0, rows, :].astype(F32)
    row = lax.broadcasted_iota(jnp.int32, (c, 1), 0)
    lane = lax.broadcasted_iota(jnp.int32, (c, c), 1)

    def body(t, sct):
        pick = row == t
        take = lambda a: jnp.sum(jnp.where(pick, a, 0.0), axis=0, keepdims=True)
        dec = (jnp.exp(jnp.where(row <= t, take(bf) - bf, MASK_VALUE))
               + jnp.exp(jnp.where(row >= t, take(cb) - cb, MASK_VALUE)))
        col = jnp.sum(k * take(q) * dec, axis=-1, keepdims=True)
        return jnp.where(lane == t, col, sct)

    sct = lax.fori_loop(0, c, body, jnp.zeros((c, c), F32))
    oi_ref[rows, :] = _dot(sct.astype(BF16), v_ref[0, rows, :], _TN)


def _gla_group_out(chunks, r_ref, nw_ref, qi_ref, oi_ref, st_ref, o_ref):
    inter = [_dot(qi_ref[rows, :], st_ref[ci], _NT) for rows, ci in chunks]
    for (rows, ci), o_inter in zip(chunks, inter):
        o = oi_ref[rows, :] + o_inter
        o = o * lax.rsqrt(jnp.mean(o * o, axis=-1, keepdims=True) + LN_EPS) * nw_ref[...]
        r = r_ref[0, rows, :].astype(F32)
        o_ref[0, rows, :] = (o * (r / (1.0 + jnp.exp(-r)))).astype(o_ref.dtype)


def _gla_kernel(q_ref, k_ref, v_ref, r_ref, lr_ref, wc_ref, bc_ref, nw_ref, o_ref,
                g_ref, qi_ref, oi_ref, ut_ref, e_ref, st_ref, *, seq_len, gate_rows):
    c = _GLA_BLOCK
    n_full = (seq_len - N_META) // c
    n_chunks = n_full + 1

    for i in range(seq_len // gate_rows):
        rows = pl.ds(i * gate_rows, gate_rows)
        x = _dot(lr_ref[0, rows, :], wc_ref[0]) + bc_ref[0]
        g_ref[rows, :] = (jnp.minimum(x, 0.0) - jnp.log(1.0 + jnp.exp(-jnp.abs(x)))) * (1.0 / GLA_TAU)

    def chunk_rows(ci):
        return pl.ds(pl.multiple_of(N_META + (ci - 1) * c, 16), c)

    terms = functools.partial(_gla_group_terms, q_ref=q_ref, k_ref=k_ref, v_ref=v_ref, g_ref=g_ref,
                              qi_ref=qi_ref, oi_ref=oi_ref, ut_ref=ut_ref, e_ref=e_ref)
    full = [(pl.ds(N_META + (ci - 1) * c, c), ci, c) for ci in range(1, n_chunks)]
    decay_range = terms([(pl.ds(0, N_META), 0, N_META)])
    for i in range(0, n_full, _GLA_GROUP):
        decay_range += terms(full[i:i + _GLA_GROUP])

    @pl.when(jnp.max(functools.reduce(jnp.maximum, decay_range)) > _GLA_SAFE_RANGE)
    def _():
        exact = functools.partial(_gla_exact_intra, q_ref=q_ref, k_ref=k_ref, v_ref=v_ref, g_ref=g_ref,
                                  oi_ref=oi_ref)
        exact(pl.ds(0, N_META), N_META)

        def exact_body(ci, carry):
            exact(chunk_rows(ci), c)
            return carry

        lax.fori_loop(1, n_chunks, exact_body, 0)

    def scan_body(i, carry):
        s_f, s_b = carry
        j = n_chunks - 1 - i
        st_ref[i, :, :GLA_DK] = s_f.astype(BF16)
        st_ref[j, :, GLA_DK:] = s_b.astype(BF16)
        s_f = s_f * e_ref[i, 0:1, :GLA_DK] + ut_ref[i, :, :GLA_DK]
        s_b = s_b * e_ref[j, 0:1, GLA_DK:] + ut_ref[j, :, GLA_DK:]
        return s_f, s_b

    zero = jnp.zeros((GLA_DV, GLA_DK), F32)
    lax.fori_loop(0, n_chunks, scan_body, (zero, zero))

    out = functools.partial(_gla_group_out, r_ref=r_ref, nw_ref=nw_ref, qi_ref=qi_ref, oi_ref=oi_ref,
                            st_ref=st_ref, o_ref=o_ref)
    out([(pl.ds(0, N_META), 0)])
    for i in range(0, n_full, _GLA_GROUP):
        out([(rows, ci) for rows, ci, _ in full[i:i + _GLA_GROUP]])


def _gla(proj, lr, wc, bc, nw, layer):
    bsz, seq_len, _ = proj.shape
    assert (seq_len - N_META) % _GLA_BLOCK == 0 and seq_len % _ROW_TILE == 0
    n_chunks = (seq_len - N_META) // _GLA_BLOCK + 1
    kern = functools.partial(_gla_kernel, seq_len=seq_len, gate_rows=_ROW_TILE)
    qk_blk = lambda off: pl.BlockSpec((1, seq_len, GLA_DK), lambda b, h: (b, 0, off + h))
    vr_blk = lambda off: pl.BlockSpec((1, seq_len, GLA_DV), lambda b, h: (b, 0, off + h))
    return pl.pallas_call(
        kern,
        grid=(bsz, GLA_HEADS),
        in_specs=[qk_blk(_COL_GQ // GLA_DK), qk_blk(_COL_GK // GLA_DK),
                  vr_blk(_COL_GV // GLA_DV), vr_blk(_COL_GR // GLA_DV),
                  pl.BlockSpec((1, seq_len, 128), lambda b, h: (b, 0, 0)),
                  pl.BlockSpec((None, 1, 128, 2 * GLA_DK), lambda b, h: (layer, h, 0, 0)),
                  pl.BlockSpec((None, 1, 1, 2 * GLA_DK), lambda b, h: (layer, h, 0, 0)),
                  pl.BlockSpec((None, 1, GLA_DV), lambda b, h: (layer, 0, 0))],
        out_specs=pl.BlockSpec((1, seq_len, GLA_DV), lambda b, h: (b, 0, h)),
        out_shape=jax.ShapeDtypeStruct((bsz, seq_len, GLA_HEADS * GLA_DV), BF16),
        scratch_shapes=[pltpu.VMEM((seq_len, 2 * GLA_DK), F32),
                        pltpu.VMEM((seq_len, 2 * GLA_DK), BF16),
                        pltpu.VMEM((seq_len, GLA_DV), F32),
                        pltpu.VMEM((n_chunks, GLA_DV, 2 * GLA_DK), F32),
                        pltpu.VMEM((n_chunks, 8, 2 * GLA_DK), F32),
                        pltpu.VMEM((n_chunks, GLA_DV, 2 * GLA_DK), BF16)],
        compiler_params=pltpu.CompilerParams(
            dimension_semantics=("arbitrary", "arbitrary"),
            vmem_limit_bytes=V7X_VMEM_LIMIT_BYTES),
        name="gla",
    )(proj, proj, proj, proj, lr, wc, bc, nw)


def _na_kernel(q_ref, k_ref, v_ref, t_ref, o_ref, *, seq_len):
    q_ref, k_ref, v_ref = q_ref.at[0], k_ref.at[0], v_ref.at[0]
    rows = (seq_len - N_META) // GRID_W
    hw = NA_HEAD_GROUP * NA_DH
    nq = NA_HEAD_GROUP * GRID_W
    n_keys = NA_WIN_H * GRID_W
    meta_blk = 128

    def block_diag(q4):
        n = q4.shape[0]
        qrep = jnp.concatenate([q4] * NA_HEAD_GROUP, axis=0)
        rb = lax.broadcasted_iota(jnp.int32, (NA_HEAD_GROUP * n, hw), 0) // n
        lb = lax.broadcasted_iota(jnp.int32, (NA_HEAD_GROUP * n, hw), 1) // NA_DH
        return jnp.where(rb == lb, qrep, jnp.zeros_like(qrep))

    def pick_diag(o, n):
        lb = lax.broadcasted_iota(jnp.int32, (n, hw), 1) // NA_DH
        out = jnp.zeros((n, hw), F32)
        for h in range(NA_HEAD_GROUP):
            out = jnp.where(lb == h, o[h * n:(h + 1) * n, :], out)
        return out

    k_meta = k_ref[0:meta_blk, :]
    v_meta = v_ref[0:meta_blk, :]

    def meta_scores(qbd):
        s = _dot(qbd, k_meta, _NT)
        lane = lax.broadcasted_iota(jnp.int32, s.shape, 1)
        return jnp.where(lane < N_META, s, MASK_VALUE)

    qm = block_diag(q_ref[0:N_META, :] * (NA_DH ** -0.5))
    sm = meta_scores(qm)
    pm = jnp.exp(sm - jnp.max(sm, axis=-1, keepdims=True))
    om = _dot(pm.astype(BF16), v_meta) / jnp.sum(pm, axis=-1, keepdims=True)
    o_ref[0, 0:N_META, :] = pick_diag(om, N_META).astype(o_ref.dtype)

    def group_body(gi, carry):
        scores = []
        for j in range(_NA_ROW_GROUP):
            r = gi * _NA_ROW_GROUP + j
            rs = jnp.clip(r - NA_WIN_H // 2, 0, rows - NA_WIN_H)
            q_start = pl.multiple_of(N_META + r * GRID_W, 16)
            k_start = pl.multiple_of(N_META + rs * GRID_W, 16)
            qbd = block_diag(q_ref[pl.ds(q_start, GRID_W), :] * (NA_DH ** -0.5))
            sw = _dot(qbd, k_ref[pl.ds(k_start, n_keys), :], _NT)
            scores.append((r - rs, q_start, k_start, sw, meta_scores(qbd)))
        probs = []
        for delta, q_start, k_start, sw, smeta in scores:
            bias = jnp.concatenate(
                [t_ref[2 * j - delta + NA_WIN_H - 1] for j in range(NA_WIN_H // 2)], axis=-1)
            sw = sw + bias
            blocks = lambda a: [a[:, i:i + meta_blk] for i in range(0, n_keys, meta_blk)]
            mx = jnp.max(functools.reduce(jnp.maximum, blocks(sw), smeta), axis=-1, keepdims=True)
            pw = jnp.exp(sw - mx)
            pmeta = jnp.exp(smeta - mx)
            den = jnp.sum(functools.reduce(jnp.add, blocks(pw), pmeta), axis=-1, keepdims=True)
            probs.append((q_start, k_start, pw.astype(BF16), pmeta.astype(BF16), den))
        outs = [(q_start, _dot(pw, v_ref[pl.ds(k_start, n_keys), :]) + _dot(pmeta, v_meta), den)
                for q_start, k_start, pw, pmeta, den in probs]
        for q_start, o, den in outs:
            o_ref[0, pl.ds(q_start, GRID_W), :] = pick_diag(o / den, GRID_W).astype(o_ref.dtype)
        return carry

    lax.fori_loop(0, rows // _NA_ROW_GROUP, group_body, 0)


def _na_bias_table(rel_bias):
    depth = rel_bias.shape[0]
    col = np.arange(GRID_W)
    cs = np.clip(col - NA_WIN_W // 2, 0, GRID_W - NA_WIN_W)
    inside = (col[None, :] >= cs[:, None]) & (col[None, :] < cs[:, None] + NA_WIN_W)
    dc = col[None, :] - col[:, None] + NA_WIN_W - 1
    onehot = (np.arange(2 * NA_WIN_W - 1)[:, None, None] == dc[None]) & inside[None]
    t = jnp.einsum("lhdj,jqk->lhdqk", rel_bias.astype(F32), jnp.asarray(onehot, F32),
                   precision=lax.Precision.HIGHEST)
    t = jnp.where(inside, t, MASK_VALUE)
    t = jnp.concatenate([t[:, :, :-1], t[:, :, 1:]], axis=-1)
    n_dr = t.shape[2]
    t = t.reshape(depth, NA_HEADS // NA_HEAD_GROUP, NA_HEAD_GROUP, n_dr, GRID_W, 2 * GRID_W)
    return t.transpose(0, 1, 3, 2, 4, 5).reshape(depth, NA_HEADS // NA_HEAD_GROUP, n_dr,
                                                  NA_HEAD_GROUP * GRID_W, 2 * GRID_W)


def _na(proj, table, layer):
    bsz, seq_len, _ = proj.shape
    hw = NA_HEAD_GROUP * NA_DH
    n_groups = NA_HEADS // NA_HEAD_GROUP
    kern = functools.partial(_na_kernel, seq_len=seq_len)
    blk = lambda off: pl.BlockSpec((1, seq_len, hw), lambda g, b: (b, 0, off + g))
    return pl.pallas_call(
        kern,
        grid=(n_groups, bsz),
        in_specs=[blk(_COL_NQ // hw), blk(_COL_NK // hw), blk(_COL_NV // hw),
                  pl.BlockSpec((None, None) + table.shape[2:], lambda g, b: (layer, g, 0, 0, 0))],
        out_specs=pl.BlockSpec((1, seq_len, hw), lambda g, b: (b, 0, g)),
        out_shape=jax.ShapeDtypeStruct((bsz, seq_len, NA_HEADS * NA_DH), BF16),
        compiler_params=pltpu.CompilerParams(
            dimension_semantics=("arbitrary", "arbitrary"),
            vmem_limit_bytes=V7X_VMEM_LIMIT_BYTES),
        name="na",
    )(proj, proj, proj, table)


def _layer_norm(z, w, b):
    mu = jnp.mean(z, axis=-1, keepdims=True)
    zc = z - mu
    var = jnp.mean(zc * zc, axis=-1, keepdims=True)
    return zc * lax.rsqrt(var + LN_EPS) * w + b


def _out_ln_kernel(yg_ref, yn_ref, w_ref, *rest, alpha, sub_rows, tiles_per_seq):
    from_tokens = len(rest) == 6
    res_ref = rest[0]
    lw_ref, lb_ref, o_ref, ob_ref = rest[-4:]
    kg = yg_ref.shape[1]
    starts = [sum(sub_rows[:i]) for i in range(len(sub_rows))]
    mixes = [_dot(yg_ref[r0:r0 + n, :], w_ref[0:kg, :]) + _dot(yn_ref[r0:r0 + n, :], w_ref[kg:, :])
             for r0, n in zip(starts, sub_rows)]
    for r0, n, mix in zip(starts, sub_rows, mixes):
        if not from_tokens:
            res = res_ref[r0:r0 + n, :]
        else:
            meta_ref = rest[1]
            if r0 == 0:
                head = jnp.concatenate([meta_ref[...], res_ref[0:n - N_META, :]], axis=0)
            else:
                head = res_ref[r0 - N_META:r0 - N_META + n, :]
            first = pl.program_id(0) % tiles_per_seq == 0
            res = jnp.where(first, head, res_ref[r0:r0 + n, :])
        o = _layer_norm(alpha * res + mix, lw_ref[...], lb_ref[...])
        o_ref[r0:r0 + n, :] = o
        ob_ref[r0:r0 + n, :] = o.astype(ob_ref.dtype)


def _out_ln(yg, yn, w, res, lw, lb, layer, alpha, tm, sub_rows, seq_len):
    m, kg = yg.shape
    kn = yn.shape[1]
    n = w.shape[1]
    assert sum(sub_rows) == tm and seq_len % tm == 0 and all(r % 16 == 0 for r in sub_rows)
    tiles_per_seq = seq_len // tm
    row = lambda width: pl.BlockSpec((tm, width), lambda i: (i, 0))
    per_layer = lambda shape: pl.BlockSpec((None,) + shape, lambda i: (layer, 0, 0))
    if isinstance(res, tuple):
        x_spec = pl.BlockSpec(
            (pl.Squeezed(), pl.Element(tm), pl.Element(n)),
            lambda i: (i // tiles_per_seq,
                       pl.multiple_of(jnp.maximum((i % tiles_per_seq) * tm - N_META, 0), 8), 0))
        res_specs = [x_spec, pl.BlockSpec((N_META, n), lambda i: (0, 0))]
    else:
        res, res_specs = (res,), [row(n)]
    return pl.pallas_call(
        functools.partial(_out_ln_kernel, alpha=alpha, sub_rows=sub_rows, tiles_per_seq=tiles_per_seq),
        grid=(m // tm,),
        in_specs=[row(kg), row(kn),
                  pl.BlockSpec((kg + kn, n), lambda i: (0, 0), pipeline_mode=pl.Buffered(1))]
        + res_specs + [per_layer((1, n)), per_layer((1, n))],
        out_specs=[row(n), row(n)],
        out_shape=[jax.ShapeDtypeStruct((m, n), F32), jax.ShapeDtypeStruct((m, n), BF16)],
        compiler_params=pltpu.CompilerParams(
            dimension_semantics=("arbitrary",),
            vmem_limit_bytes=V7X_VMEM_LIMIT_BYTES),
        name="out_ln",
    )(yg, yn, w, *res, lw, lb)


def _ffn_ln_kernel(a_ref, w1_ref, w2_ref, h_ref, lw_ref, lb_ref, o_ref, ob_ref=None, *, alpha):
    f = pl.program_id(1)

    @pl.when(f == 0)
    def _():
        o_ref[...] = alpha * h_ref[...]

    u = jnp.square(jnp.maximum(_dot(a_ref[...], w1_ref[...]), 0.0)).astype(BF16)
    o_ref[...] += _dot(u, w2_ref[...])

    @pl.when(f == pl.num_programs(1) - 1)
    def _():
        o = _layer_norm(o_ref[...], lw_ref[...], lb_ref[...])
        o_ref[...] = o
        if ob_ref is not None:
            ob_ref[...] = o.astype(ob_ref.dtype)


def _ffn_ln(a, w1, w2, h, lw, lb, layer, alpha, tm, tf, skip_rows):
    bsz, seq_len, d = a.shape
    d_ff = w1.shape[1]
    rows = seq_len - skip_rows
    assert rows % tm == 0 and skip_rows % 16 == 0 and tm % 16 == 0 and d_ff % tf == 0
    nt = rows // tm
    row_in = pl.BlockSpec((pl.Squeezed(), pl.Element(tm), pl.Element(d)),
                          lambda i, f: (i // nt, pl.multiple_of(skip_rows + (i % nt) * tm, 16), 0))
    row_out = pl.BlockSpec((None, tm, d), lambda i, f: (i // nt, i % nt, 0))
    vec = pl.BlockSpec((None, 1, d), lambda i, f: (layer, 0, 0))
    out_shape = [jax.ShapeDtypeStruct((bsz, rows, d), F32)]
    if not skip_rows:
        out_shape.append(jax.ShapeDtypeStruct((bsz, rows, d), BF16))
    return pl.pallas_call(
        functools.partial(_ffn_ln_kernel, alpha=alpha),
        grid=(bsz * nt, d_ff // tf),
        in_specs=[row_in, pl.BlockSpec((d, tf), lambda i, f: (0, f)),
                  pl.BlockSpec((tf, d), lambda i, f: (f, 0)), row_in, vec, vec],
        out_specs=[row_out] * len(out_shape),
        out_shape=out_shape,
        compiler_params=pltpu.CompilerParams(
            dimension_semantics=("arbitrary", "arbitrary"),
            vmem_limit_bytes=V7X_VMEM_LIMIT_BYTES),
        name="ffn_ln",
    )(a, w1, w2, h, lw, lb)


_GLA_QK = GLA_HEADS * GLA_DK
_GLA_VR = GLA_HEADS * GLA_DV
_NA_W = NA_HEADS * NA_DH
_COL_GQ = 0
_COL_GK = _COL_GQ + _GLA_QK
_COL_GV = _COL_GK + _GLA_QK
_COL_GR = _COL_GV + _GLA_VR
_COL_NQ = _COL_GR + _GLA_VR
_COL_NK = _COL_NQ + _NA_W
_COL_NV = _COL_NK + _NA_W
_PROJ_COLS = _COL_NV + _NA_W
_W_IN_LR = _COL_NQ
_W_IN_COLS = _PROJ_COLS + 2 * GLA_RANK
_PROJ_TN = 1024
_LR_TN = 128
_ROW_TILE = 688
_PROJ_ROW_TILE = 1376
_OUT_ROW_TILE = 688
_OUT_SUB_ROWS = (352, 336)
_FFN_TF = 512
_LAST_ROW_TILE = 512
_LAST_FFN_TF = 1024


def _gla_gate_params(w_up, b_up):
    depth = w_up.shape[0]
    wc = jnp.zeros((depth, GLA_HEADS, 128, 2 * GLA_DK), F32)
    for z in range(2):
        w = w_up[:, z].astype(F32).reshape(depth, GLA_RANK, GLA_HEADS, GLA_DK).transpose(0, 2, 1, 3)
        wc = wc.at[:, :, z * GLA_RANK:(z + 1) * GLA_RANK, z * GLA_DK:(z + 1) * GLA_DK].set(w)
    bc = b_up.astype(F32).reshape(depth, 2, GLA_HEADS, GLA_DK).transpose(0, 2, 1, 3)
    return wc.astype(BF16), bc.reshape(depth, GLA_HEADS, 1, 2 * GLA_DK)


def kernel(x, meta, w_in, gla_w_up, gla_b_up, gla_norm_w, na_rel_bias, w_out, ln1_w, ln1_b,
           w_ff1, w_ff2, ln2_w, ln2_b):
    bsz, seq, d = x.shape
    depth = w_in.shape[0]
    alpha = (2 * depth) ** 0.25
    seq_len = seq + N_META
    m = bsz * seq_len
    meta_b = jnp.broadcast_to(meta.astype(BF16)[None], (bsz, N_META, d))
    hb = jnp.concatenate([meta_b, x.astype(BF16)], axis=1).reshape(m, d)
    h = (x.astype(F32), meta.astype(F32))
    assert w_in.shape[2] == _W_IN_COLS and _W_IN_LR % _PROJ_TN == 0 and _PROJ_COLS % _PROJ_TN == 0
    w_in_t = jnp.swapaxes(w_in, 1, 2)
    wc, bc = _gla_gate_params(gla_w_up, gla_b_up)
    nw = gla_norm_w.astype(F32).reshape(depth, 1, GLA_DV)
    table = _na_bias_table(na_rel_bias)
    vec = lambda p: p.astype(F32).reshape(depth, 1, d)
    ln1_w, ln1_b, ln2_w, ln2_b = vec(ln1_w), vec(ln1_b), vec(ln2_w), vec(ln2_b)
    for l in range(depth):
        proj, w_out_b, w_ff1_b, w_ff2_b = _proj(
            hb, w_in_t, l, _PROJ_COLS // _PROJ_TN, _PROJ_TN, 0, _W_IN_LR // _PROJ_TN, 2 * GLA_RANK,
            _PROJ_ROW_TILE, "proj", cast=(w_out, w_ff1, w_ff2))
        proj = proj.reshape(bsz, seq_len, _PROJ_COLS)
        lr, = _proj(hb, w_in_t, l, 1, _LR_TN, _W_IN_LR, 1, 0, _ROW_TILE, "proj_lr")
        lr = lr.reshape(bsz, seq_len, _LR_TN)
        y_gla = _gla(proj, lr, wc, bc, nw, l).reshape(m, GLA_HEADS * GLA_DV)
        y_na = _na(proj, table, l).reshape(m, NA_HEADS * NA_DH)
        h, hb = _out_ln(y_gla, y_na, w_out_b, h, ln1_w, ln1_b, l, alpha, _OUT_ROW_TILE, _OUT_SUB_ROWS, seq_len)
        hb3, h3 = hb.reshape(bsz, seq_len, d), h.reshape(bsz, seq_len, d)
        if l + 1 < depth:
            h, hb = _ffn_ln(hb3, w_ff1_b, w_ff2_b, h3, ln2_w, ln2_b, l, alpha, _ROW_TILE, _FFN_TF, 0)
            h, hb = h.reshape(m, d), hb.reshape(m, d)
        else:
            out, = _ffn_ln(hb3, w_ff1_b, w_ff2_b, h3, ln2_w, ln2_b, l, alpha, _LAST_ROW_TILE, _LAST_FFN_TF,
                           N_META)
    return out.astype(x.dtype)
```

```python
import functools

import numpy as np
import jax
import jax.numpy as jnp
from jax import lax
from jax.experimental import pallas as pl
from jax.experimental.pallas import tpu as pltpu

N_META = 16
GRID_W = 64
GLA_HEADS = 4
GLA_DK = 128
GLA_DV = 256
GLA_RANK = 16
GLA_TAU = 16.0
_GLA_BLOCK = 256
_GLA_GROUP = 4
_GLA_SAFE_RANGE = 115.0
LOG2_E = 1.4426950408889634
NA_HEADS = 16
NA_DH = 64
NA_WIN_H = 8
NA_WIN_W = 16
NA_HEAD_GROUP = 4
_NA_ROW_GROUP = 4
LN_EPS = 1e-5
MASK_VALUE = -1e30

V7X_VMEM_LIMIT_BYTES = 58 * 1024 * 1024

F32 = jnp.float32
BF16 = jnp.bfloat16

_NT = (((1,), (1,)), ((), ()))
_TN = (((0,), (0,)), ((), ()))


def _dot(a, b, dims=None):
    if dims is None:
        return jnp.dot(a, b, preferred_element_type=F32)
    return lax.dot_general(a, b, dims, preferred_element_type=F32)


def _proj_kernel(*refs, n_cast):
    a_ref, w_ref = refs[:2]
    cast_src = refs[2:2 + n_cast]
    o_ref = refs[2 + n_cast]
    cast_dst = refs[3 + n_cast:3 + 2 * n_cast]
    wb_ref = refs[-1]

    @pl.when(pl.program_id(1) == 0)
    def _():
        wb_ref[...] = w_ref[...].astype(BF16)

    o_ref[...] = _dot(a_ref[...], wb_ref[...], _NT).astype(o_ref.dtype)
    for src, dst in zip(cast_src, cast_dst):
        dst[...] = src[...].astype(dst.dtype)


def _proj(a, wt, layer, n_tiles, tn, first, skip_from, skip, tm, name, cast=()):
    m, k = a.shape
    n_rows = m // tm
    assert first % 16 == 0 and tn % 16 == 0 and skip % 16 == 0

    def w_index(j, i):
        return (layer, pl.multiple_of(first + j * tn + jnp.where(j >= skip_from, skip, 0), 16), 0)

    cast_in, cast_out, cast_shapes = [], [], []
    for arr in cast:
        _, r, c = arr.shape
        n_slabs = max(s for s in range(1, n_tiles * n_rows + 1) if r % (16 * s) == 0)
        slab = lambda j, i, n_slabs=n_slabs: jnp.minimum(j * n_rows + i, n_slabs - 1)
        cast_in.append(pl.BlockSpec((None, r // n_slabs, c), lambda j, i, slab=slab: (layer, slab(j, i), 0)))
        cast_out.append(pl.BlockSpec((r // n_slabs, c), lambda j, i, slab=slab: (slab(j, i), 0)))
        cast_shapes.append(jax.ShapeDtypeStruct((r, c), BF16))

    return pl.pallas_call(
        functools.partial(_proj_kernel, n_cast=len(cast)),
        grid=(n_tiles, n_rows),
        in_specs=[pl.BlockSpec((tm, k), lambda j, i: (i, 0)),
                  pl.BlockSpec((pl.Squeezed(), pl.Element(tn), pl.Element(k)), w_index)] + cast_in,
        out_specs=[pl.BlockSpec((tm, tn), lambda j, i: (i, j))] + cast_out,
        out_shape=[jax.ShapeDtypeStruct((m, n_tiles * tn), BF16)] + cast_shapes,
        scratch_shapes=[pltpu.VMEM((tn, k), BF16)],
        compiler_params=pltpu.CompilerParams(
            dimension_semantics=("arbitrary", "arbitrary"),
            vmem_limit_bytes=V7X_VMEM_LIMIT_BYTES),
        name=name,
    )(a, wt, *cast)


def _gla_decays(rows, c, g_ref):
    g2 = g_ref[rows, :]
    t_idx = lax.broadcasted_iota(jnp.int32, (c, c), 0)
    s_idx = lax.broadcasted_iota(jnp.int32, (c, c), 1)
    tri = jnp.where(s_idx <= t_idx, 1.0, 0.0).astype(BF16)
    g_hi = g2.astype(BF16)
    g_lo = (g2 - g_hi.astype(F32)).astype(BF16)
    p = _dot(tri, g_hi) + _dot(tri, g_lo)
    cb = p[c - 1:c, GLA_DK:] - p[:, GLA_DK:] + g2[:, GLA_DK:]
    return p[:, :GLA_DK], cb


def _gla_group_terms(chunks, q_ref, k_ref, v_ref, g_ref, qi_ref, oi_ref, ut_ref, e_ref):
    decays = [_gla_decays(rows, c, g_ref) for rows, _, c in chunks]
    operands, ranges = [], []
    for (rows, ci, c), (bf, cb) in zip(chunks, decays):
        bf_end = bf[c - 1:c, :]
        cb_end = cb[0:1, :]
        bf_mid = bf[c // 2 - 1:c // 2, :]
        cb_mid = cb[c // 2:c // 2 + 1, :]
        q = q_ref[0, rows, :].astype(F32) * (GLA_DK ** -0.5)
        k = k_ref[0, rows, :].astype(F32)
        qi_f = q * jnp.exp2(bf)
        ks_f = k * jnp.exp2(bf_end - bf)
        qi_b = q * jnp.exp2(cb)
        ks_b = k * jnp.exp2(cb_end - cb)
        qi_ref[rows, :] = jnp.concatenate([qi_f, qi_b], axis=-1).astype(BF16)
        e_ref[ci] = jnp.broadcast_to(jnp.exp2(jnp.concatenate([bf_end, cb_end], axis=-1)), (8, 2 * GLA_DK))
        operands.append(((qi_f * jnp.exp2(-bf_mid)).astype(BF16), (ks_f * jnp.exp2(bf_mid - bf_end)).astype(BF16),
                         (qi_b * jnp.exp2(-cb_mid)).astype(BF16), (ks_b * jnp.exp2(cb_mid - cb_end)).astype(BF16),
                         jnp.concatenate([ks_f, ks_b], axis=-1).astype(BF16)))
        ranges.append(jnp.maximum(-bf_end, -cb_end))
    scores = [(_dot(qt_f, kt_f, _NT), _dot(qt_b, kt_b, _NT)) for qt_f, kt_f, qt_b, kt_b, _ in operands]
    def causal_mix(sc_f, sc_b):
        n = sc_f.shape[0]
        t_idx = lax.broadcasted_iota(jnp.int32, (n, n), 0)
        s_idx = lax.broadcasted_iota(jnp.int32, (n, n), 1)
        return jnp.where(s_idx <= t_idx, sc_f, 0.0) + jnp.where(s_idx >= t_idx, sc_b, 0.0)

    for (rows, ci, c), (sc_f, sc_b) in zip(chunks, scores):
        if c % 256 == 0:
            h = c // 2
            top = jnp.concatenate([causal_mix(sc_f[:h, :h], sc_b[:h, :h]), sc_b[:h, h:]], axis=1)
            bot = jnp.concatenate([sc_f[h:, :h], causal_mix(sc_f[h:, h:], sc_b[h:, h:])], axis=1)
            sc = jnp.concatenate([top, bot], axis=0)
        else:
            sc = causal_mix(sc_f, sc_b)
        oi_ref[rows, :] = _dot(sc.astype(BF16), v_ref[0, rows, :])
    for (rows, ci, c), ops in zip(chunks, operands):
        ut_ref[ci] = _dot(v_ref[0, rows, :], ops[4], _TN)
    return ranges


def _gla_exact_intra(rows, size, q_ref, k_ref, v_ref, g_ref, oi_ref):
    c = size
    bf, cb = _gla_decays(rows, c, g_ref)
    q = q_ref[0, rows, :].astype(F32) * (GLA_DK ** -0.5)
    k = k_ref[0, rows, :].astype(F32)
    row = lax.broadcasted_iota(jnp.int32, (c, 1), 0)
    lane = lax.broadcasted_iota(jnp.int32, (c, c), 1)

    def body(t, sct):
        pick = row == t
        take = lambda a: jnp.sum(jnp.where(pick, a, 0.0), axis=0, keepdims=True)
        dec = (jnp.exp2(jnp.where(row <= t, take(bf) - bf, MASK_VALUE))
               + jnp.exp2(jnp.where(row >= t, take(cb) - cb, MASK_VALUE)))
        col = jnp.sum(k * take(q) * dec, axis=-1, keepdims=True)
        return jnp.where(lane == t, col, sct)

    sct = lax.fori_loop(0, c, body, jnp.zeros((c, c), F32))
    oi_ref[rows, :] = _dot(sct.astype(BF16), v_ref[0, rows, :], _TN)


def _gla_group_out(chunks, r_ref, nw_ref, qi_ref, oi_ref, st_ref, o_ref):
    inter = [_dot(qi_ref[rows, :], st_ref[ci], _NT) for rows, ci in chunks]
    for (rows, ci), o_inter in zip(chunks, inter):
        o = oi_ref[rows, :] + o_inter
        o = o * lax.rsqrt(jnp.mean(o * o, axis=-1, keepdims=True) + LN_EPS) * nw_ref[...]
        r = r_ref[0, rows, :].astype(F32)
        o_ref[0, rows, :] = (o * (r / (1.0 + jnp.exp(-r)))).astype(o_ref.dtype)


def _gla_kernel(q_ref, k_ref, v_ref, r_ref, lr_ref, wc_ref, bc_ref, nw_ref, o_ref,
                g_ref, qi_ref, oi_ref, ut_ref, e_ref, st_ref, *, seq_len, gate_rows):
    c = _GLA_BLOCK
    n_full = (seq_len - N_META) // c
    n_chunks = n_full + 1

    for i in range(seq_len // gate_rows):
        rows = pl.ds(i * gate_rows, gate_rows)
        x = _dot(lr_ref[0, rows, :], wc_ref[0]) + bc_ref[0]
        g_ref[rows, :] = (jnp.minimum(x, 0.0) - jnp.log(1.0 + jnp.exp(-jnp.abs(x)))) * (LOG2_E / GLA_TAU)

    def chunk_rows(ci):
        return pl.ds(pl.multiple_of(N_META + (ci - 1) * c, 16), c)

    terms = functools.partial(_gla_group_terms, q_ref=q_ref, k_ref=k_ref, v_ref=v_ref, g_ref=g_ref,
                              qi_ref=qi_ref, oi_ref=oi_ref, ut_ref=ut_ref, e_ref=e_ref)
    full = [(pl.ds(N_META + (ci - 1) * c, c), ci, c) for ci in range(1, n_chunks)]
    decay_range = terms([(pl.ds(0, N_META), 0, N_META)])
    for i in range(0, n_full, _GLA_GROUP):
        decay_range += terms(full[i:i + _GLA_GROUP])

    @pl.when(jnp.max(functools.reduce(jnp.maximum, decay_range)) > _GLA_SAFE_RANGE)
    def _():
        exact = functools.partial(_gla_exact_intra, q_ref=q_ref, k_ref=k_ref, v_ref=v_ref, g_ref=g_ref,
                                  oi_ref=oi_ref)
        exact(pl.ds(0, N_META), N_META)

        def exact_body(ci, carry):
            exact(chunk_rows(ci), c)
            return carry

        lax.fori_loop(1, n_chunks, exact_body, 0)

    def scan_body(i, carry):
        s_f, s_b = carry
        j = n_chunks - 1 - i
        st_ref[i, :, :GLA_DK] = s_f.astype(BF16)
        st_ref[j, :, GLA_DK:] = s_b.astype(BF16)
        s_f = s_f * e_ref[i, 0:1, :GLA_DK] + ut_ref[i, :, :GLA_DK]
        s_b = s_b * e_ref[j, 0:1, GLA_DK:] + ut_ref[j, :, GLA_DK:]
        return s_f, s_b

    zero = jnp.zeros((GLA_DV, GLA_DK), F32)
    lax.fori_loop(0, n_chunks, scan_body, (zero, zero))

    out = functools.partial(_gla_group_out, r_ref=r_ref, nw_ref=nw_ref, qi_ref=qi_ref, oi_ref=oi_ref,
                            st_ref=st_ref, o_ref=o_ref)
    out([(pl.ds(0, N_META), 0)])
    for i in range(0, n_full, _GLA_GROUP):
        out([(rows, ci) for rows, ci, _ in full[i:i + _GLA_GROUP]])


def _gla(proj, lr, wc, bc, nw, layer):
    bsz, seq_len, _ = proj.shape
    assert (seq_len - N_META) % _GLA_BLOCK == 0 and seq_len % _ROW_TILE == 0
    n_chunks = (seq_len - N_META) // _GLA_BLOCK + 1
    kern = functools.partial(_gla_kernel, seq_len=seq_len, gate_rows=_ROW_TILE)
    qk_blk = lambda off: pl.BlockSpec((1, seq_len, GLA_DK), lambda b, h: (b, 0, off + h))
    vr_blk = lambda off: pl.BlockSpec((1, seq_len, GLA_DV), lambda b, h: (b, 0, off + h))
    return pl.pallas_call(
        kern,
        grid=(bsz, GLA_HEADS),
        in_specs=[qk_blk(_COL_GQ // GLA_DK), qk_blk(_COL_GK // GLA_DK),
                  vr_blk(_COL_GV // GLA_DV), vr_blk(_COL_GR // GLA_DV),
                  pl.BlockSpec((1, seq_len, 128), lambda b, h: (b, 0, 0)),
                  pl.BlockSpec((None, 1, 128, 2 * GLA_DK), lambda b, h: (layer, h, 0, 0)),
                  pl.BlockSpec((None, 1, 1, 2 * GLA_DK), lambda b, h: (layer, h, 0, 0)),
                  pl.BlockSpec((None, 1, GLA_DV), lambda b, h: (layer, 0, 0))],
        out_specs=pl.BlockSpec((1, seq_len, GLA_DV), lambda b, h: (b, 0, h)),
        out_shape=jax.ShapeDtypeStruct((bsz, seq_len, GLA_HEADS * GLA_DV), BF16),
        scratch_shapes=[pltpu.VMEM((seq_len, 2 * GLA_DK), F32),
                        pltpu.VMEM((seq_len, 2 * GLA_DK), BF16),
                        pltpu.VMEM((seq_len, GLA_DV), F32),
                        pltpu.VMEM((n_chunks, GLA_DV, 2 * GLA_DK), F32),
                        pltpu.VMEM((n_chunks, 8, 2 * GLA_DK), F32),
                        pltpu.VMEM((n_chunks, GLA_DV, 2 * GLA_DK), BF16)],
        compiler_params=pltpu.CompilerParams(
            dimension_semantics=("arbitrary", "arbitrary"),
            vmem_limit_bytes=V7X_VMEM_LIMIT_BYTES),
        name="gla",
    )(proj, proj, proj, proj, lr, wc, bc, nw)


def _na_kernel(q_ref, k_ref, v_ref, t_ref, o_ref, *, seq_len):
    q_ref, k_ref, v_ref = q_ref.at[0], k_ref.at[0], v_ref.at[0]
    scaled = lambda q: (q.astype(F32) * (NA_DH ** -0.5 * LOG2_E)).astype(BF16)
    rows = (seq_len - N_META) // GRID_W
    hw = NA_HEAD_GROUP * NA_DH
    nq = NA_HEAD_GROUP * GRID_W
    n_keys = NA_WIN_H * GRID_W
    meta_blk = 128

    def block_diag(q4):
        n = q4.shape[0]
        qrep = jnp.concatenate([q4] * NA_HEAD_GROUP, axis=0)
        rb = lax.broadcasted_iota(jnp.int32, (NA_HEAD_GROUP * n, hw), 0) // n
        lb = lax.broadcasted_iota(jnp.int32, (NA_HEAD_GROUP * n, hw), 1) // NA_DH
        return jnp.where(rb == lb, qrep, jnp.zeros_like(qrep))

    def pick_diag(o, n):
        lb = lax.broadcasted_iota(jnp.int32, (n, hw), 1) // NA_DH
        out = jnp.zeros((n, hw), F32)
        for h in range(NA_HEAD_GROUP):
            out = jnp.where(lb == h, o[h * n:(h + 1) * n, :], out)
        return out

    k_meta = k_ref[0:meta_blk, :]
    v_meta = v_ref[0:meta_blk, :]

    def meta_scores(qbd):
        s = _dot(qbd, k_meta, _NT)
        lane = lax.broadcasted_iota(jnp.int32, s.shape, 1)
        return jnp.where(lane < N_META, s, MASK_VALUE)

    qm = block_diag(scaled(q_ref[0:N_META, :]))
    sm = meta_scores(qm)
    pm = jnp.exp2(sm - jnp.max(sm, axis=-1, keepdims=True))
    om = _dot(pm.astype(BF16), v_meta) / jnp.sum(pm, axis=-1, keepdims=True)
    o_ref[0, 0:N_META, :] = pick_diag(om, N_META).astype(o_ref.dtype)

    def group_body(gi, carry):
        scores = []
        for j in range(_NA_ROW_GROUP):
            r = gi * _NA_ROW_GROUP + j
            rs = jnp.clip(r - NA_WIN_H // 2, 0, rows - NA_WIN_H)
            q_start = pl.multiple_of(N_META + r * GRID_W, 16)
            k_start = pl.multiple_of(N_META + rs * GRID_W, 16)
            qbd = block_diag(scaled(q_ref[pl.ds(q_start, GRID_W), :]))
            sw = _dot(qbd, k_ref[pl.ds(k_start, n_keys), :], _NT)
            scores.append((r - rs, q_start, k_start, sw, meta_scores(qbd)))
        probs = []
        for delta, q_start, k_start, sw, smeta in scores:
            bias = jnp.concatenate(
                [t_ref[2 * j - delta + NA_WIN_H - 1] for j in range(NA_WIN_H // 2)], axis=-1)
            sw = sw + bias
            blocks = lambda a: [a[:, i:i + meta_blk] for i in range(0, n_keys, meta_blk)]
            mx = jnp.max(functools.reduce(jnp.maximum, blocks(sw), smeta), axis=-1, keepdims=True)
            pw = jnp.exp2(sw - mx)
            pmeta = jnp.exp2(smeta - mx)
            den = jnp.sum(functools.reduce(jnp.add, blocks(pw), pmeta), axis=-1, keepdims=True)
            probs.append((q_start, k_start, pw.astype(BF16), pmeta.astype(BF16), den))
        outs = [(q_start, _dot(pw, v_ref[pl.ds(k_start, n_keys), :]) + _dot(pmeta, v_meta), den)
                for q_start, k_start, pw, pmeta, den in probs]
        for q_start, o, den in outs:
            o_ref[0, pl.ds(q_start, GRID_W), :] = pick_diag(o / den, GRID_W).astype(o_ref.dtype)
        return carry

    lax.fori_loop(0, rows // _NA_ROW_GROUP, group_body, 0)


def _na_bias_table(rel_bias):
    depth = rel_bias.shape[0]
    col = np.arange(GRID_W)
    cs = np.clip(col - NA_WIN_W // 2, 0, GRID_W - NA_WIN_W)
    inside = (col[None, :] >= cs[:, None]) & (col[None, :] < cs[:, None] + NA_WIN_W)
    dc = col[None, :] - col[:, None] + NA_WIN_W - 1
    onehot = (np.arange(2 * NA_WIN_W - 1)[:, None, None] == dc[None]) & inside[None]
    t = jnp.einsum("lhdj,jqk->lhdqk", rel_bias.astype(F32), jnp.asarray(onehot, F32),
                   precision=lax.Precision.HIGHEST)
    t = jnp.where(inside, t * LOG2_E, MASK_VALUE)
    t = jnp.concatenate([t[:, :, :-1], t[:, :, 1:]], axis=-1)
    n_dr = t.shape[2]
    t = t.reshape(depth, NA_HEADS // NA_HEAD_GROUP, NA_HEAD_GROUP, n_dr, GRID_W, 2 * GRID_W)
    return t.transpose(0, 1, 3, 2, 4, 5).reshape(depth, NA_HEADS // NA_HEAD_GROUP, n_dr,
                                                  NA_HEAD_GROUP * GRID_W, 2 * GRID_W)


def _na(proj, table, layer):
    bsz, seq_len, _ = proj.shape
    hw = NA_HEAD_GROUP * NA_DH
    n_groups = NA_HEADS // NA_HEAD_GROUP
    kern = functools.partial(_na_kernel, seq_len=seq_len)
    blk = lambda off: pl.BlockSpec((1, seq_len, hw), lambda g, b: (b, 0, off + g))
    return pl.pallas_call(
        kern,
        grid=(n_groups, bsz),
        in_specs=[blk(_COL_NQ // hw), blk(_COL_NK // hw), blk(_COL_NV // hw),
                  pl.BlockSpec((None, None) + table.shape[2:], lambda g, b: (layer, g, 0, 0, 0))],
        out_specs=pl.BlockSpec((1, seq_len, hw), lambda g, b: (b, 0, g)),
        out_shape=jax.ShapeDtypeStruct((bsz, seq_len, NA_HEADS * NA_DH), BF16),
        compiler_params=pltpu.CompilerParams(
            dimension_semantics=("arbitrary", "arbitrary"),
            vmem_limit_bytes=V7X_VMEM_LIMIT_BYTES),
        name="na",
    )(proj, proj, proj, table)


def _layer_norm(z, w, b):
    mu = jnp.mean(z, axis=-1, keepdims=True)
    zc = z - mu
    var = jnp.mean(zc * zc, axis=-1, keepdims=True)
    return zc * lax.rsqrt(var + LN_EPS) * w + b


def _out_ln_kernel(yg_ref, yn_ref, w_ref, *rest, alpha, sub_rows, tiles_per_seq):
    from_tokens = len(rest) == 6
    res_ref = rest[0]
    lw_ref, lb_ref, o_ref, ob_ref = rest[-4:]
    kg = yg_ref.shape[1]
    starts = [sum(sub_rows[:i]) for i in range(len(sub_rows))]
    mixes = [_dot(yg_ref[r0:r0 + n, :], w_ref[0:kg, :]) + _dot(yn_ref[r0:r0 + n, :], w_ref[kg:, :])
             for r0, n in zip(starts, sub_rows)]
    for r0, n, mix in zip(starts, sub_rows, mixes):
        if not from_tokens:
            res = res_ref[r0:r0 + n, :]
        else:
            meta_ref = rest[1]
            if r0 == 0:
                head = jnp.concatenate([meta_ref[...], res_ref[0:n - N_META, :]], axis=0)
            else:
                head = res_ref[r0 - N_META:r0 - N_META + n, :]
            first = pl.program_id(0) % tiles_per_seq == 0
            res = jnp.where(first, head, res_ref[r0:r0 + n, :])
        o = _layer_norm(alpha * res + mix, lw_ref[...], lb_ref[...])
        o_ref[r0:r0 + n, :] = o
        ob_ref[r0:r0 + n, :] = o.astype(ob_ref.dtype)


def _out_ln(yg, yn, w, res, lw, lb, layer, alpha, tm, sub_rows, seq_len):
    m, kg = yg.shape
    kn = yn.shape[1]
    n = w.shape[1]
    assert sum(sub_rows) == tm and seq_len % tm == 0 and all(r % 16 == 0 for r in sub_rows)
    tiles_per_seq = seq_len // tm
    row = lambda width: pl.BlockSpec((tm, width), lambda i: (i, 0))
    per_layer = lambda shape: pl.BlockSpec((None,) + shape, lambda i: (layer, 0, 0))
    if isinstance(res, tuple):
        x_spec = pl.BlockSpec(
            (pl.Squeezed(), pl.Element(tm), pl.Element(n)),
            lambda i: (i // tiles_per_seq,
                       pl.multiple_of(jnp.maximum((i % tiles_per_seq) * tm - N_META, 0), 8), 0))
        res_specs = [x_spec, pl.BlockSpec((N_META, n), lambda i: (0, 0))]
    else:
        res, res_specs = (res,), [row(n)]
    return pl.pallas_call(
        functools.partial(_out_ln_kernel, alpha=alpha, sub_rows=sub_rows, tiles_per_seq=tiles_per_seq),
        grid=(m // tm,),
        in_specs=[row(kg), row(kn),
                  pl.BlockSpec((kg + kn, n), lambda i: (0, 0), pipeline_mode=pl.Buffered(1))]
        + res_specs + [per_layer((1, n)), per_layer((1, n))],
        out_specs=[row(n), row(n)],
        out_shape=[jax.ShapeDtypeStruct((m, n), F32), jax.ShapeDtypeStruct((m, n), BF16)],
        compiler_params=pltpu.CompilerParams(
            dimension_semantics=("arbitrary",),
            vmem_limit_bytes=V7X_VMEM_LIMIT_BYTES),
        name="out_ln",
    )(yg, yn, w, *res, lw, lb)


def _ffn_ln_kernel(a_ref, w1_ref, w2_ref, h_ref, lw_ref, lb_ref, o_ref, ob_ref=None, *, alpha):
    f = pl.program_id(1)

    @pl.when(f == 0)
    def _():
        o_ref[...] = alpha * h_ref[...]

    u = jnp.square(jnp.maximum(_dot(a_ref[...], w1_ref[...]), 0.0)).astype(BF16)
    o_ref[...] += _dot(u, w2_ref[...])

    @pl.when(f == pl.num_programs(1) - 1)
    def _():
        o = _layer_norm(o_ref[...], lw_ref[...], lb_ref[...])
        o_ref[...] = o
        if ob_ref is not None:
            ob_ref[...] = o.astype(ob_ref.dtype)


def _ffn_ln(a, w1, w2, h, lw, lb, layer, alpha, tm, tf, skip_rows):
    bsz, seq_len, d = a.shape
    d_ff = w1.shape[1]
    rows = seq_len - skip_rows
    assert rows % tm == 0 and skip_rows % 16 == 0 and tm % 16 == 0 and d_ff % tf == 0
    nt = rows // tm
    row_in = pl.BlockSpec((pl.Squeezed(), pl.Element(tm), pl.Element(d)),
                          lambda i, f: (i // nt, pl.multiple_of(skip_rows + (i % nt) * tm, 16), 0))
    row_out = pl.BlockSpec((None, tm, d), lambda i, f: (i // nt, i % nt, 0))
    vec = pl.BlockSpec((None, 1, d), lambda i, f: (layer, 0, 0))
    out_shape = [jax.ShapeDtypeStruct((bsz, rows, d), F32)]
    if not skip_rows:
        out_shape.append(jax.ShapeDtypeStruct((bsz, rows, d), BF16))
    return pl.pallas_call(
        functools.partial(_ffn_ln_kernel, alpha=alpha),
        grid=(bsz * nt, d_ff // tf),
        in_specs=[row_in, pl.BlockSpec((d, tf), lambda i, f: (0, f)),
                  pl.BlockSpec((tf, d), lambda i, f: (f, 0)), row_in, vec, vec],
        out_specs=[row_out] * len(out_shape),
        out_shape=out_shape,
        compiler_params=pltpu.CompilerParams(
            dimension_semantics=("arbitrary", "arbitrary"),
            vmem_limit_bytes=V7X_VMEM_LIMIT_BYTES),
        name="ffn_ln",
    )(a, w1, w2, h, lw, lb)


_GLA_QK = GLA_HEADS * GLA_DK
_GLA_VR = GLA_HEADS * GLA_DV
_NA_W = NA_HEADS * NA_DH
_COL_GQ = 0
_COL_GK = _COL_GQ + _GLA_QK
_COL_GV = _COL_GK + _GLA_QK
_COL_GR = _COL_GV + _GLA_VR
_COL_NQ = _COL_GR + _GLA_VR
_COL_NK = _COL_NQ + _NA_W
_COL_NV = _COL_NK + _NA_W
_PROJ_COLS = _COL_NV + _NA_W
_W_IN_LR = _COL_NQ
_W_IN_COLS = _PROJ_COLS + 2 * GLA_RANK
_PROJ_TN = 1024
_LR_TN = 128
_ROW_TILE = 688
_PROJ_ROW_TILE = 1376
_OUT_ROW_TILE = 688
_OUT_SUB_ROWS = (352, 336)
_FFN_TF = 512
_LAST_ROW_TILE = 512
_LAST_FFN_TF = 1024


def _gla_gate_params(w_up, b_up):
    depth = w_up.shape[0]
    wc = jnp.zeros((depth, GLA_HEADS, 128, 2 * GLA_DK), F32)
    for z in range(2):
        w = w_up[:, z].astype(F32).reshape(depth, GLA_RANK, GLA_HEADS, GLA_DK).transpose(0, 2, 1, 3)
        wc = wc.at[:, :, z * GLA_RANK:(z + 1) * GLA_RANK, z * GLA_DK:(z + 1) * GLA_DK].set(w)
    bc = b_up.astype(F32).reshape(depth, 2, GLA_HEADS, GLA_DK).transpose(0, 2, 1, 3)
    return wc.astype(BF16), bc.reshape(depth, GLA_HEADS, 1, 2 * GLA_DK)


def kernel(x, meta, w_in, gla_w_up, gla_b_up, gla_norm_w, na_rel_bias, w_out, ln1_w, ln1_b,
           w_ff1, w_ff2, ln2_w, ln2_b):
    bsz, seq, d = x.shape
    depth = w_in.shape[0]
    alpha = (2 * depth) ** 0.25
    seq_len = seq + N_META
    m = bsz * seq_len
    meta_b = jnp.broadcast_to(meta.astype(BF16)[None], (bsz, N_META, d))
    hb = jnp.pad(x.astype(BF16), ((0, 0), (N_META, 0), (0, 0)))
    hb = lax.dynamic_update_slice(hb, meta_b, (0, 0, 0)).reshape(m, d)
    h = (x.astype(F32), meta.astype(F32))
    assert w_in.shape[2] == _W_IN_COLS and _W_IN_LR % _PROJ_TN == 0 and _PROJ_COLS % _PROJ_TN == 0
    w_in_t = jnp.swapaxes(w_in, 1, 2)
    wc, bc = _gla_gate_params(gla_w_up, gla_b_up)
    nw = gla_norm_w.astype(F32).reshape(depth, 1, GLA_DV)
    table = _na_bias_table(na_rel_bias)
    vec = lambda p: p.astype(F32).reshape(depth, 1, d)
    ln1_w, ln1_b, ln2_w, ln2_b = vec(ln1_w), vec(ln1_b), vec(ln2_w), vec(ln2_b)
    for l in range(depth):
        proj, w_out_b, w_ff1_b, w_ff2_b = _proj(
            hb, w_in_t, l, _PROJ_COLS // _PROJ_TN, _PROJ_TN, 0, _W_IN_LR // _PROJ_TN, 2 * GLA_RANK,
            _PROJ_ROW_TILE, "proj", cast=(w_out, w_ff1, w_ff2))
        proj = proj.reshape(bsz, seq_len, _PROJ_COLS)
        lr, = _proj(hb, w_in_t, l, 1, _LR_TN, _W_IN_LR, 1, 0, _ROW_TILE, "proj_lr")
        lr = lr.reshape(bsz, seq_len, _LR_TN)
        y_gla = _gla(proj, lr, wc, bc, nw, l).reshape(m, GLA_HEADS * GLA_DV)
        y_na = _na(proj, table, l).reshape(m, NA_HEADS * NA_DH)
        h, hb = _out_ln(y_gla, y_na, w_out_b, h, ln1_w, ln1_b, l, alpha, _OUT_ROW_TILE, _OUT_SUB_ROWS, seq_len)
        hb3, h3 = hb.reshape(bsz, seq_len, d), h.reshape(bsz, seq_len, d)
        if l + 1 < depth:
            h, hb = _ffn_ln(hb3, w_ff1_b, w_ff2_b, h3, ln2_w, ln2_b, l, alpha, _ROW_TILE, _FFN_TF, 0)
            h, hb = h.reshape(m, d), hb.reshape(m, d)
        else:
            out, = _ffn_ln(hb3, w_ff1_b, w_ff2_b, h3, ln2_w, ln2_b, l, alpha, _LAST_ROW_TILE, _LAST_FFN_TF,
                           N_META)
    return out.astype(x.dtype)
```

```python
import functools

import numpy as np
import jax
import jax.numpy as jnp
from jax import lax
from jax.experimental import pallas as pl
from jax.experimental.pallas import tpu as pltpu

N_META = 16
GRID_W = 64
GLA_HEADS = 4
GLA_DK = 128
GLA_DV = 256
GLA_RANK = 16
GLA_TAU = 16.0
_GLA_BLOCK = 256
_GLA_GROUP = 4
_GLA_SAFE_RANGE = 115.0
LOG2_E = 1.4426950408889634
NA_HEADS = 16
NA_DH = 64
NA_WIN_H = 8
NA_WIN_W = 16
NA_HEAD_GROUP = 4
_NA_ROW_GROUP = 4
LN_EPS = 1e-5
MASK_VALUE = -1e30

V7X_VMEM_LIMIT_BYTES = 58 * 1024 * 1024

F32 = jnp.float32
BF16 = jnp.bfloat16

_NT = (((1,), (1,)), ((), ()))
_TN = (((0,), (0,)), ((), ()))


def _dot(a, b, dims=None):
    if dims is None:
        return jnp.dot(a, b, preferred_element_type=F32)
    return lax.dot_general(a, b, dims, preferred_element_type=F32)


def _proj_kernel(*refs, n_cast):
    a_ref, w_ref = refs[:2]
    cast_src = refs[2:2 + n_cast]
    o_ref = refs[2 + n_cast]
    cast_dst = refs[3 + n_cast:3 + 2 * n_cast]
    wb_ref = refs[-1]

    @pl.when(pl.program_id(1) == 0)
    def _():
        wb_ref[...] = w_ref[...].astype(BF16)

    o_ref[...] = _dot(a_ref[...], wb_ref[...], _NT).astype(o_ref.dtype)
    for src, dst in zip(cast_src, cast_dst):
        dst[...] = src[...].astype(dst.dtype)


def _proj(a, wt, layer, n_tiles, tn, first, skip_from, skip, tm, name, cast=()):
    m, k = a.shape
    n_rows = m // tm
    assert first % 16 == 0 and tn % 16 == 0 and skip % 16 == 0

    def w_index(j, i):
        return (layer, pl.multiple_of(first + j * tn + jnp.where(j >= skip_from, skip, 0), 16), 0)

    cast_in, cast_out, cast_shapes = [], [], []
    for arr in cast:
        _, r, c = arr.shape
        n_slabs = max(s for s in range(1, n_tiles * n_rows + 1) if r % (16 * s) == 0)
        slab = lambda j, i, n_slabs=n_slabs: jnp.minimum(j * n_rows + i, n_slabs - 1)
        cast_in.append(pl.BlockSpec((None, r // n_slabs, c), lambda j, i, slab=slab: (layer, slab(j, i), 0)))
        cast_out.append(pl.BlockSpec((r // n_slabs, c), lambda j, i, slab=slab: (slab(j, i), 0)))
        cast_shapes.append(jax.ShapeDtypeStruct((r, c), BF16))

    return pl.pallas_call(
        functools.partial(_proj_kernel, n_cast=len(cast)),
        grid=(n_tiles, n_rows),
        in_specs=[pl.BlockSpec((tm, k), lambda j, i: (i, 0)),
                  pl.BlockSpec((pl.Squeezed(), pl.Element(tn), pl.Element(k)), w_index)] + cast_in,
        out_specs=[pl.BlockSpec((tm, tn), lambda j, i: (i, j))] + cast_out,
        out_shape=[jax.ShapeDtypeStruct((m, n_tiles * tn), BF16)] + cast_shapes,
        scratch_shapes=[pltpu.VMEM((tn, k), BF16)],
        compiler_params=pltpu.CompilerParams(
            dimension_semantics=("arbitrary", "arbitrary"),
            vmem_limit_bytes=V7X_VMEM_LIMIT_BYTES),
        name=name,
    )(a, wt, *cast)


def _gla_decays(rows, c, g_ref):
    g2 = g_ref[rows, :]
    t_idx = lax.broadcasted_iota(jnp.int32, (c, c), 0)
    s_idx = lax.broadcasted_iota(jnp.int32, (c, c), 1)
    tri = jnp.where(s_idx <= t_idx, 1.0, 0.0).astype(BF16)
    g_hi = g2.astype(BF16)
    g_lo = (g2 - g_hi.astype(F32)).astype(BF16)
    p = _dot(tri, g_hi) + _dot(tri, g_lo)
    cb = p[c - 1:c, GLA_DK:] - p[:, GLA_DK:] + g2[:, GLA_DK:]
    return p[:, :GLA_DK], cb


def _gla_group_terms(chunks, q_ref, k_ref, v_ref, g_ref, qi_ref, oi_ref, ut_ref, e_ref):
    decays = [_gla_decays(rows, c, g_ref) for rows, _, c in chunks]
    operands, ranges = [], []
    for (rows, ci, c), (bf, cb) in zip(chunks, decays):
        bf_end = bf[c - 1:c, :]
        cb_end = cb[0:1, :]
        bf_mid = bf[c // 2 - 1:c // 2, :]
        cb_mid = cb[c // 2:c // 2 + 1, :]
        q = q_ref[0, rows, :].astype(F32) * (GLA_DK ** -0.5)
        k = k_ref[0, rows, :].astype(F32)
        qi_f = q * jnp.exp2(bf)
        ks_f = k * jnp.exp2(bf_end - bf)
        qi_b = q * jnp.exp2(cb)
        ks_b = k * jnp.exp2(cb_end - cb)
        qi_ref[rows, :] = jnp.concatenate([qi_f, qi_b], axis=-1).astype(BF16)
        e_ref[ci] = jnp.broadcast_to(jnp.exp2(jnp.concatenate([bf_end, cb_end], axis=-1)), (8, 2 * GLA_DK))
        operands.append(((qi_f * jnp.exp2(-bf_mid)).astype(BF16), (ks_f * jnp.exp2(bf_mid - bf_end)).astype(BF16),
                         (qi_b * jnp.exp2(-cb_mid)).astype(BF16), (ks_b * jnp.exp2(cb_mid - cb_end)).astype(BF16),
                         jnp.concatenate([ks_f, ks_b], axis=-1).astype(BF16)))
        ranges.append(jnp.maximum(-bf_end, -cb_end))
    scores = [(_dot(qt_f, kt_f, _NT), _dot(qt_b, kt_b, _NT)) for qt_f, kt_f, qt_b, kt_b, _ in operands]
    def causal_mix(sc_f, sc_b):
        n = sc_f.shape[0]
        t_idx = lax.broadcasted_iota(jnp.int32, (n, n), 0)
        s_idx = lax.broadcasted_iota(jnp.int32, (n, n), 1)
        return jnp.where(s_idx <= t_idx, sc_f, 0.0) + jnp.where(s_idx >= t_idx, sc_b, 0.0)

    for (rows, ci, c), (sc_f, sc_b) in zip(chunks, scores):
        if c % 256 == 0:
            h = c // 2
            top = jnp.concatenate([causal_mix(sc_f[:h, :h], sc_b[:h, :h]), sc_b[:h, h:]], axis=1)
            bot = jnp.concatenate([sc_f[h:, :h], causal_mix(sc_f[h:, h:], sc_b[h:, h:])], axis=1)
            sc = jnp.concatenate([top, bot], axis=0)
        else:
            sc = causal_mix(sc_f, sc_b)
        oi_ref[rows, :] = _dot(sc.astype(BF16), v_ref[0, rows, :])
    for (rows, ci, c), ops in zip(chunks, operands):
        ut_ref[ci] = _dot(v_ref[0, rows, :], ops[4], _TN)
    return ranges


def _gla_exact_intra(rows, size, q_ref, k_ref, v_ref, g_ref, oi_ref):
    c = size
    bf, cb = _gla_decays(rows, c, g_ref)
    q = q_ref[0, rows, :].astype(F32) * (GLA_DK ** -0.5)
    k = k_ref[0, rows, :].astype(F32)
    row = lax.broadcasted_iota(jnp.int32, (c, 1), 0)
    lane = lax.broadcasted_iota(jnp.int32, (c, c), 1)

    def body(t, sct):
        pick = row == t
        take = lambda a: jnp.sum(jnp.where(pick, a, 0.0), axis=0, keepdims=True)
        dec = (jnp.exp2(jnp.where(row <= t, take(bf) - bf, MASK_VALUE))
               + jnp.exp2(jnp.where(row >= t, take(cb) - cb, MASK_VALUE)))
        col = jnp.sum(k * take(q) * dec, axis=-1, keepdims=True)
        return jnp.where(lane == t, col, sct)

    sct = lax.fori_loop(0, c, body, jnp.zeros((c, c), F32))
    oi_ref[rows, :] = _dot(sct.astype(BF16), v_ref[0, rows, :], _TN)


def _gla_group_out(chunks, r_ref, nw_ref, qi_ref, oi_ref, st_ref, o_ref):
    inter = [_dot(qi_ref[rows, :], st_ref[ci], _NT) for rows, ci in chunks]
    for (rows, ci), o_inter in zip(chunks, inter):
        o = oi_ref[rows, :] + o_inter
        o = o * lax.rsqrt(jnp.mean(o * o, axis=-1, keepdims=True) + LN_EPS) * nw_ref[...]
        r = r_ref[0, rows, :].astype(F32)
        o_ref[0, rows, :] = (o * (r / (1.0 + jnp.exp(-r)))).astype(o_ref.dtype)


def _gla_kernel(q_ref, k_ref, v_ref, r_ref, lr_ref, wc_ref, bc_ref, nw_ref, o_ref,
                g_ref, qi_ref, oi_ref, ut_ref, e_ref, st_ref, *, seq_len, gate_rows):
    c = _GLA_BLOCK
    n_full = (seq_len - N_META) // c
    n_chunks = n_full + 1

    for i in range(seq_len // gate_rows):
        rows = pl.ds(i * gate_rows, gate_rows)
        x = _dot(lr_ref[0, rows, :], wc_ref[0]) + bc_ref[0]
        g_ref[rows, :] = (jnp.minimum(x, 0.0) - jnp.log(1.0 + jnp.exp(-jnp.abs(x)))) * (LOG2_E / GLA_TAU)

    def chunk_rows(ci):
        return pl.ds(pl.multiple_of(N_META + (ci - 1) * c, 16), c)

    terms = functools.partial(_gla_group_terms, q_ref=q_ref, k_ref=k_ref, v_ref=v_ref, g_ref=g_ref,
                              qi_ref=qi_ref, oi_ref=oi_ref, ut_ref=ut_ref, e_ref=e_ref)
    full = [(pl.ds(N_META + (ci - 1) * c, c), ci, c) for ci in range(1, n_chunks)]
    decay_range = terms([(pl.ds(0, N_META), 0, N_META)])
    for i in range(0, n_full, _GLA_GROUP):
        decay_range += terms(full[i:i + _GLA_GROUP])

    @pl.when(jnp.max(functools.reduce(jnp.maximum, decay_range)) > _GLA_SAFE_RANGE)
    def _():
        exact = functools.partial(_gla_exact_intra, q_ref=q_ref, k_ref=k_ref, v_ref=v_ref, g_ref=g_ref,
                                  oi_ref=oi_ref)
        exact(pl.ds(0, N_META), N_META)

        def exact_body(ci, carry):
            exact(chunk_rows(ci), c)
            return carry

        lax.fori_loop(1, n_chunks, exact_body, 0)

    def scan_body(i, carry):
        s_f, s_b = carry
        j = n_chunks - 1 - i
        st_ref[i, :, :GLA_DK] = s_f.astype(BF16)
        st_ref[j, :, GLA_DK:] = s_b.astype(BF16)
        s_f = s_f * e_ref[i, 0:1, :GLA_DK] + ut_ref[i, :, :GLA_DK]
        s_b = s_b * e_ref[j, 0:1, GLA_DK:] + ut_ref[j, :, GLA_DK:]
        return s_f, s_b

    zero = jnp.zeros((GLA_DV, GLA_DK), F32)
    lax.fori_loop(0, n_chunks, scan_body, (zero, zero))

    out = functools.partial(_gla_group_out, r_ref=r_ref, nw_ref=nw_ref, qi_ref=qi_ref, oi_ref=oi_ref,
                            st_ref=st_ref, o_ref=o_ref)
    out([(pl.ds(0, N_META), 0)])
    for i in range(0, n_full, _GLA_GROUP):
        out([(rows, ci) for rows, ci, _ in full[i:i + _GLA_GROUP]])


def _gla(proj, lr, wc, bc, nw, layer):
    bsz, seq_len, _ = proj.shape
    assert (seq_len - N_META) % _GLA_BLOCK == 0 and seq_len % _ROW_TILE == 0
    n_chunks = (seq_len - N_META) // _GLA_BLOCK + 1
    kern = functools.partial(_gla_kernel, seq_len=seq_len, gate_rows=_ROW_TILE)
    qk_blk = lambda off: pl.BlockSpec((1, seq_len, GLA_DK), lambda b, h: (b, 0, off + h))
    vr_blk = lambda off: pl.BlockSpec((1, seq_len, GLA_DV), lambda b, h: (b, 0, off + h))
    return pl.pallas_call(
        kern,
        grid=(bsz, GLA_HEADS),
        in_specs=[qk_blk(_COL_GQ // GLA_DK), qk_blk(_COL_GK // GLA_DK),
                  vr_blk(_COL_GV // GLA_DV), vr_blk(_COL_GR // GLA_DV),
                  pl.BlockSpec((1, seq_len, 128), lambda b, h: (b, 0, 0)),
                  pl.BlockSpec((None, 1, 128, 2 * GLA_DK), lambda b, h: (layer, h, 0, 0)),
                  pl.BlockSpec((None, 1, 1, 2 * GLA_DK), lambda b, h: (layer, h, 0, 0)),
                  pl.BlockSpec((None, 1, GLA_DV), lambda b, h: (layer, 0, 0))],
        out_specs=pl.BlockSpec((1, seq_len, GLA_DV), lambda b, h: (b, 0, h)),
        out_shape=jax.ShapeDtypeStruct((bsz, seq_len, GLA_HEADS * GLA_DV), BF16),
        scratch_shapes=[pltpu.VMEM((seq_len, 2 * GLA_DK), F32),
                        pltpu.VMEM((seq_len, 2 * GLA_DK), BF16),
                        pltpu.VMEM((seq_len, GLA_DV), F32),
                        pltpu.VMEM((n_chunks, GLA_DV, 2 * GLA_DK), F32),
                        pltpu.VMEM((n_chunks, 8, 2 * GLA_DK), F32),
                        pltpu.VMEM((n_chunks, GLA_DV, 2 * GLA_DK), BF16)],
        compiler_params=pltpu.CompilerParams(
            dimension_semantics=("arbitrary", "arbitrary"),
            vmem_limit_bytes=V7X_VMEM_LIMIT_BYTES),
        name="gla",
    )(proj, proj, proj, proj, lr, wc, bc, nw)


def _na_kernel(q_ref, k_ref, v_ref, t_ref, o_ref, *, seq_len):
    q_ref, k_ref, v_ref = q_ref.at[0], k_ref.at[0], v_ref.at[0]
    scaled = lambda q: (q.astype(F32) * (NA_DH ** -0.5 * LOG2_E)).astype(BF16)
    rows = (seq_len - N_META) // GRID_W
    hw = NA_HEAD_GROUP * NA_DH
    nq = NA_HEAD_GROUP * GRID_W
    n_keys = NA_WIN_H * GRID_W
    meta_blk = 128

    def block_diag(q4):
        n = q4.shape[0]
        qrep = jnp.concatenate([q4] * NA_HEAD_GROUP, axis=0)
        rb = lax.broadcasted_iota(jnp.int32, (NA_HEAD_GROUP * n, hw), 0) // n
        lb = lax.broadcasted_iota(jnp.int32, (NA_HEAD_GROUP * n, hw), 1) // NA_DH
        return jnp.where(rb == lb, qrep, jnp.zeros_like(qrep))

    def pick_diag(o, n):
        lb = lax.broadcasted_iota(jnp.int32, (n, hw), 1) // NA_DH
        out = jnp.zeros((n, hw), F32)
        for h in range(NA_HEAD_GROUP):
            out = jnp.where(lb == h, o[h * n:(h + 1) * n, :], out)
        return out

    k_meta = k_ref[0:meta_blk, :]
    v_meta = v_ref[0:meta_blk, :]

    def meta_scores(qbd):
        s = _dot(qbd, k_meta, _NT)
        lane = lax.broadcasted_iota(jnp.int32, s.shape, 1)
        return jnp.where(lane < N_META, s, MASK_VALUE)

    qm = block_diag(scaled(q_ref[0:N_META, :]))
    sm = meta_scores(qm)
    pm = jnp.exp2(sm - jnp.max(sm, axis=-1, keepdims=True))
    om = _dot(pm.astype(BF16), v_meta) / jnp.sum(pm, axis=-1, keepdims=True)
    o_ref[0, 0:N_META, :] = pick_diag(om, N_META).astype(o_ref.dtype)

    def group_body(gi, carry):
        scores = []
        for j in range(_NA_ROW_GROUP):
            r = gi * _NA_ROW_GROUP + j
            rs = jnp.clip(r - NA_WIN_H // 2, 0, rows - NA_WIN_H)
            q_start = pl.multiple_of(N_META + r * GRID_W, 16)
            k_start = pl.multiple_of(N_META + rs * GRID_W, 16)
            qbd = block_diag(scaled(q_ref[pl.ds(q_start, GRID_W), :]))
            sw = _dot(qbd, k_ref[pl.ds(k_start, n_keys), :], _NT)
            scores.append((r - rs, q_start, k_start, sw, meta_scores(qbd)))
        probs = []
        for delta, q_start, k_start, sw, smeta in scores:
            bias = jnp.concatenate(
                [t_ref[2 * j - delta + NA_WIN_H - 1] for j in range(NA_WIN_H // 2)], axis=-1)
            sw = sw + bias
            blocks = lambda a: [a[:, i:i + meta_blk] for i in range(0, n_keys, meta_blk)]
            mx = jnp.max(functools.reduce(jnp.maximum, blocks(sw), smeta), axis=-1, keepdims=True)
            pw = jnp.exp2(sw - mx)
            pmeta = jnp.exp2(smeta - mx)
            den = jnp.sum(functools.reduce(jnp.add, blocks(pw), pmeta), axis=-1, keepdims=True)
            probs.append((q_start, k_start, pw.astype(BF16), pmeta.astype(BF16), den))
        outs = [(q_start, _dot(pw, v_ref[pl.ds(k_start, n_keys), :]) + _dot(pmeta, v_meta), den)
                for q_start, k_start, pw, pmeta, den in probs]
        for q_start, o, den in outs:
            o_ref[0, pl.ds(q_start, GRID_W), :] = pick_diag(o / den, GRID_W).astype(o_ref.dtype)
        return carry

    lax.fori_loop(0, rows // _NA_ROW_GROUP, group_body, 0)


def _na_bias_table(rel_bias):
    depth = rel_bias.shape[0]
    col = np.arange(GRID_W)
    cs = np.clip(col - NA_WIN_W // 2, 0, GRID_W - NA_WIN_W)
    inside = (col[None, :] >= cs[:, None]) & (col[None, :] < cs[:, None] + NA_WIN_W)
    dc = col[None, :] - col[:, None] + NA_WIN_W - 1
    onehot = (np.arange(2 * NA_WIN_W - 1)[:, None, None] == dc[None]) & inside[None]
    t = jnp.einsum("lhdj,jqk->lhdqk", rel_bias.astype(F32), jnp.asarray(onehot, F32),
                   precision=lax.Precision.HIGHEST)
    t = jnp.where(inside, t * LOG2_E, MASK_VALUE)
    t = jnp.concatenate([t[:, :, :-1], t[:, :, 1:]], axis=-1)
    n_dr = t.shape[2]
    t = t.reshape(depth, NA_HEADS // NA_HEAD_GROUP, NA_HEAD_GROUP, n_dr, GRID_W, 2 * GRID_W)
    return t.transpose(0, 1, 3, 2, 4, 5).reshape(depth, NA_HEADS // NA_HEAD_GROUP, n_dr,
                                                  NA_HEAD_GROUP * GRID_W, 2 * GRID_W)


def _na(proj, table, layer):
    bsz, seq_len, _ = proj.shape
    hw = NA_HEAD_GROUP * NA_DH
    n_groups = NA_HEADS // NA_HEAD_GROUP
    kern = functools.partial(_na_kernel, seq_len=seq_len)
    blk = lambda off: pl.BlockSpec((1, seq_len, hw), lambda g, b: (b, 0, off + g))
    return pl.pallas_call(
        kern,
        grid=(n_groups, bsz),
        in_specs=[blk(_COL_NQ // hw), blk(_COL_NK // hw), blk(_COL_NV // hw),
                  pl.BlockSpec((None, None) + table.shape[2:], lambda g, b: (layer, g, 0, 0, 0))],
        out_specs=pl.BlockSpec((1, seq_len, hw), lambda g, b: (b, 0, g)),
        out_shape=jax.ShapeDtypeStruct((bsz, seq_len, NA_HEADS * NA_DH), BF16),
        compiler_params=pltpu.CompilerParams(
            dimension_semantics=("arbitrary", "arbitrary"),
            vmem_limit_bytes=V7X_VMEM_LIMIT_BYTES),
        name="na",
    )(proj, proj, proj, table)


def _layer_norm(z, w, b):
    mu = jnp.mean(z, axis=-1, keepdims=True)
    zc = z - mu
    var = jnp.mean(zc * zc, axis=-1, keepdims=True)
    return zc * lax.rsqrt(var + LN_EPS) * w + b


def _out_ln_kernel(yg_ref, yn_ref, w_ref, *rest, alpha, sub_rows, tiles_per_seq):
    from_tokens = len(rest) == 6
    res_ref = rest[0]
    lw_ref, lb_ref, o_ref, ob_ref = rest[-4:]
    kg = yg_ref.shape[1]
    starts = [sum(sub_rows[:i]) for i in range(len(sub_rows))]
    mixes = [_dot(yg_ref[r0:r0 + n, :], w_ref[0:kg, :]) + _dot(yn_ref[r0:r0 + n, :], w_ref[kg:, :])
             for r0, n in zip(starts, sub_rows)]
    for r0, n, mix in zip(starts, sub_rows, mixes):
        if not from_tokens:
            res = res_ref[r0:r0 + n, :]
        else:
            meta_ref = rest[1]
            if r0 == 0:
                head = jnp.concatenate([meta_ref[...], res_ref[0:n - N_META, :]], axis=0)
            else:
                head = res_ref[r0 - N_META:r0 - N_META + n, :]
            first = pl.program_id(0) % tiles_per_seq == 0
            res = jnp.where(first, head, res_ref[r0:r0 + n, :])
        o = _layer_norm(alpha * res + mix, lw_ref[...], lb_ref[...])
        o_ref[r0:r0 + n, :] = o
        ob_ref[r0:r0 + n, :] = o.astype(ob_ref.dtype)


def _out_ln(yg, yn, w, res, lw, lb, layer, alpha, tm, sub_rows, seq_len):
    m, kg = yg.shape
    kn = yn.shape[1]
    n = w.shape[1]
    assert sum(sub_rows) == tm and seq_len % tm == 0 and all(r % 16 == 0 for r in sub_rows)
    tiles_per_seq = seq_len // tm
    row = lambda width: pl.BlockSpec((tm, width), lambda i: (i, 0))
    per_layer = lambda shape: pl.BlockSpec((None,) + shape, lambda i: (layer, 0, 0))
    if isinstance(res, tuple):
        x_spec = pl.BlockSpec(
            (pl.Squeezed(), pl.Element(tm), pl.Element(n)),
            lambda i: (i // tiles_per_seq,
                       pl.multiple_of(jnp.maximum((i % tiles_per_seq) * tm - N_META, 0), 8), 0))
        res_specs = [x_spec, pl.BlockSpec((N_META, n), lambda i: (0, 0))]
    else:
        res, res_specs = (res,), [row(n)]
    return pl.pallas_call(
        functools.partial(_out_ln_kernel, alpha=alpha, sub_rows=sub_rows, tiles_per_seq=tiles_per_seq),
        grid=(m // tm,),
        in_specs=[row(kg), row(kn),
                  pl.BlockSpec((kg + kn, n), lambda i: (0, 0), pipeline_mode=pl.Buffered(1))]
        + res_specs + [per_layer((1, n)), per_layer((1, n))],
        out_specs=[row(n), row(n)],
        out_shape=[jax.ShapeDtypeStruct((m, n), F32), jax.ShapeDtypeStruct((m, n), BF16)],
        compiler_params=pltpu.CompilerParams(
            dimension_semantics=("arbitrary",),
            vmem_limit_bytes=V7X_VMEM_LIMIT_BYTES),
        name="out_ln",
    )(yg, yn, w, *res, lw, lb)


def _ffn_ln_kernel(a_ref, w1_ref, w2_ref, h_ref, lw_ref, lb_ref, o_ref, ob_ref=None, *, alpha, sub_rows):
    f = pl.program_id(1)
    last = pl.num_programs(1) - 1

    def hidden():
        return jnp.square(jnp.maximum(_dot(a_ref[...], w1_ref[...]), 0.0)).astype(BF16)

    @pl.when(f == 0)
    def _():
        o_ref[...] = alpha * h_ref[...] + _dot(hidden(), w2_ref[...])

    @pl.when(jnp.logical_and(f > 0, f < last))
    def _():
        o_ref[...] += _dot(hidden(), w2_ref[...])

    @pl.when(f == last)
    def _():
        u = hidden()
        starts = [sum(sub_rows[:i]) for i in range(len(sub_rows))]
        parts = [_dot(u[r0:r0 + n, :], w2_ref[...]) for r0, n in zip(starts, sub_rows)]
        for r0, n, part in zip(starts, sub_rows, parts):
            o = _layer_norm(o_ref[r0:r0 + n, :] + part, lw_ref[...], lb_ref[...])
            o_ref[r0:r0 + n, :] = o
            if ob_ref is not None:
                ob_ref[r0:r0 + n, :] = o.astype(ob_ref.dtype)


def _ffn_ln(a, w1, w2, h, lw, lb, layer, alpha, tm, sub_rows, tf, skip_rows):
    bsz, seq_len, d = a.shape
    d_ff = w1.shape[1]
    rows = seq_len - skip_rows
    assert rows % tm == 0 and skip_rows % 16 == 0 and tm % 16 == 0 and d_ff % tf == 0 and d_ff // tf >= 2
    assert sum(sub_rows) == tm and all(r % 16 == 0 for r in sub_rows)
    nt = rows // tm
    row_in = pl.BlockSpec((pl.Squeezed(), pl.Element(tm), pl.Element(d)),
                          lambda i, f: (i // nt, pl.multiple_of(skip_rows + (i % nt) * tm, 16), 0))
    row_out = pl.BlockSpec((None, tm, d), lambda i, f: (i // nt, i % nt, 0))
    vec = pl.BlockSpec((None, 1, d), lambda i, f: (layer, 0, 0))
    out_shape = [jax.ShapeDtypeStruct((bsz, rows, d), F32)]
    if not skip_rows:
        out_shape.append(jax.ShapeDtypeStruct((bsz, rows, d), BF16))
    return pl.pallas_call(
        functools.partial(_ffn_ln_kernel, alpha=alpha, sub_rows=sub_rows),
        grid=(bsz * nt, d_ff // tf),
        in_specs=[row_in, pl.BlockSpec((d, tf), lambda i, f: (0, f)),
                  pl.BlockSpec((tf, d), lambda i, f: (f, 0)), row_in, vec, vec],
        out_specs=[row_out] * len(out_shape),
        out_shape=out_shape,
        compiler_params=pltpu.CompilerParams(
            dimension_semantics=("arbitrary", "arbitrary"),
            vmem_limit_bytes=V7X_VMEM_LIMIT_BYTES),
        name="ffn_ln",
    )(a, w1, w2, h, lw, lb)


_GLA_QK = GLA_HEADS * GLA_DK
_GLA_VR = GLA_HEADS * GLA_DV
_NA_W = NA_HEADS * NA_DH
_COL_GQ = 0
_COL_GK = _COL_GQ + _GLA_QK
_COL_GV = _COL_GK + _GLA_QK
_COL_GR = _COL_GV + _GLA_VR
_COL_NQ = _COL_GR + _GLA_VR
_COL_NK = _COL_NQ + _NA_W
_COL_NV = _COL_NK + _NA_W
_PROJ_COLS = _COL_NV + _NA_W
_W_IN_LR = _COL_NQ
_W_IN_COLS = _PROJ_COLS + 2 * GLA_RANK
_PROJ_TN = 1024
_LR_TN = 128
_ROW_TILE = 688
_PROJ_ROW_TILE = 1376
_OUT_ROW_TILE = 688
_SUB_ROWS = (176, 176, 176, 160)
_FFN_TF = 512
_LAST_ROW_TILE = 512
_LAST_SUB_ROWS = (128, 128, 128, 128)
_LAST_FFN_TF = 1024


def _gla_gate_params(w_up, b_up):
    depth = w_up.shape[0]
    wc = jnp.zeros((depth, GLA_HEADS, 128, 2 * GLA_DK), F32)
    for z in range(2):
        w = w_up[:, z].astype(F32).reshape(depth, GLA_RANK, GLA_HEADS, GLA_DK).transpose(0, 2, 1, 3)
        wc = wc.at[:, :, z * GLA_RANK:(z + 1) * GLA_RANK, z * GLA_DK:(z + 1) * GLA_DK].set(w)
    bc = b_up.astype(F32).reshape(depth, 2, GLA_HEADS, GLA_DK).transpose(0, 2, 1, 3)
    return wc.astype(BF16), bc.reshape(depth, GLA_HEADS, 1, 2 * GLA_DK)


def kernel(x, meta, w_in, gla_w_up, gla_b_up, gla_norm_w, na_rel_bias, w_out, ln1_w, ln1_b,
           w_ff1, w_ff2, ln2_w, ln2_b):
    bsz, seq, d = x.shape
    depth = w_in.shape[0]
    alpha = (2 * depth) ** 0.25
    seq_len = seq + N_META
    m = bsz * seq_len
    meta_b = jnp.broadcast_to(meta.astype(BF16)[None], (bsz, N_META, d))
    hb = jnp.pad(x.astype(BF16), ((0, 0), (N_META, 0), (0, 0)))
    hb = lax.dynamic_update_slice(hb, meta_b, (0, 0, 0)).reshape(m, d)
    h = (x.astype(F32), meta.astype(F32))
    assert w_in.shape[2] == _W_IN_COLS and _W_IN_LR % _PROJ_TN == 0 and _PROJ_COLS % _PROJ_TN == 0
    w_in_t = jnp.swapaxes(w_in, 1, 2)
    wc, bc = _gla_gate_params(gla_w_up, gla_b_up)
    nw = gla_norm_w.astype(F32).reshape(depth, 1, GLA_DV)
    table = _na_bias_table(na_rel_bias)
    vec = lambda p: p.astype(F32).reshape(depth, 1, d)
    ln1_w, ln1_b, ln2_w, ln2_b = vec(ln1_w), vec(ln1_b), vec(ln2_w), vec(ln2_b)
    for l in range(depth):
        proj, w_out_b, w_ff1_b, w_ff2_b = _proj(
            hb, w_in_t, l, _PROJ_COLS // _PROJ_TN, _PROJ_TN, 0, _W_IN_LR // _PROJ_TN, 2 * GLA_RANK,
            _PROJ_ROW_TILE, "proj", cast=(w_out, w_ff1, w_ff2))
        proj = proj.reshape(bsz, seq_len, _PROJ_COLS)
        lr, = _proj(hb, w_in_t, l, 1, _LR_TN, _W_IN_LR, 1, 0, _ROW_TILE, "proj_lr")
        lr = lr.reshape(bsz, seq_len, _LR_TN)
        y_gla = _gla(proj, lr, wc, bc, nw, l).reshape(m, GLA_HEADS * GLA_DV)
        y_na = _na(proj, table, l).reshape(m, NA_HEADS * NA_DH)
        h, hb = _out_ln(y_gla, y_na, w_out_b, h, ln1_w, ln1_b, l, alpha, _OUT_ROW_TILE, _SUB_ROWS, seq_len)
        hb3, h3 = hb.reshape(bsz, seq_len, d), h.reshape(bsz, seq_len, d)
        if l + 1 < depth:
            h, hb = _ffn_ln(hb3, w_ff1_b, w_ff2_b, h3, ln2_w, ln2_b, l, alpha, _ROW_TILE, _SUB_ROWS, _FFN_TF, 0)
            h, hb = h.reshape(m, d), hb.reshape(m, d)
        else:
            out, = _ffn_ln(hb3, w_ff1_b, w_ff2_b, h3, ln2_w, ln2_b, l, alpha, _LAST_ROW_TILE, _LAST_SUB_ROWS,
                           _LAST_FFN_TF, N_META)
    return out.astype(x.dtype)
```

```python
import functools

import numpy as np
import jax
import jax.numpy as jnp
from jax import lax
from jax.experimental import pallas as pl
from jax.experimental.pallas import tpu as pltpu

N_META = 16
GRID_W = 64
GLA_HEADS = 4
GLA_DK = 128
GLA_DV = 256
GLA_RANK = 16
GLA_TAU = 16.0
_GLA_BLOCK = 256
_GLA_GROUP = 4
_GLA_SAFE_RANGE = 115.0
LOG2_E = 1.4426950408889634
NA_HEADS = 16
NA_DH = 64
NA_WIN_H = 8
NA_WIN_W = 16
NA_HEAD_GROUP = 4
_NA_ROW_GROUP = 4
LN_EPS = 1e-5
MASK_VALUE = -1e30

V7X_VMEM_LIMIT_BYTES = 58 * 1024 * 1024

F32 = jnp.float32
BF16 = jnp.bfloat16

_NT = (((1,), (1,)), ((), ()))
_TN = (((0,), (0,)), ((), ()))


def _dot(a, b, dims=None):
    if dims is None:
        return jnp.dot(a, b, preferred_element_type=F32)
    return lax.dot_general(a, b, dims, preferred_element_type=F32)


def _proj_kernel(*refs, n_cast):
    a_ref, w_ref = refs[:2]
    cast_src = refs[2:2 + n_cast]
    o_ref = refs[2 + n_cast]
    cast_dst = refs[3 + n_cast:3 + 2 * n_cast]
    wb_ref = refs[-1]

    @pl.when(pl.program_id(1) == 0)
    def _():
        wb_ref[...] = w_ref[...].astype(BF16)

    o_ref[...] = _dot(a_ref[...], wb_ref[...], _NT).astype(o_ref.dtype)
    for src, dst in zip(cast_src, cast_dst):
        dst[...] = src[...].astype(dst.dtype)


def _proj(a, wt, layer, n_tiles, tn, first, skip_from, skip, tm, name, cast=()):
    m, k = a.shape
    n_rows = m // tm
    assert first % 16 == 0 and tn % 16 == 0 and skip % 16 == 0

    def w_index(j, i):
        return (layer, pl.multiple_of(first + j * tn + jnp.where(j >= skip_from, skip, 0), 16), 0)

    cast_in, cast_out, cast_shapes = [], [], []
    for arr in cast:
        _, r, c = arr.shape
        n_slabs = max(s for s in range(1, n_tiles * n_rows + 1) if r % (16 * s) == 0)
        slab = lambda j, i, n_slabs=n_slabs: jnp.minimum(j * n_rows + i, n_slabs - 1)
        cast_in.append(pl.BlockSpec((None, r // n_slabs, c), lambda j, i, slab=slab: (layer, slab(j, i), 0)))
        cast_out.append(pl.BlockSpec((r // n_slabs, c), lambda j, i, slab=slab: (slab(j, i), 0)))
        cast_shapes.append(jax.ShapeDtypeStruct((r, c), BF16))

    return pl.pallas_call(
        functools.partial(_proj_kernel, n_cast=len(cast)),
        grid=(n_tiles, n_rows),
        in_specs=[pl.BlockSpec((tm, k), lambda j, i: (i, 0)),
                  pl.BlockSpec((pl.Squeezed(), pl.Element(tn), pl.Element(k)), w_index)] + cast_in,
        out_specs=[pl.BlockSpec((tm, tn), lambda j, i: (i, j))] + cast_out,
        out_shape=[jax.ShapeDtypeStruct((m, n_tiles * tn), BF16)] + cast_shapes,
        scratch_shapes=[pltpu.VMEM((tn, k), BF16)],
        compiler_params=pltpu.CompilerParams(
            dimension_semantics=("arbitrary", "arbitrary"),
            vmem_limit_bytes=V7X_VMEM_LIMIT_BYTES),
        name=name,
    )(a, wt, *cast)


def _gla_decays(rows, c, g_ref):
    g2 = g_ref[rows, :]
    t_idx = lax.broadcasted_iota(jnp.int32, (c, c), 0)
    s_idx = lax.broadcasted_iota(jnp.int32, (c, c), 1)
    tri = jnp.where(s_idx <= t_idx, 1.0, 0.0).astype(BF16)
    g_hi = g2.astype(BF16)
    g_lo = (g2 - g_hi.astype(F32)).astype(BF16)
    p = _dot(tri, g_hi) + _dot(tri, g_lo)
    cb = p[c - 1:c, GLA_DK:] - p[:, GLA_DK:] + g2[:, GLA_DK:]
    return p[:, :GLA_DK], cb


def _gla_group_terms(chunks, q_ref, k_ref, v_ref, g_ref, qi_ref, oi_ref, ut_ref, e_ref):
    decays = [_gla_decays(rows, c, g_ref) for rows, _, c in chunks]
    operands, ranges = [], []
    for (rows, ci, c), (bf, cb) in zip(chunks, decays):
        bf_end = bf[c - 1:c, :]
        cb_end = cb[0:1, :]
        bf_mid = bf[c // 2 - 1:c // 2, :]
        cb_mid = cb[c // 2:c // 2 + 1, :]
        q = q_ref[0, rows, :].astype(F32) * (GLA_DK ** -0.5)
        k = k_ref[0, rows, :].astype(F32)
        qi_f = q * jnp.exp2(bf)
        ks_f = k * jnp.exp2(bf_end - bf)
        qi_b = q * jnp.exp2(cb)
        ks_b = k * jnp.exp2(cb_end - cb)
        qi_ref[rows, :] = jnp.concatenate([qi_f, qi_b], axis=-1).astype(BF16)
        e_ref[ci] = jnp.broadcast_to(jnp.exp2(jnp.concatenate([bf_end, cb_end], axis=-1)), (8, 2 * GLA_DK))
        operands.append(((qi_f * jnp.exp2(-bf_mid)).astype(BF16), (ks_f * jnp.exp2(bf_mid - bf_end)).astype(BF16),
                         (qi_b * jnp.exp2(-cb_mid)).astype(BF16), (ks_b * jnp.exp2(cb_mid - cb_end)).astype(BF16),
                         jnp.concatenate([ks_f, ks_b], axis=-1).astype(BF16)))
        ranges.append(jnp.maximum(-bf_end, -cb_end))
    scores = [(_dot(qt_f, kt_f, _NT), _dot(qt_b, kt_b, _NT)) for qt_f, kt_f, qt_b, kt_b, _ in operands]
    def causal_mix(sc_f, sc_b):
        n = sc_f.shape[0]
        t_idx = lax.broadcasted_iota(jnp.int32, (n, n), 0)
        s_idx = lax.broadcasted_iota(jnp.int32, (n, n), 1)
        return jnp.where(s_idx <= t_idx, sc_f, 0.0) + jnp.where(s_idx >= t_idx, sc_b, 0.0)

    for (rows, ci, c), (sc_f, sc_b) in zip(chunks, scores):
        if c % 256 == 0:
            h = c // 2
            top = jnp.concatenate([causal_mix(sc_f[:h, :h], sc_b[:h, :h]), sc_b[:h, h:]], axis=1)
            bot = jnp.concatenate([sc_f[h:, :h], causal_mix(sc_f[h:, h:], sc_b[h:, h:])], axis=1)
            sc = jnp.concatenate([top, bot], axis=0)
        else:
            sc = causal_mix(sc_f, sc_b)
        oi_ref[rows, :] = _dot(sc.astype(BF16), v_ref[0, rows, :])
    for (rows, ci, c), ops in zip(chunks, operands):
        ut_ref[ci] = _dot(v_ref[0, rows, :], ops[4], _TN)
    return ranges


def _gla_exact_intra(rows, size, q_ref, k_ref, v_ref, g_ref, oi_ref):
    c = size
    bf, cb = _gla_decays(rows, c, g_ref)
    q = q_ref[0, rows, :].astype(F32) * (GLA_DK ** -0.5)
    k = k_ref[0, rows, :].astype(F32)
    row = lax.broadcasted_iota(jnp.int32, (c, 1), 0)
    lane = lax.broadcasted_iota(jnp.int32, (c, c), 1)

    def body(t, sct):
        pick = row == t
        take = lambda a: jnp.sum(jnp.where(pick, a, 0.0), axis=0, keepdims=True)
        dec = (jnp.exp2(jnp.where(row <= t, take(bf) - bf, MASK_VALUE))
               + jnp.exp2(jnp.where(row >= t, take(cb) - cb, MASK_VALUE)))
        col = jnp.sum(k * take(q) * dec, axis=-1, keepdims=True)
        return jnp.where(lane == t, col, sct)

    sct = lax.fori_loop(0, c, body, jnp.zeros((c, c), F32))
    oi_ref[rows, :] = _dot(sct.astype(BF16), v_ref[0, rows, :], _TN)


def _gla_group_out(chunks, r_ref, nw_ref, qi_ref, oi_ref, st_ref, o_ref):
    inter = [_dot(qi_ref[rows, :], st_ref[ci], _NT) for rows, ci in chunks]
    for (rows, ci), o_inter in zip(chunks, inter):
        o = oi_ref[rows, :] + o_inter
        o = o * lax.rsqrt(jnp.mean(o * o, axis=-1, keepdims=True) + LN_EPS) * nw_ref[...]
        r = r_ref[0, rows, :].astype(F32)
        o_ref[0, rows, :] = (o * (r / (1.0 + jnp.exp(-r)))).astype(o_ref.dtype)


def _gla_kernel(q_ref, k_ref, v_ref, r_ref, hb_ref, wlr_ref, wc_ref, bc_ref, nw_ref, o_ref,
                g_ref, qi_ref, oi_ref, ut_ref, e_ref, st_ref, lr_ref, *, seq_len, gate_rows):
    c = _GLA_BLOCK
    n_full = (seq_len - N_META) // c
    n_chunks = n_full + 1

    @pl.when(pl.program_id(1) == 0)
    def _():
        w_lr = wlr_ref[...].astype(BF16)
        for i in range(seq_len // gate_rows):
            rows = pl.ds(i * gate_rows, gate_rows)
            lr_ref[rows, :] = _dot(hb_ref[0, rows, :], w_lr, _NT).astype(BF16)

    for i in range(seq_len // gate_rows):
        rows = pl.ds(i * gate_rows, gate_rows)
        x = _dot(lr_ref[rows, :], wc_ref[0]) + bc_ref[0]
        g_ref[rows, :] = (jnp.minimum(x, 0.0) - jnp.log(1.0 + jnp.exp(-jnp.abs(x)))) * (LOG2_E / GLA_TAU)

    def chunk_rows(ci):
        return pl.ds(pl.multiple_of(N_META + (ci - 1) * c, 16), c)

    terms = functools.partial(_gla_group_terms, q_ref=q_ref, k_ref=k_ref, v_ref=v_ref, g_ref=g_ref,
                              qi_ref=qi_ref, oi_ref=oi_ref, ut_ref=ut_ref, e_ref=e_ref)
    full = [(pl.ds(N_META + (ci - 1) * c, c), ci, c) for ci in range(1, n_chunks)]
    decay_range = terms([(pl.ds(0, N_META), 0, N_META)])
    for i in range(0, n_full, _GLA_GROUP):
        decay_range += terms(full[i:i + _GLA_GROUP])

    @pl.when(jnp.max(functools.reduce(jnp.maximum, decay_range)) > _GLA_SAFE_RANGE)
    def _():
        exact = functools.partial(_gla_exact_intra, q_ref=q_ref, k_ref=k_ref, v_ref=v_ref, g_ref=g_ref,
                                  oi_ref=oi_ref)
        exact(pl.ds(0, N_META), N_META)

        def exact_body(ci, carry):
            exact(chunk_rows(ci), c)
            return carry

        lax.fori_loop(1, n_chunks, exact_body, 0)

    def scan_body(i, carry):
        s_f, s_b = carry
        j = n_chunks - 1 - i
        st_ref[i, :, :GLA_DK] = s_f.astype(BF16)
        st_ref[j, :, GLA_DK:] = s_b.astype(BF16)
        s_f = s_f * e_ref[i, 0:1, :GLA_DK] + ut_ref[i, :, :GLA_DK]
        s_b = s_b * e_ref[j, 0:1, GLA_DK:] + ut_ref[j, :, GLA_DK:]
        return s_f, s_b

    zero = jnp.zeros((GLA_DV, GLA_DK), F32)
    lax.fori_loop(0, n_chunks, scan_body, (zero, zero))

    out = functools.partial(_gla_group_out, r_ref=r_ref, nw_ref=nw_ref, qi_ref=qi_ref, oi_ref=oi_ref,
                            st_ref=st_ref, o_ref=o_ref)
    out([(pl.ds(0, N_META), 0)])
    for i in range(0, n_full, _GLA_GROUP):
        out([(rows, ci) for rows, ci, _ in full[i:i + _GLA_GROUP]])


def _gla(proj, hb, wt, wc, bc, nw, layer):
    bsz, seq_len, _ = proj.shape
    d = hb.shape[2]
    assert (seq_len - N_META) % _GLA_BLOCK == 0 and seq_len % _ROW_TILE == 0
    n_chunks = (seq_len - N_META) // _GLA_BLOCK + 1
    kern = functools.partial(_gla_kernel, seq_len=seq_len, gate_rows=_ROW_TILE)
    qk_blk = lambda off: pl.BlockSpec((1, seq_len, GLA_DK), lambda b, h: (b, 0, off + h))
    vr_blk = lambda off: pl.BlockSpec((1, seq_len, GLA_DV), lambda b, h: (b, 0, off + h))
    return pl.pallas_call(
        kern,
        grid=(bsz, GLA_HEADS),
        in_specs=[qk_blk(_COL_GQ // GLA_DK), qk_blk(_COL_GK // GLA_DK),
                  vr_blk(_COL_GV // GLA_DV), vr_blk(_COL_GR // GLA_DV),
                  pl.BlockSpec((1, seq_len, d), lambda b, h: (b, 0, 0)),
                  pl.BlockSpec((pl.Squeezed(), pl.Element(_LR_TN), pl.Element(d)),
                               lambda b, h: (layer, _W_IN_LR, 0)),
                  pl.BlockSpec((None, 1, 128, 2 * GLA_DK), lambda b, h: (layer, h, 0, 0)),
                  pl.BlockSpec((None, 1, 1, 2 * GLA_DK), lambda b, h: (layer, h, 0, 0)),
                  pl.BlockSpec((None, 1, GLA_DV), lambda b, h: (layer, 0, 0))],
        out_specs=pl.BlockSpec((1, seq_len, GLA_DV), lambda b, h: (b, 0, h)),
        out_shape=jax.ShapeDtypeStruct((bsz, seq_len, GLA_HEADS * GLA_DV), BF16),
        scratch_shapes=[pltpu.VMEM((seq_len, 2 * GLA_DK), F32),
                        pltpu.VMEM((seq_len, 2 * GLA_DK), BF16),
                        pltpu.VMEM((seq_len, GLA_DV), F32),
                        pltpu.VMEM((n_chunks, GLA_DV, 2 * GLA_DK), F32),
                        pltpu.VMEM((n_chunks, 8, 2 * GLA_DK), F32),
                        pltpu.VMEM((n_chunks, GLA_DV, 2 * GLA_DK), BF16),
                        pltpu.VMEM((seq_len, _LR_TN), BF16)],
        compiler_params=pltpu.CompilerParams(
            dimension_semantics=("arbitrary", "arbitrary"),
            vmem_limit_bytes=V7X_VMEM_LIMIT_BYTES),
        name="gla",
    )(proj, proj, proj, proj, hb, wt, wc, bc, nw)


def _na_kernel(q_ref, k_ref, v_ref, t_ref, o_ref, *, seq_len):
    q_ref, k_ref, v_ref = q_ref.at[0], k_ref.at[0], v_ref.at[0]
    scaled = lambda q: (q.astype(F32) * (NA_DH ** -0.5 * LOG2_E)).astype(BF16)
    rows = (seq_len - N_META) // GRID_W
    hw = NA_HEAD_GROUP * NA_DH
    nq = NA_HEAD_GROUP * GRID_W
    n_keys = NA_WIN_H * GRID_W
    meta_blk = 128

    def block_diag(q4):
        n = q4.shape[0]
        qrep = jnp.concatenate([q4] * NA_HEAD_GROUP, axis=0)
        rb = lax.broadcasted_iota(jnp.int32, (NA_HEAD_GROUP * n, hw), 0) // n
        lb = lax.broadcasted_iota(jnp.int32, (NA_HEAD_GROUP * n, hw), 1) // NA_DH
        return jnp.where(rb == lb, qrep, jnp.zeros_like(qrep))

    def pick_diag(o, n):
        lb = lax.broadcasted_iota(jnp.int32, (n, hw), 1) // NA_DH
        out = jnp.zeros((n, hw), F32)
        for h in range(NA_HEAD_GROUP):
            out = jnp.where(lb == h, o[h * n:(h + 1) * n, :], out)
        return out

    k_meta = k_ref[0:meta_blk, :]
    v_meta = v_ref[0:meta_blk, :]

    def meta_scores(qbd):
        s = _dot(qbd, k_meta, _NT)
        lane = lax.broadcasted_iota(jnp.int32, s.shape, 1)
        return jnp.where(lane < N_META, s, MASK_VALUE)

    qm = block_diag(scaled(q_ref[0:N_META, :]))
    sm = meta_scores(qm)
    pm = jnp.exp2(sm - jnp.max(sm, axis=-1, keepdims=True))
    om = _dot(pm.astype(BF16), v_meta) / jnp.sum(pm, axis=-1, keepdims=True)
    o_ref[0, 0:N_META, :] = pick_diag(om, N_META).astype(o_ref.dtype)

    def group_body(gi, carry):
        scores = []
        for j in range(_NA_ROW_GROUP):
            r = gi * _NA_ROW_GROUP + j
            rs = jnp.clip(r - NA_WIN_H // 2, 0, rows - NA_WIN_H)
            q_start = pl.multiple_of(N_META + r * GRID_W, 16)
            k_start = pl.multiple_of(N_META + rs * GRID_W, 16)
            qbd = block_diag(scaled(q_ref[pl.ds(q_start, GRID_W), :]))
            sw = _dot(qbd, k_ref[pl.ds(k_start, n_keys), :], _NT)
            scores.append((r - rs, q_start, k_start, sw, meta_scores(qbd)))
        probs = []
        for delta, q_start, k_start, sw, smeta in scores:
            bias = jnp.concatenate(
                [t_ref[2 * j - delta + NA_WIN_H - 1] for j in range(NA_WIN_H // 2)], axis=-1)
            sw = sw + bias
            blocks = lambda a: [a[:, i:i + meta_blk] for i in range(0, n_keys, meta_blk)]
            mx = jnp.max(functools.reduce(jnp.maximum, blocks(sw), smeta), axis=-1, keepdims=True)
            pw = jnp.exp2(sw - mx)
            pmeta = jnp.exp2(smeta - mx)
            den = jnp.sum(functools.reduce(jnp.add, blocks(pw), pmeta), axis=-1, keepdims=True)
            probs.append((q_start, k_start, pw.astype(BF16), pmeta.astype(BF16), den))
        outs = [(q_start, _dot(pw, v_ref[pl.ds(k_start, n_keys), :]) + _dot(pmeta, v_meta), den)
                for q_start, k_start, pw, pmeta, den in probs]
        for q_start, o, den in outs:
            o_ref[0, pl.ds(q_start, GRID_W), :] = pick_diag(o / den, GRID_W).astype(o_ref.dtype)
        return carry

    lax.fori_loop(0, rows // _NA_ROW_GROUP, group_body, 0)


def _na_bias_table(rel_bias):
    depth = rel_bias.shape[0]
    col = np.arange(GRID_W)
    cs = np.clip(col - NA_WIN_W // 2, 0, GRID_W - NA_WIN_W)
    inside = (col[None, :] >= cs[:, None]) & (col[None, :] < cs[:, None] + NA_WIN_W)
    dc = col[None, :] - col[:, None] + NA_WIN_W - 1
    onehot = (np.arange(2 * NA_WIN_W - 1)[:, None, None] == dc[None]) & inside[None]
    t = jnp.einsum("lhdj,jqk->lhdqk", rel_bias.astype(F32), jnp.asarray(onehot, F32),
                   precision=lax.Precision.HIGHEST)
    t = jnp.where(inside, t * LOG2_E, MASK_VALUE)
    t = jnp.concatenate([t[:, :, :-1], t[:, :, 1:]], axis=-1)
    n_dr = t.shape[2]
    t = t.reshape(depth, NA_HEADS // NA_HEAD_GROUP, NA_HEAD_GROUP, n_dr, GRID_W, 2 * GRID_W)
    return t.transpose(0, 1, 3, 2, 4, 5).reshape(depth, NA_HEADS // NA_HEAD_GROUP, n_dr,
                                                  NA_HEAD_GROUP * GRID_W, 2 * GRID_W)


def _na(proj, table, layer):
    bsz, seq_len, _ = proj.shape
    hw = NA_HEAD_GROUP * NA_DH
    n_groups = NA_HEADS // NA_HEAD_GROUP
    kern = functools.partial(_na_kernel, seq_len=seq_len)
    blk = lambda off: pl.BlockSpec((1, seq_len, hw), lambda g, b: (b, 0, off + g))
    return pl.pallas_call(
        kern,
        grid=(n_groups, bsz),
        in_specs=[blk(_COL_NQ // hw), blk(_COL_NK // hw), blk(_COL_NV // hw),
                  pl.BlockSpec((None, None) + table.shape[2:], lambda g, b: (layer, g, 0, 0, 0))],
        out_specs=pl.BlockSpec((1, seq_len, hw), lambda g, b: (b, 0, g)),
        out_shape=jax.ShapeDtypeStruct((bsz, seq_len, NA_HEADS * NA_DH), BF16),
        compiler_params=pltpu.CompilerParams(
            dimension_semantics=("arbitrary", "arbitrary"),
            vmem_limit_bytes=V7X_VMEM_LIMIT_BYTES),
        name="na",
    )(proj, proj, proj, table)


def _layer_norm(z, w, b):
    mu = jnp.mean(z, axis=-1, keepdims=True)
    zc = z - mu
    var = jnp.mean(zc * zc, axis=-1, keepdims=True)
    return zc * lax.rsqrt(var + LN_EPS) * w + b


def _token_rows_spec(tm, n, tiles_per_seq):
    return pl.BlockSpec(
        (pl.Squeezed(), pl.Element(tm), pl.Element(n)),
        lambda i: (i // tiles_per_seq, pl.multiple_of(jnp.maximum((i % tiles_per_seq) * tm - N_META, 0), 8), 0))


def _embed_kernel(x_ref, meta_ref, o_ref, *, tiles_per_seq):
    tm = o_ref.shape[0]
    head = jnp.concatenate([meta_ref[...], x_ref[0:tm - N_META, :]], axis=0)
    first = pl.program_id(0) % tiles_per_seq == 0
    o_ref[...] = jnp.where(first, head, x_ref[...]).astype(o_ref.dtype)


def _embed(x, meta, tm):
    bsz, seq, n = x.shape
    seq_len = seq + N_META
    assert seq_len % tm == 0 and tm % 16 == 0
    tiles_per_seq = seq_len // tm
    return pl.pallas_call(
        functools.partial(_embed_kernel, tiles_per_seq=tiles_per_seq),
        grid=(bsz * tiles_per_seq,),
        in_specs=[_token_rows_spec(tm, n, tiles_per_seq), pl.BlockSpec((N_META, n), lambda i: (0, 0))],
        out_specs=pl.BlockSpec((tm, n), lambda i: (i, 0)),
        out_shape=jax.ShapeDtypeStruct((bsz * seq_len, n), BF16),
        compiler_params=pltpu.CompilerParams(
            dimension_semantics=("arbitrary",),
            vmem_limit_bytes=V7X_VMEM_LIMIT_BYTES),
        name="embed",
    )(x, meta)


def _out_ln_kernel(yg_ref, yn_ref, w_ref, *rest, alpha, sub_rows, tiles_per_seq):
    from_tokens = len(rest) == 6
    res_ref = rest[0]
    lw_ref, lb_ref, o_ref, ob_ref = rest[-4:]
    kg = yg_ref.shape[1]
    starts = [sum(sub_rows[:i]) for i in range(len(sub_rows))]
    mixes = [_dot(yg_ref[r0:r0 + n, :], w_ref[0:kg, :]) + _dot(yn_ref[r0:r0 + n, :], w_ref[kg:, :])
             for r0, n in zip(starts, sub_rows)]
    for r0, n, mix in zip(starts, sub_rows, mixes):
        if not from_tokens:
            res = res_ref[r0:r0 + n, :]
        else:
            meta_ref = rest[1]
            if r0 == 0:
                head = jnp.concatenate([meta_ref[...], res_ref[0:n - N_META, :]], axis=0)
            else:
                head = res_ref[r0 - N_META:r0 - N_META + n, :]
            first = pl.program_id(0) % tiles_per_seq == 0
            res = jnp.where(first, head, res_ref[r0:r0 + n, :])
        o = _layer_norm(alpha * res + mix, lw_ref[...], lb_ref[...])
        o_ref[r0:r0 + n, :] = o
        ob_ref[r0:r0 + n, :] = o.astype(ob_ref.dtype)


def _out_ln(yg, yn, w, res, lw, lb, layer, alpha, tm, sub_rows, seq_len):
    m, kg = yg.shape
    kn = yn.shape[1]
    n = w.shape[1]
    assert sum(sub_rows) == tm and seq_len % tm == 0 and all(r % 16 == 0 for r in sub_rows)
    tiles_per_seq = seq_len // tm
    row = lambda width: pl.BlockSpec((tm, width), lambda i: (i, 0))
    per_layer = lambda shape: pl.BlockSpec((None,) + shape, lambda i: (layer, 0, 0))
    if isinstance(res, tuple):
        res_specs = [_token_rows_spec(tm, n, tiles_per_seq), pl.BlockSpec((N_META, n), lambda i: (0, 0))]
    else:
        res, res_specs = (res,), [row(n)]
    return pl.pallas_call(
        functools.partial(_out_ln_kernel, alpha=alpha, sub_rows=sub_rows, tiles_per_seq=tiles_per_seq),
        grid=(m // tm,),
        in_specs=[row(kg), row(kn),
                  pl.BlockSpec((kg + kn, n), lambda i: (0, 0), pipeline_mode=pl.Buffered(1))]
        + res_specs + [per_layer((1, n)), per_layer((1, n))],
        out_specs=[row(n), row(n)],
        out_shape=[jax.ShapeDtypeStruct((m, n), F32), jax.ShapeDtypeStruct((m, n), BF16)],
        compiler_params=pltpu.CompilerParams(
            dimension_semantics=("arbitrary",),
            vmem_limit_bytes=V7X_VMEM_LIMIT_BYTES),
        name="out_ln",
    )(yg, yn, w, *res, lw, lb)


def _ffn_ln_kernel(a_ref, w1_ref, w2_ref, h_ref, lw_ref, lb_ref, o_ref, ob_ref=None, *, alpha, sub_rows):
    f = pl.program_id(1)
    last = pl.num_programs(1) - 1

    def hidden():
        return jnp.square(jnp.maximum(_dot(a_ref[...], w1_ref[...]), 0.0)).astype(BF16)

    @pl.when(f == 0)
    def _():
        o_ref[...] = alpha * h_ref[...] + _dot(hidden(), w2_ref[...])

    @pl.when(jnp.logical_and(f > 0, f < last))
    def _():
        o_ref[...] += _dot(hidden(), w2_ref[...])

    @pl.when(f == last)
    def _():
        u = hidden()
        starts = [sum(sub_rows[:i]) for i in range(len(sub_rows))]
        parts = [_dot(u[r0:r0 + n, :], w2_ref[...]) for r0, n in zip(starts, sub_rows)]
        for r0, n, part in zip(starts, sub_rows, parts):
            o = _layer_norm(o_ref[r0:r0 + n, :] + part, lw_ref[...], lb_ref[...])
            o_ref[r0:r0 + n, :] = o
            if ob_ref is not None:
                ob_ref[r0:r0 + n, :] = o.astype(ob_ref.dtype)


def _ffn_ln(a, w1, w2, h, lw, lb, layer, alpha, tm, sub_rows, tf, skip_rows):
    bsz, seq_len, d = a.shape
    d_ff = w1.shape[1]
    rows = seq_len - skip_rows
    assert rows % tm == 0 and skip_rows % 16 == 0 and tm % 16 == 0 and d_ff % tf == 0 and d_ff // tf >= 2
    assert sum(sub_rows) == tm and all(r % 16 == 0 for r in sub_rows)
    nt = rows // tm
    row_in = pl.BlockSpec((pl.Squeezed(), pl.Element(tm), pl.Element(d)),
                          lambda i, f: (i // nt, pl.multiple_of(skip_rows + (i % nt) * tm, 16), 0))
    row_out = pl.BlockSpec((None, tm, d), lambda i, f: (i // nt, i % nt, 0))
    vec = pl.BlockSpec((None, 1, d), lambda i, f: (layer, 0, 0))
    out_shape = [jax.ShapeDtypeStruct((bsz, rows, d), F32)]
    if not skip_rows:
        out_shape.append(jax.ShapeDtypeStruct((bsz, rows, d), BF16))
    return pl.pallas_call(
        functools.partial(_ffn_ln_kernel, alpha=alpha, sub_rows=sub_rows),
        grid=(bsz * nt, d_ff // tf),
        in_specs=[row_in, pl.BlockSpec((d, tf), lambda i, f: (0, f)),
                  pl.BlockSpec((tf, d), lambda i, f: (f, 0)), row_in, vec, vec],
        out_specs=[row_out] * len(out_shape),
        out_shape=out_shape,
        compiler_params=pltpu.CompilerParams(
            dimension_semantics=("arbitrary", "arbitrary"),
            vmem_limit_bytes=V7X_VMEM_LIMIT_BYTES),
        name="ffn_ln",
    )(a, w1, w2, h, lw, lb)


_GLA_QK = GLA_HEADS * GLA_DK
_GLA_VR = GLA_HEADS * GLA_DV
_NA_W = NA_HEADS * NA_DH
_COL_GQ = 0
_COL_GK = _COL_GQ + _GLA_QK
_COL_GV = _COL_GK + _GLA_QK
_COL_GR = _COL_GV + _GLA_VR
_COL_NQ = _COL_GR + _GLA_VR
_COL_NK = _COL_NQ + _NA_W
_COL_NV = _COL_NK + _NA_W
_PROJ_COLS = _COL_NV + _NA_W
_W_IN_LR = _COL_NQ
_W_IN_COLS = _PROJ_COLS + 2 * GLA_RANK
_PROJ_TN = 1024
_LR_TN = 128
_ROW_TILE = 688
_PROJ_ROW_TILE = 1376
_OUT_ROW_TILE = 688
_SUB_ROWS = (176, 176, 176, 160)
_FFN_TF = 512
_LAST_ROW_TILE = 512
_LAST_SUB_ROWS = (128, 128, 128, 128)
_LAST_FFN_TF = 1024


def _gla_gate_params(w_up, b_up):
    depth = w_up.shape[0]
    wc = jnp.zeros((depth, GLA_HEADS, 128, 2 * GLA_DK), F32)
    for z in range(2):
        w = w_up[:, z].astype(F32).reshape(depth, GLA_RANK, GLA_HEADS, GLA_DK).transpose(0, 2, 1, 3)
        wc = wc.at[:, :, z * GLA_RANK:(z + 1) * GLA_RANK, z * GLA_DK:(z + 1) * GLA_DK].set(w)
    bc = b_up.astype(F32).reshape(depth, 2, GLA_HEADS, GLA_DK).transpose(0, 2, 1, 3)
    return wc.astype(BF16), bc.reshape(depth, GLA_HEADS, 1, 2 * GLA_DK)


def kernel(x, meta, w_in, gla_w_up, gla_b_up, gla_norm_w, na_rel_bias, w_out, ln1_w, ln1_b,
           w_ff1, w_ff2, ln2_w, ln2_b):
    bsz, seq, d = x.shape
    depth = w_in.shape[0]
    alpha = (2 * depth) ** 0.25
    seq_len = seq + N_META
    m = bsz * seq_len
    h = (x.astype(F32), meta.astype(F32))
    hb = _embed(*h, _ROW_TILE)
    assert w_in.shape[2] == _W_IN_COLS and _W_IN_LR % _PROJ_TN == 0 and _PROJ_COLS % _PROJ_TN == 0
    w_in_t = jnp.swapaxes(w_in, 1, 2)
    wc, bc = _gla_gate_params(gla_w_up, gla_b_up)
    nw = gla_norm_w.astype(F32).reshape(depth, 1, GLA_DV)
    table = _na_bias_table(na_rel_bias)
    vec = lambda p: p.astype(F32).reshape(depth, 1, d)
    ln1_w, ln1_b, ln2_w, ln2_b = vec(ln1_w), vec(ln1_b), vec(ln2_w), vec(ln2_b)
    for l in range(depth):
        proj, w_out_b, w_ff1_b, w_ff2_b = _proj(
            hb, w_in_t, l, _PROJ_COLS // _PROJ_TN, _PROJ_TN, 0, _W_IN_LR // _PROJ_TN, 2 * GLA_RANK,
            _PROJ_ROW_TILE, "proj", cast=(w_out, w_ff1, w_ff2))
        proj = proj.reshape(bsz, seq_len, _PROJ_COLS)
        y_gla = _gla(proj, hb.reshape(bsz, seq_len, d), w_in_t, wc, bc, nw, l).reshape(m, GLA_HEADS * GLA_DV)
        y_na = _na(proj, table, l).reshape(m, NA_HEADS * NA_DH)
        h, hb = _out_ln(y_gla, y_na, w_out_b, h, ln1_w, ln1_b, l, alpha, _OUT_ROW_TILE, _SUB_ROWS, seq_len)
        hb3, h3 = hb.reshape(bsz, seq_len, d), h.reshape(bsz, seq_len, d)
        if l + 1 < depth:
            h, hb = _ffn_ln(hb3, w_ff1_b, w_ff2_b, h3, ln2_w, ln2_b, l, alpha, _ROW_TILE, _SUB_ROWS, _FFN_TF, 0)
            h, hb = h.reshape(m, d), hb.reshape(m, d)
        else:
            out, = _ffn_ln(hb3, w_ff1_b, w_ff2_b, h3, ln2_w, ln2_b, l, alpha, _LAST_ROW_TILE, _LAST_SUB_ROWS,
                           _LAST_FFN_TF, N_META)
    return out.astype(x.dtype)
```

```python
import functools

import numpy as np
import jax
import jax.numpy as jnp
from jax import lax
from jax.experimental import pallas as pl
from jax.experimental.pallas import tpu as pltpu

N_META = 16
GRID_W = 64
GLA_HEADS = 4
GLA_DK = 128
GLA_DV = 256
GLA_RANK = 16
GLA_TAU = 16.0
_GLA_BLOCK = 256
_GLA_GROUP = 4
_GLA_SAFE_RANGE = 115.0
LOG2_E = 1.4426950408889634
NA_HEADS = 16
NA_DH = 64
NA_WIN_H = 8
NA_WIN_W = 16
NA_HEAD_GROUP = 4
_NA_ROW_GROUP = 4
LN_EPS = 1e-5
MASK_VALUE = -1e30

V7X_VMEM_LIMIT_BYTES = 58 * 1024 * 1024

F32 = jnp.float32
BF16 = jnp.bfloat16

_NT = (((1,), (1,)), ((), ()))
_TN = (((0,), (0,)), ((), ()))


def _dot(a, b, dims=None):
    if dims is None:
        return jnp.dot(a, b, preferred_element_type=F32)
    return lax.dot_general(a, b, dims, preferred_element_type=F32)


def _proj_kernel(*refs, n_cast):
    a_ref, w_ref = refs[:2]
    cast_src = refs[2:2 + n_cast]
    o_ref = refs[2 + n_cast]
    cast_dst = refs[3 + n_cast:3 + 2 * n_cast]
    wb_ref = refs[-1]

    @pl.when(pl.program_id(1) == 0)
    def _():
        wb_ref[...] = w_ref[...].astype(BF16)

    o_ref[...] = _dot(a_ref[...], wb_ref[...], _NT).astype(o_ref.dtype)
    for src, dst in zip(cast_src, cast_dst):
        dst[...] = src[...].astype(dst.dtype)


def _proj(a, wt, layer, n_tiles, tn, first, skip_from, skip, tm, name, cast=()):
    m, k = a.shape
    n_rows = m // tm
    assert first % 16 == 0 and tn % 16 == 0 and skip % 16 == 0

    def w_index(j, i):
        return (layer, pl.multiple_of(first + j * tn + jnp.where(j >= skip_from, skip, 0), 16), 0)

    cast_in, cast_out, cast_shapes = [], [], []
    for arr in cast:
        _, r, c = arr.shape
        n_slabs = max(s for s in range(1, n_tiles * n_rows + 1) if r % (16 * s) == 0)
        slab = lambda j, i, n_slabs=n_slabs: jnp.minimum(j * n_rows + i, n_slabs - 1)
        cast_in.append(pl.BlockSpec((None, r // n_slabs, c), lambda j, i, slab=slab: (layer, slab(j, i), 0)))
        cast_out.append(pl.BlockSpec((r // n_slabs, c), lambda j, i, slab=slab: (slab(j, i), 0)))
        cast_shapes.append(jax.ShapeDtypeStruct((r, c), BF16))

    return pl.pallas_call(
        functools.partial(_proj_kernel, n_cast=len(cast)),
        grid=(n_tiles, n_rows),
        in_specs=[pl.BlockSpec((tm, k), lambda j, i: (i, 0)),
                  pl.BlockSpec((pl.Squeezed(), pl.Element(tn), pl.Element(k)), w_index)] + cast_in,
        out_specs=[pl.BlockSpec((tm, tn), lambda j, i: (i, j))] + cast_out,
        out_shape=[jax.ShapeDtypeStruct((m, n_tiles * tn), BF16)] + cast_shapes,
        scratch_shapes=[pltpu.VMEM((tn, k), BF16)],
        compiler_params=pltpu.CompilerParams(
            dimension_semantics=("arbitrary", "arbitrary"),
            vmem_limit_bytes=V7X_VMEM_LIMIT_BYTES),
        name=name,
    )(a, wt, *cast)


def _gla_decays(rows, c, g_ref):
    g2 = g_ref[rows, :]
    t_idx = lax.broadcasted_iota(jnp.int32, (c, c), 0)
    s_idx = lax.broadcasted_iota(jnp.int32, (c, c), 1)
    tri = jnp.where(s_idx <= t_idx, 1.0, 0.0).astype(BF16)
    g_hi = g2.astype(BF16)
    g_lo = (g2 - g_hi.astype(F32)).astype(BF16)
    p = _dot(tri, g_hi) + _dot(tri, g_lo)
    cb = p[c - 1:c, GLA_DK:] - p[:, GLA_DK:] + g2[:, GLA_DK:]
    return p[:, :GLA_DK], cb


def _gla_group_terms(chunks, q_ref, k_ref, v_ref, g_ref, qi_ref, oi_ref, ut_ref, e_ref):
    decays = [_gla_decays(rows, c, g_ref) for rows, _, c in chunks]
    operands, ranges = [], []
    for (rows, ci, c), (bf, cb) in zip(chunks, decays):
        bf_end = bf[c - 1:c, :]
        cb_end = cb[0:1, :]
        bf_mid = bf[c // 2 - 1:c // 2, :]
        cb_mid = cb[c // 2:c // 2 + 1, :]
        q = q_ref[0, rows, :].astype(F32) * (GLA_DK ** -0.5)
        k = k_ref[0, rows, :].astype(F32)
        qi_f = q * jnp.exp2(bf)
        ks_f = k * jnp.exp2(bf_end - bf)
        qi_b = q * jnp.exp2(cb)
        ks_b = k * jnp.exp2(cb_end - cb)
        qi_ref[rows, :] = jnp.concatenate([qi_f, qi_b], axis=-1).astype(BF16)
        e_ref[ci] = jnp.broadcast_to(jnp.exp2(jnp.concatenate([bf_end, cb_end], axis=-1)), (8, 2 * GLA_DK))
        operands.append(((qi_f * jnp.exp2(-bf_mid)).astype(BF16), (ks_f * jnp.exp2(bf_mid - bf_end)).astype(BF16),
                         (qi_b * jnp.exp2(-cb_mid)).astype(BF16), (ks_b * jnp.exp2(cb_mid - cb_end)).astype(BF16),
                         jnp.concatenate([ks_f, ks_b], axis=-1).astype(BF16)))
        ranges.append(jnp.maximum(-bf_end, -cb_end))
    scores = [(_dot(qt_f, kt_f, _NT), _dot(qt_b, kt_b, _NT)) for qt_f, kt_f, qt_b, kt_b, _ in operands]
    def causal_mix(sc_f, sc_b):
        n = sc_f.shape[0]
        t_idx = lax.broadcasted_iota(jnp.int32, (n, n), 0)
        s_idx = lax.broadcasted_iota(jnp.int32, (n, n), 1)
        return jnp.where(s_idx <= t_idx, sc_f, 0.0) + jnp.where(s_idx >= t_idx, sc_b, 0.0)

    for (rows, ci, c), (sc_f, sc_b) in zip(chunks, scores):
        if c % 256 == 0:
            h = c // 2
            top = jnp.concatenate([causal_mix(sc_f[:h, :h], sc_b[:h, :h]), sc_b[:h, h:]], axis=1)
            bot = jnp.concatenate([sc_f[h:, :h], causal_mix(sc_f[h:, h:], sc_b[h:, h:])], axis=1)
            sc = jnp.concatenate([top, bot], axis=0)
        else:
            sc = causal_mix(sc_f, sc_b)
        oi_ref[rows, :] = _dot(sc.astype(BF16), v_ref[0, rows, :])
    for (rows, ci, c), ops in zip(chunks, operands):
        ut_ref[ci] = _dot(v_ref[0, rows, :], ops[4], _TN)
    return ranges


def _gla_exact_intra(rows, size, q_ref, k_ref, v_ref, g_ref, oi_ref):
    c = size
    bf, cb = _gla_decays(rows, c, g_ref)
    q = q_ref[0, rows, :].astype(F32) * (GLA_DK ** -0.5)
    k = k_ref[0, rows, :].astype(F32)
    row = lax.broadcasted_iota(jnp.int32, (c, 1), 0)
    lane = lax.broadcasted_iota(jnp.int32, (c, c), 1)

    def body(t, sct):
        pick = row == t
        take = lambda a: jnp.sum(jnp.where(pick, a, 0.0), axis=0, keepdims=True)
        dec = (jnp.exp2(jnp.where(row <= t, take(bf) - bf, MASK_VALUE))
               + jnp.exp2(jnp.where(row >= t, take(cb) - cb, MASK_VALUE)))
        col = jnp.sum(k * take(q) * dec, axis=-1, keepdims=True)
        return jnp.where(lane == t, col, sct)

    sct = lax.fori_loop(0, c, body, jnp.zeros((c, c), F32))
    oi_ref[rows, :] = _dot(sct.astype(BF16), v_ref[0, rows, :], _TN)


def _gla_group_out(chunks, r_ref, nw_ref, qi_ref, oi_ref, st_ref, o_ref):
    inter = [_dot(qi_ref[rows, :], st_ref[ci], _NT) for rows, ci in chunks]
    for (rows, ci), o_inter in zip(chunks, inter):
        o = oi_ref[rows, :] + o_inter
        o = o * lax.rsqrt(jnp.mean(o * o, axis=-1, keepdims=True) + LN_EPS) * nw_ref[...]
        r = r_ref[0, rows, :].astype(F32)
        o_ref[0, rows, :] = (o * (r / (1.0 + jnp.exp(-r)))).astype(o_ref.dtype)


def _gla_kernel(q_ref, k_ref, v_ref, r_ref, hb_ref, wlr_ref, wc_ref, bc_ref, nw_ref, o_ref,
                g_ref, qi_ref, oi_ref, ut_ref, e_ref, st_ref, lr_ref, *, seq_len, gate_rows):
    c = _GLA_BLOCK
    n_full = (seq_len - N_META) // c
    n_chunks = n_full + 1

    @pl.when(pl.program_id(1) == 0)
    def _():
        w_lr = wlr_ref[...].astype(BF16)
        for i in range(seq_len // gate_rows):
            rows = pl.ds(i * gate_rows, gate_rows)
            lr_ref[rows, :] = _dot(hb_ref[0, rows, :], w_lr, _NT).astype(BF16)

    for i in range(seq_len // gate_rows):
        rows = pl.ds(i * gate_rows, gate_rows)
        x = _dot(lr_ref[rows, :], wc_ref[0]) + bc_ref[0]
        g_ref[rows, :] = (jnp.minimum(x, 0.0) - jnp.log(1.0 + jnp.exp(-jnp.abs(x)))) * (LOG2_E / GLA_TAU)

    def chunk_rows(ci):
        return pl.ds(pl.multiple_of(N_META + (ci - 1) * c, 16), c)

    terms = functools.partial(_gla_group_terms, q_ref=q_ref, k_ref=k_ref, v_ref=v_ref, g_ref=g_ref,
                              qi_ref=qi_ref, oi_ref=oi_ref, ut_ref=ut_ref, e_ref=e_ref)
    full = [(pl.ds(N_META + (ci - 1) * c, c), ci, c) for ci in range(1, n_chunks)]
    decay_range = terms([(pl.ds(0, N_META), 0, N_META)])
    for i in range(0, n_full, _GLA_GROUP):
        decay_range += terms(full[i:i + _GLA_GROUP])

    @pl.when(jnp.max(functools.reduce(jnp.maximum, decay_range)) > _GLA_SAFE_RANGE)
    def _():
        exact = functools.partial(_gla_exact_intra, q_ref=q_ref, k_ref=k_ref, v_ref=v_ref, g_ref=g_ref,
                                  oi_ref=oi_ref)
        exact(pl.ds(0, N_META), N_META)

        def exact_body(ci, carry):
            exact(chunk_rows(ci), c)
            return carry

        lax.fori_loop(1, n_chunks, exact_body, 0)

    def scan_body(i, carry):
        s_f, s_b = carry
        j = n_chunks - 1 - i
        st_ref[i, :, :GLA_DK] = s_f.astype(BF16)
        st_ref[j, :, GLA_DK:] = s_b.astype(BF16)
        s_f = s_f * e_ref[i, 0:1, :GLA_DK] + ut_ref[i, :, :GLA_DK]
        s_b = s_b * e_ref[j, 0:1, GLA_DK:] + ut_ref[j, :, GLA_DK:]
        return s_f, s_b

    zero = jnp.zeros((GLA_DV, GLA_DK), F32)
    lax.fori_loop(0, n_chunks, scan_body, (zero, zero))

    out = functools.partial(_gla_group_out, r_ref=r_ref, nw_ref=nw_ref, qi_ref=qi_ref, oi_ref=oi_ref,
                            st_ref=st_ref, o_ref=o_ref)
    out([(pl.ds(0, N_META), 0)])
    for i in range(0, n_full, _GLA_GROUP):
        out([(rows, ci) for rows, ci, _ in full[i:i + _GLA_GROUP]])


def _gla(proj, hb, wt, wc, bc, nw, layer):
    bsz, seq_len, _ = proj.shape
    d = hb.shape[2]
    assert (seq_len - N_META) % _GLA_BLOCK == 0 and seq_len % _ROW_TILE == 0
    n_chunks = (seq_len - N_META) // _GLA_BLOCK + 1
    kern = functools.partial(_gla_kernel, seq_len=seq_len, gate_rows=_ROW_TILE)
    qk_blk = lambda off: pl.BlockSpec((1, seq_len, GLA_DK), lambda b, h: (b, 0, off + h))
    vr_blk = lambda off: pl.BlockSpec((1, seq_len, GLA_DV), lambda b, h: (b, 0, off + h))
    return pl.pallas_call(
        kern,
        grid=(bsz, GLA_HEADS),
        in_specs=[qk_blk(_COL_GQ // GLA_DK), qk_blk(_COL_GK // GLA_DK),
                  vr_blk(_COL_GV // GLA_DV), vr_blk(_COL_GR // GLA_DV),
                  pl.BlockSpec((1, seq_len, d), lambda b, h: (b, 0, 0)),
                  pl.BlockSpec((pl.Squeezed(), pl.Element(_LR_TN), pl.Element(d)),
                               lambda b, h: (layer, _W_IN_LR, 0)),
                  pl.BlockSpec((None, 1, 128, 2 * GLA_DK), lambda b, h: (layer, h, 0, 0)),
                  pl.BlockSpec((None, 1, 1, 2 * GLA_DK), lambda b, h: (layer, h, 0, 0)),
                  pl.BlockSpec((None, 1, GLA_DV), lambda b, h: (layer, 0, 0))],
        out_specs=pl.BlockSpec((1, seq_len, GLA_DV), lambda b, h: (b, 0, h)),
        out_shape=jax.ShapeDtypeStruct((bsz, seq_len, GLA_HEADS * GLA_DV), BF16),
        scratch_shapes=[pltpu.VMEM((seq_len, 2 * GLA_DK), F32),
                        pltpu.VMEM((seq_len, 2 * GLA_DK), BF16),
                        pltpu.VMEM((seq_len, GLA_DV), F32),
                        pltpu.VMEM((n_chunks, GLA_DV, 2 * GLA_DK), F32),
                        pltpu.VMEM((n_chunks, 8, 2 * GLA_DK), F32),
                        pltpu.VMEM((n_chunks, GLA_DV, 2 * GLA_DK), BF16),
                        pltpu.VMEM((seq_len, _LR_TN), BF16)],
        compiler_params=pltpu.CompilerParams(
            dimension_semantics=("arbitrary", "arbitrary"),
            vmem_limit_bytes=V7X_VMEM_LIMIT_BYTES),
        name="gla",
    )(proj, proj, proj, proj, hb, wt, wc, bc, nw)


def _na_kernel(q_ref, k_ref, v_ref, t_ref, o_ref, *, seq_len):
    q_ref, k_ref, v_ref = q_ref.at[0], k_ref.at[0], v_ref.at[0]
    scaled = lambda q: (q.astype(F32) * (NA_DH ** -0.5 * LOG2_E)).astype(BF16)
    rows = (seq_len - N_META) // GRID_W
    hw = NA_HEAD_GROUP * NA_DH
    nq = NA_HEAD_GROUP * GRID_W
    n_keys = NA_WIN_H * GRID_W
    meta_blk = 128

    def block_diag(q4):
        n = q4.shape[0]
        qrep = jnp.concatenate([q4] * NA_HEAD_GROUP, axis=0)
        rb = lax.broadcasted_iota(jnp.int32, (NA_HEAD_GROUP * n, hw), 0) // n
        lb = lax.broadcasted_iota(jnp.int32, (NA_HEAD_GROUP * n, hw), 1) // NA_DH
        return jnp.where(rb == lb, qrep, jnp.zeros_like(qrep))

    def pick_diag(o, n):
        lb = lax.broadcasted_iota(jnp.int32, (n, hw), 1) // NA_DH
        out = jnp.zeros((n, hw), F32)
        for h in range(NA_HEAD_GROUP):
            out = jnp.where(lb == h, o[h * n:(h + 1) * n, :], out)
        return out

    k_meta = k_ref[0:meta_blk, :]
    v_meta = v_ref[0:meta_blk, :]

    def meta_scores(qbd):
        s = _dot(qbd, k_meta, _NT)
        lane = lax.broadcasted_iota(jnp.int32, s.shape, 1)
        return jnp.where(lane < N_META, s, MASK_VALUE)

    qm = block_diag(scaled(q_ref[0:N_META, :]))
    sm = meta_scores(qm)
    pm = jnp.exp2(sm - jnp.max(sm, axis=-1, keepdims=True))
    om = _dot(pm.astype(BF16), v_meta) / jnp.sum(pm, axis=-1, keepdims=True)
    o_ref[0, 0:N_META, :] = pick_diag(om, N_META).astype(o_ref.dtype)

    def group_body(gi, carry):
        scores = []
        for j in range(_NA_ROW_GROUP):
            r = gi * _NA_ROW_GROUP + j
            rs = jnp.clip(r - NA_WIN_H // 2, 0, rows - NA_WIN_H)
            q_start = pl.multiple_of(N_META + r * GRID_W, 16)
            k_start = pl.multiple_of(N_META + rs * GRID_W, 16)
            qbd = block_diag(scaled(q_ref[pl.ds(q_start, GRID_W), :]))
            sw = _dot(qbd, k_ref[pl.ds(k_start, n_keys), :], _NT)
            scores.append((r - rs, q_start, k_start, sw, meta_scores(qbd)))
        probs = []
        for delta, q_start, k_start, sw, smeta in scores:
            bias = jnp.concatenate(
                [t_ref[2 * j - delta + NA_WIN_H - 1] for j in range(NA_WIN_H // 2)], axis=-1)
            sw = sw + bias
            blocks = lambda a: [a[:, i:i + meta_blk] for i in range(0, n_keys, meta_blk)]
            mx = jnp.max(functools.reduce(jnp.maximum, blocks(sw), smeta), axis=-1, keepdims=True)
            pw = jnp.exp2(sw - mx)
            pmeta = jnp.exp2(smeta - mx)
            den = jnp.sum(functools.reduce(jnp.add, blocks(pw), pmeta), axis=-1, keepdims=True)
            probs.append((q_start, k_start, pw.astype(BF16), pmeta.astype(BF16), den))
        outs = [(q_start, _dot(pw, v_ref[pl.ds(k_start, n_keys), :]) + _dot(pmeta, v_meta), den)
                for q_start, k_start, pw, pmeta, den in probs]
        for q_start, o, den in outs:
            o_ref[0, pl.ds(q_start, GRID_W), :] = pick_diag(o / den, GRID_W).astype(o_ref.dtype)
        return carry

    lax.fori_loop(0, rows // _NA_ROW_GROUP, group_body, 0, unroll=2)


def _na_bias_table(rel_bias):
    depth = rel_bias.shape[0]
    col = np.arange(GRID_W)
    cs = np.clip(col - NA_WIN_W // 2, 0, GRID_W - NA_WIN_W)
    inside = (col[None, :] >= cs[:, None]) & (col[None, :] < cs[:, None] + NA_WIN_W)
    dc = col[None, :] - col[:, None] + NA_WIN_W - 1
    n_dc = 2 * NA_WIN_W - 1
    onehot = (np.arange(n_dc)[:, None, None] == dc[None]) & inside[None]
    sel = np.zeros((2, n_dc, GRID_W, 2, GRID_W), np.float32)
    for p in range(2):
        sel[p, :, :, p, :] = onehot
    rb = rel_bias.astype(F32) * LOG2_E
    n_dr = rb.shape[2] - 1
    pairs = jnp.concatenate([rb[:, :, :-1], rb[:, :, 1:]], axis=-1)
    pairs = pairs.reshape(depth, NA_HEADS // NA_HEAD_GROUP, NA_HEAD_GROUP, n_dr, 2 * n_dc)
    t = jnp.einsum("lghic,cqpk->lgihqpk", pairs, jnp.asarray(sel.reshape(2 * n_dc, GRID_W, 2, GRID_W)),
                   precision=lax.Precision.HIGHEST)
    t = jnp.where(inside[:, None, :], t, MASK_VALUE)
    return t.reshape(depth, NA_HEADS // NA_HEAD_GROUP, n_dr, NA_HEAD_GROUP * GRID_W, 2 * GRID_W)


def _na(proj, table, layer):
    bsz, seq_len, _ = proj.shape
    hw = NA_HEAD_GROUP * NA_DH
    n_groups = NA_HEADS // NA_HEAD_GROUP
    kern = functools.partial(_na_kernel, seq_len=seq_len)
    blk = lambda off: pl.BlockSpec((1, seq_len, hw), lambda g, b: (b, 0, off + g))
    return pl.pallas_call(
        kern,
        grid=(n_groups, bsz),
        in_specs=[blk(_COL_NQ // hw), blk(_COL_NK // hw), blk(_COL_NV // hw),
                  pl.BlockSpec((None, None) + table.shape[2:], lambda g, b: (layer, g, 0, 0, 0))],
        out_specs=pl.BlockSpec((1, seq_len, hw), lambda g, b: (b, 0, g)),
        out_shape=jax.ShapeDtypeStruct((bsz, seq_len, NA_HEADS * NA_DH), BF16),
        compiler_params=pltpu.CompilerParams(
            dimension_semantics=("arbitrary", "arbitrary"),
            vmem_limit_bytes=V7X_VMEM_LIMIT_BYTES),
        name="na",
    )(proj, proj, proj, table)


def _layer_norm(z, w, b):
    mu = jnp.mean(z, axis=-1, keepdims=True)
    zc = z - mu
    var = jnp.mean(zc * zc, axis=-1, keepdims=True)
    return zc * lax.rsqrt(var + LN_EPS) * w + b


def _token_rows_spec(tm, n, tiles_per_seq):
    return pl.BlockSpec(
        (pl.Squeezed(), pl.Element(tm), pl.Element(n)),
        lambda i: (i // tiles_per_seq, pl.multiple_of(jnp.maximum((i % tiles_per_seq) * tm - N_META, 0), 8), 0))


def _embed_kernel(x_ref, meta_ref, o_ref, *, tiles_per_seq):
    tm = o_ref.shape[0]
    head = jnp.concatenate([meta_ref[...], x_ref[0:tm - N_META, :]], axis=0)
    first = pl.program_id(0) % tiles_per_seq == 0
    o_ref[...] = jnp.where(first, head, x_ref[...]).astype(o_ref.dtype)


def _embed(x, meta, tm):
    bsz, seq, n = x.shape
    seq_len = seq + N_META
    assert seq_len % tm == 0 and tm % 16 == 0
    tiles_per_seq = seq_len // tm
    return pl.pallas_call(
        functools.partial(_embed_kernel, tiles_per_seq=tiles_per_seq),
        grid=(bsz * tiles_per_seq,),
        in_specs=[_token_rows_spec(tm, n, tiles_per_seq), pl.BlockSpec((N_META, n), lambda i: (0, 0))],
        out_specs=pl.BlockSpec((tm, n), lambda i: (i, 0)),
        out_shape=jax.ShapeDtypeStruct((bsz * seq_len, n), BF16),
        compiler_params=pltpu.CompilerParams(
            dimension_semantics=("arbitrary",),
            vmem_limit_bytes=V7X_VMEM_LIMIT_BYTES),
        name="embed",
    )(x, meta)


def _out_ln_kernel(yg_ref, yn_ref, w_ref, *rest, alpha, sub_rows, tiles_per_seq):
    from_tokens = len(rest) == 6
    res_ref = rest[0]
    lw_ref, lb_ref, o_ref, ob_ref = rest[-4:]
    kg = yg_ref.shape[1]
    starts = [sum(sub_rows[:i]) for i in range(len(sub_rows))]
    mixes = [_dot(yg_ref[r0:r0 + n, :], w_ref[0:kg, :]) + _dot(yn_ref[r0:r0 + n, :], w_ref[kg:, :])
             for r0, n in zip(starts, sub_rows)]
    for r0, n, mix in zip(starts, sub_rows, mixes):
        if not from_tokens:
            res = res_ref[r0:r0 + n, :]
        else:
            meta_ref = rest[1]
            if r0 == 0:
                head = jnp.concatenate([meta_ref[...], res_ref[0:n - N_META, :]], axis=0)
            else:
                head = res_ref[r0 - N_META:r0 - N_META + n, :]
            first = pl.program_id(0) % tiles_per_seq == 0
            res = jnp.where(first, head, res_ref[r0:r0 + n, :])
        o = _layer_norm(alpha * res + mix, lw_ref[...], lb_ref[...])
        o_ref[r0:r0 + n, :] = o
        ob_ref[r0:r0 + n, :] = o.astype(ob_ref.dtype)


def _out_ln(yg, yn, w, res, lw, lb, layer, alpha, tm, sub_rows, seq_len):
    m, kg = yg.shape
    kn = yn.shape[1]
    n = w.shape[1]
    assert sum(sub_rows) == tm and seq_len % tm == 0 and all(r % 16 == 0 for r in sub_rows)
    tiles_per_seq = seq_len // tm
    row = lambda width: pl.BlockSpec((tm, width), lambda i: (i, 0))
    per_layer = lambda shape: pl.BlockSpec((None,) + shape, lambda i: (layer, 0, 0))
    if isinstance(res, tuple):
        res_specs = [_token_rows_spec(tm, n, tiles_per_seq), pl.BlockSpec((N_META, n), lambda i: (0, 0))]
    else:
        res, res_specs = (res,), [row(n)]
    return pl.pallas_call(
        functools.partial(_out_ln_kernel, alpha=alpha, sub_rows=sub_rows, tiles_per_seq=tiles_per_seq),
        grid=(m // tm,),
        in_specs=[row(kg), row(kn),
                  pl.BlockSpec((kg + kn, n), lambda i: (0, 0), pipeline_mode=pl.Buffered(1))]
        + res_specs + [per_layer((1, n)), per_layer((1, n))],
        out_specs=[row(n), row(n)],
        out_shape=[jax.ShapeDtypeStruct((m, n), F32), jax.ShapeDtypeStruct((m, n), BF16)],
        compiler_params=pltpu.CompilerParams(
            dimension_semantics=("arbitrary",),
            vmem_limit_bytes=V7X_VMEM_LIMIT_BYTES),
        name="out_ln",
    )(yg, yn, w, *res, lw, lb)


def _ffn_ln_kernel(a_ref, w1_ref, w2_ref, h_ref, lw_ref, lb_ref, o_ref, ob_ref=None, *, alpha, sub_rows):
    f = pl.program_id(1)
    last = pl.num_programs(1) - 1

    def hidden():
        return jnp.square(jnp.maximum(_dot(a_ref[...], w1_ref[...]), 0.0)).astype(BF16)

    @pl.when(f == 0)
    def _():
        o_ref[...] = alpha * h_ref[...] + _dot(hidden(), w2_ref[...])

    @pl.when(jnp.logical_and(f > 0, f < last))
    def _():
        o_ref[...] += _dot(hidden(), w2_ref[...])

    @pl.when(f == last)
    def _():
        u = hidden()
        starts = [sum(sub_rows[:i]) for i in range(len(sub_rows))]
        parts = [_dot(u[r0:r0 + n, :], w2_ref[...]) for r0, n in zip(starts, sub_rows)]
        for r0, n, part in zip(starts, sub_rows, parts):
            o = _layer_norm(o_ref[r0:r0 + n, :] + part, lw_ref[...], lb_ref[...])
            o_ref[r0:r0 + n, :] = o
            if ob_ref is not None:
                ob_ref[r0:r0 + n, :] = o.astype(ob_ref.dtype)


def _ffn_ln(a, w1, w2, h, lw, lb, layer, alpha, tm, sub_rows, tf, skip_rows):
    bsz, seq_len, d = a.shape
    d_ff = w1.shape[1]
    rows = seq_len - skip_rows
    assert rows % tm == 0 and skip_rows % 16 == 0 and tm % 16 == 0 and d_ff % tf == 0 and d_ff // tf >= 2
    assert sum(sub_rows) == tm and all(r % 16 == 0 for r in sub_rows)
    nt = rows // tm
    row_in = pl.BlockSpec((pl.Squeezed(), pl.Element(tm), pl.Element(d)),
                          lambda i, f: (i // nt, pl.multiple_of(skip_rows + (i % nt) * tm, 16), 0))
    row_out = pl.BlockSpec((None, tm, d), lambda i, f: (i // nt, i % nt, 0))
    vec = pl.BlockSpec((None, 1, d), lambda i, f: (layer, 0, 0))
    out_shape = [jax.ShapeDtypeStruct((bsz, rows, d), F32)]
    if not skip_rows:
        out_shape.append(jax.ShapeDtypeStruct((bsz, rows, d), BF16))
    return pl.pallas_call(
        functools.partial(_ffn_ln_kernel, alpha=alpha, sub_rows=sub_rows),
        grid=(bsz * nt, d_ff // tf),
        in_specs=[row_in, pl.BlockSpec((d, tf), lambda i, f: (0, f)),
                  pl.BlockSpec((tf, d), lambda i, f: (f, 0)), row_in, vec, vec],
        out_specs=[row_out] * len(out_shape),
        out_shape=out_shape,
        compiler_params=pltpu.CompilerParams(
            dimension_semantics=("arbitrary", "arbitrary"),
            vmem_limit_bytes=V7X_VMEM_LIMIT_BYTES),
        name="ffn_ln",
    )(a, w1, w2, h, lw, lb)


_GLA_QK = GLA_HEADS * GLA_DK
_GLA_VR = GLA_HEADS * GLA_DV
_NA_W = NA_HEADS * NA_DH
_COL_GQ = 0
_COL_GK = _COL_GQ + _GLA_QK
_COL_GV = _COL_GK + _GLA_QK
_COL_GR = _COL_GV + _GLA_VR
_COL_NQ = _COL_GR + _GLA_VR
_COL_NK = _COL_NQ + _NA_W
_COL_NV = _COL_NK + _NA_W
_PROJ_COLS = _COL_NV + _NA_W
_W_IN_LR = _COL_NQ
_W_IN_COLS = _PROJ_COLS + 2 * GLA_RANK
_PROJ_TN = 1024
_LR_TN = 128
_ROW_TILE = 688
_PROJ_ROW_TILE = 1376
_OUT_ROW_TILE = 688
_SUB_ROWS = (176, 176, 176, 160)
_FFN_TF = 512
_LAST_ROW_TILE = 512
_LAST_SUB_ROWS = (128, 128, 128, 128)
_LAST_FFN_TF = 1024


def _gla_gate_params(w_up, b_up):
    depth = w_up.shape[0]
    wc = jnp.zeros((depth, GLA_HEADS, 128, 2 * GLA_DK), F32)
    for z in range(2):
        w = w_up[:, z].astype(F32).reshape(depth, GLA_RANK, GLA_HEADS, GLA_DK).transpose(0, 2, 1, 3)
        wc = wc.at[:, :, z * GLA_RANK:(z + 1) * GLA_RANK, z * GLA_DK:(z + 1) * GLA_DK].set(w)
    bc = b_up.astype(F32).reshape(depth, 2, GLA_HEADS, GLA_DK).transpose(0, 2, 1, 3)
    return wc.astype(BF16), bc.reshape(depth, GLA_HEADS, 1, 2 * GLA_DK)


def kernel(x, meta, w_in, gla_w_up, gla_b_up, gla_norm_w, na_rel_bias, w_out, ln1_w, ln1_b,
           w_ff1, w_ff2, ln2_w, ln2_b):
    bsz, seq, d = x.shape
    depth = w_in.shape[0]
    alpha = (2 * depth) ** 0.25
    seq_len = seq + N_META
    m = bsz * seq_len
    h = (x.astype(F32), meta.astype(F32))
    hb = _embed(*h, _ROW_TILE)
    assert w_in.shape[2] == _W_IN_COLS and _W_IN_LR % _PROJ_TN == 0 and _PROJ_COLS % _PROJ_TN == 0
    w_in_t = jnp.swapaxes(w_in, 1, 2)
    wc, bc = _gla_gate_params(gla_w_up, gla_b_up)
    nw = gla_norm_w.astype(F32).reshape(depth, 1, GLA_DV)
    table = _na_bias_table(na_rel_bias)
    vec = lambda p: p.astype(F32).reshape(depth, 1, d)
    ln1_w, ln1_b, ln2_w, ln2_b = vec(ln1_w), vec(ln1_b), vec(ln2_w), vec(ln2_b)
    for l in range(depth):
        proj, w_out_b, w_ff1_b, w_ff2_b = _proj(
            hb, w_in_t, l, _PROJ_COLS // _PROJ_TN, _PROJ_TN, 0, _W_IN_LR // _PROJ_TN, 2 * GLA_RANK,
            _PROJ_ROW_TILE, "proj", cast=(w_out, w_ff1, w_ff2))
        proj = proj.reshape(bsz, seq_len, _PROJ_COLS)
        y_gla = _gla(proj, hb.reshape(bsz, seq_len, d), w_in_t, wc, bc, nw, l).reshape(m, GLA_HEADS * GLA_DV)
        y_na = _na(proj, table, l).reshape(m, NA_HEADS * NA_DH)
        h, hb = _out_ln(y_gla, y_na, w_out_b, h, ln1_w, ln1_b, l, alpha, _OUT_ROW_TILE, _SUB_ROWS, seq_len)
        hb3, h3 = hb.reshape(bsz, seq_len, d), h.reshape(bsz, seq_len, d)
        if l + 1 < depth:
            h, hb = _ffn_ln(hb3, w_ff1_b, w_ff2_b, h3, ln2_w, ln2_b, l, alpha, _ROW_TILE, _SUB_ROWS, _FFN_TF, 0)
            h, hb = h.reshape(m, d), hb.reshape(m, d)
        else:
            out, = _ffn_ln(hb3, w_ff1_b, w_ff2_b, h3, ln2_w, ln2_b, l, alpha, _LAST_ROW_TILE, _LAST_SUB_ROWS,
                           _LAST_FFN_TF, N_META)
    return out.astype(x.dtype)
```

```python
import functools

import numpy as np
import jax
import jax.numpy as jnp
from jax import lax
from jax.experimental import pallas as pl
from jax.experimental.pallas import tpu as pltpu

N_META = 16
GRID_W = 64
GLA_HEADS = 4
GLA_DK = 128
GLA_DV = 256
GLA_RANK = 16
GLA_TAU = 16.0
_GLA_BLOCK = 256
_GLA_GROUP = 4
_GLA_SAFE_RANGE = 115.0
LOG2_E = 1.4426950408889634
NA_HEADS = 16
NA_DH = 64
NA_WIN_H = 8
NA_WIN_W = 16
NA_HEAD_GROUP = 4
_NA_ROW_GROUP = 4
LN_EPS = 1e-5
MASK_VALUE = -1e30

V7X_VMEM_LIMIT_BYTES = 58 * 1024 * 1024

F32 = jnp.float32
BF16 = jnp.bfloat16

_NT = (((1,), (1,)), ((), ()))
_TN = (((0,), (0,)), ((), ()))


def _dot(a, b, dims=None):
    if dims is None:
        return jnp.dot(a, b, preferred_element_type=F32)
    return lax.dot_general(a, b, dims, preferred_element_type=F32)


def _proj_kernel(*refs, n_cast):
    a_ref, w_ref = refs[:2]
    cast_src = refs[2:2 + n_cast]
    o_ref = refs[2 + n_cast]
    cast_dst = refs[3 + n_cast:3 + 2 * n_cast]
    wb_ref = refs[-1]

    @pl.when(pl.program_id(1) == 0)
    def _():
        wb_ref[...] = w_ref[...].astype(BF16)

    o_ref[...] = _dot(a_ref[...], wb_ref[...], _NT).astype(o_ref.dtype)
    for src, dst in zip(cast_src, cast_dst):
        dst[...] = src[...].astype(dst.dtype)


def _proj(a, wt, layer, n_tiles, tn, first, skip_from, skip, tm, name, cast=()):
    m, k = a.shape
    n_rows = m // tm
    assert first % 16 == 0 and tn % 16 == 0 and skip % 16 == 0

    def w_index(j, i):
        return (layer, pl.multiple_of(first + j * tn + jnp.where(j >= skip_from, skip, 0), 16), 0)

    cast_in, cast_out, cast_shapes = [], [], []
    for arr in cast:
        _, r, c = arr.shape
        n_slabs = max(s for s in range(1, n_tiles * n_rows + 1) if r % (16 * s) == 0)
        slab = lambda j, i, n_slabs=n_slabs: jnp.minimum(j * n_rows + i, n_slabs - 1)
        cast_in.append(pl.BlockSpec((None, r // n_slabs, c), lambda j, i, slab=slab: (layer, slab(j, i), 0)))
        cast_out.append(pl.BlockSpec((r // n_slabs, c), lambda j, i, slab=slab: (slab(j, i), 0)))
        cast_shapes.append(jax.ShapeDtypeStruct((r, c), BF16))

    return pl.pallas_call(
        functools.partial(_proj_kernel, n_cast=len(cast)),
        grid=(n_tiles, n_rows),
        in_specs=[pl.BlockSpec((tm, k), lambda j, i: (i, 0)),
                  pl.BlockSpec((pl.Squeezed(), pl.Element(tn), pl.Element(k)), w_index)] + cast_in,
        out_specs=[pl.BlockSpec((tm, tn), lambda j, i: (i, j))] + cast_out,
        out_shape=[jax.ShapeDtypeStruct((m, n_tiles * tn), BF16)] + cast_shapes,
        scratch_shapes=[pltpu.VMEM((tn, k), BF16)],
        compiler_params=pltpu.CompilerParams(
            dimension_semantics=("arbitrary", "arbitrary"),
            vmem_limit_bytes=V7X_VMEM_LIMIT_BYTES),
        name=name,
    )(a, wt, *cast)


def _gla_decays(rows, c, g_ref):
    g2 = g_ref[rows, :]
    t_idx = lax.broadcasted_iota(jnp.int32, (c, c), 0)
    s_idx = lax.broadcasted_iota(jnp.int32, (c, c), 1)
    tri = jnp.where(s_idx <= t_idx, 1.0, 0.0).astype(BF16)
    g_hi = g2.astype(BF16)
    g_lo = (g2 - g_hi.astype(F32)).astype(BF16)
    p = _dot(tri, g_hi) + _dot(tri, g_lo)
    cb = p[c - 1:c, GLA_DK:] - p[:, GLA_DK:] + g2[:, GLA_DK:]
    return p[:, :GLA_DK], cb


def _gla_group_terms(chunks, q_ref, k_ref, v_ref, g_ref, qi_ref, oi_ref, ut_ref, e_ref):
    decays = [_gla_decays(rows, c, g_ref) for rows, _, c in chunks]
    operands, ranges = [], []
    for (rows, ci, c), (bf, cb) in zip(chunks, decays):
        bf_end = bf[c - 1:c, :]
        cb_end = cb[0:1, :]
        bf_mid = bf[c // 2 - 1:c // 2, :]
        cb_mid = cb[c // 2:c // 2 + 1, :]
        q = q_ref[0, rows, :].astype(F32) * (GLA_DK ** -0.5)
        k = k_ref[0, rows, :].astype(F32)
        qi_f = q * jnp.exp2(bf)
        ks_f = k * jnp.exp2(bf_end - bf)
        qi_b = q * jnp.exp2(cb)
        ks_b = k * jnp.exp2(cb_end - cb)
        qi_ref[rows, :] = jnp.concatenate([qi_f, qi_b], axis=-1).astype(BF16)
        e_ref[ci] = jnp.broadcast_to(jnp.exp2(jnp.concatenate([bf_end, cb_end], axis=-1)), (8, 2 * GLA_DK))
        operands.append(((qi_f * jnp.exp2(-bf_mid)).astype(BF16), (ks_f * jnp.exp2(bf_mid - bf_end)).astype(BF16),
                         (qi_b * jnp.exp2(-cb_mid)).astype(BF16), (ks_b * jnp.exp2(cb_mid - cb_end)).astype(BF16),
                         jnp.concatenate([ks_f, ks_b], axis=-1).astype(BF16)))
        ranges.append(jnp.maximum(-bf_end, -cb_end))
    scores = [(_dot(qt_f, kt_f, _NT), _dot(qt_b, kt_b, _NT)) for qt_f, kt_f, qt_b, kt_b, _ in operands]

    def causal_mix(sc_f, sc_b):
        n = sc_f.shape[0]
        t_idx = lax.broadcasted_iota(jnp.int32, (n, n), 0)
        s_idx = lax.broadcasted_iota(jnp.int32, (n, n), 1)
        return jnp.where(s_idx <= t_idx, sc_f, 0.0) + jnp.where(s_idx >= t_idx, sc_b, 0.0)

    for (rows, ci, c), (sc_f, sc_b) in zip(chunks, scores):
        if c % 256 == 0:
            h = c // 2
            top = jnp.concatenate([causal_mix(sc_f[:h, :h], sc_b[:h, :h]), sc_b[:h, h:]], axis=1)
            bot = jnp.concatenate([sc_f[h:, :h], causal_mix(sc_f[h:, h:], sc_b[h:, h:])], axis=1)
            sc = jnp.concatenate([top, bot], axis=0)
        else:
            sc = causal_mix(sc_f, sc_b)
        oi_ref[rows, :] = _dot(sc.astype(BF16), v_ref[0, rows, :])
    for (rows, ci, c), ops in zip(chunks, operands):
        ut_ref[ci] = _dot(v_ref[0, rows, :], ops[4], _TN)
    return ranges


def _gla_exact_intra(rows, size, q_ref, k_ref, v_ref, g_ref, oi_ref):
    c = size
    bf, cb = _gla_decays(rows, c, g_ref)
    q = q_ref[0, rows, :].astype(F32) * (GLA_DK ** -0.5)
    k = k_ref[0, rows, :].astype(F32)
    row = lax.broadcasted_iota(jnp.int32, (c, 1), 0)
    lane = lax.broadcasted_iota(jnp.int32, (c, c), 1)

    def body(t, sct):
        pick = row == t
        take = lambda a: jnp.sum(jnp.where(pick, a, 0.0), axis=0, keepdims=True)
        dec = (jnp.exp2(jnp.where(row <= t, take(bf) - bf, MASK_VALUE))
               + jnp.exp2(jnp.where(row >= t, take(cb) - cb, MASK_VALUE)))
        col = jnp.sum(k * take(q) * dec, axis=-1, keepdims=True)
        return jnp.where(lane == t, col, sct)

    sct = lax.fori_loop(0, c, body, jnp.zeros((c, c), F32))
    oi_ref[rows, :] = _dot(sct.astype(BF16), v_ref[0, rows, :], _TN)


def _gla_group_out(chunks, r_ref, nw_ref, qi_ref, oi_ref, st_ref, o_ref):
    inter = [_dot(qi_ref[rows, :], st_ref[ci], _NT) for rows, ci in chunks]
    for (rows, ci), o_inter in zip(chunks, inter):
        o = oi_ref[rows, :] + o_inter
        o = o * lax.rsqrt(jnp.mean(o * o, axis=-1, keepdims=True) + LN_EPS) * nw_ref[...]
        r = r_ref[0, rows, :].astype(F32)
        o_ref[0, rows, :] = (o * (r / (1.0 + jnp.exp(-r)))).astype(o_ref.dtype)


def _gla_kernel(q_ref, k_ref, v_ref, r_ref, hb_ref, wlr_ref, wc_ref, bc_ref, nw_ref, o_ref,
                g_ref, qi_ref, oi_ref, ut_ref, e_ref, st_ref, lr_ref, *, seq_len, gate_rows):
    c = _GLA_BLOCK
    n_full = (seq_len - N_META) // c
    n_chunks = n_full + 1

    @pl.when(pl.program_id(1) == 0)
    def _():
        w_lr = wlr_ref[...].astype(BF16)
        for i in range(seq_len // gate_rows):
            rows = pl.ds(i * gate_rows, gate_rows)
            lr_ref[rows, :] = _dot(hb_ref[0, rows, :], w_lr, _NT).astype(BF16)

    for i in range(seq_len // gate_rows):
        rows = pl.ds(i * gate_rows, gate_rows)
        x = _dot(lr_ref[rows, :], wc_ref[0]) + bc_ref[0]
        g_ref[rows, :] = (jnp.minimum(x, 0.0) - jnp.log(1.0 + jnp.exp(-jnp.abs(x)))) * (LOG2_E / GLA_TAU)

    def chunk_rows(ci):
        return pl.ds(pl.multiple_of(N_META + (ci - 1) * c, 16), c)

    terms = functools.partial(_gla_group_terms, q_ref=q_ref, k_ref=k_ref, v_ref=v_ref, g_ref=g_ref,
                              qi_ref=qi_ref, oi_ref=oi_ref, ut_ref=ut_ref, e_ref=e_ref)
    full = [(pl.ds(N_META + (ci - 1) * c, c), ci, c) for ci in range(1, n_chunks)]
    decay_range = terms([(pl.ds(0, N_META), 0, N_META)])
    for i in range(0, n_full, _GLA_GROUP):
        decay_range += terms(full[i:i + _GLA_GROUP])

    @pl.when(jnp.max(functools.reduce(jnp.maximum, decay_range)) > _GLA_SAFE_RANGE)
    def _():
        exact = functools.partial(_gla_exact_intra, q_ref=q_ref, k_ref=k_ref, v_ref=v_ref, g_ref=g_ref,
                                  oi_ref=oi_ref)
        exact(pl.ds(0, N_META), N_META)

        def exact_body(ci, carry):
            exact(chunk_rows(ci), c)
            return carry

        lax.fori_loop(1, n_chunks, exact_body, 0)

    def scan_body(i, carry):
        s_f, s_b = carry
        j = n_chunks - 1 - i
        st_ref[i, :, :GLA_DK] = s_f.astype(BF16)
        st_ref[j, :, GLA_DK:] = s_b.astype(BF16)
        s_f = s_f * e_ref[i, 0:1, :GLA_DK] + ut_ref[i, :, :GLA_DK]
        s_b = s_b * e_ref[j, 0:1, GLA_DK:] + ut_ref[j, :, GLA_DK:]
        return s_f, s_b

    zero = jnp.zeros((GLA_DV, GLA_DK), F32)
    lax.fori_loop(0, n_chunks, scan_body, (zero, zero))

    out = functools.partial(_gla_group_out, r_ref=r_ref, nw_ref=nw_ref, qi_ref=qi_ref, oi_ref=oi_ref,
                            st_ref=st_ref, o_ref=o_ref)
    out([(pl.ds(0, N_META), 0)])
    for i in range(0, n_full, _GLA_GROUP):
        out([(rows, ci) for rows, ci, _ in full[i:i + _GLA_GROUP]])


def _gla(proj, hb, wt, wc, bc, nw, layer):
    bsz, seq_len, _ = proj.shape
    d = hb.shape[2]
    assert (seq_len - N_META) % _GLA_BLOCK == 0 and seq_len % _ROW_TILE == 0
    n_chunks = (seq_len - N_META) // _GLA_BLOCK + 1
    kern = functools.partial(_gla_kernel, seq_len=seq_len, gate_rows=_ROW_TILE)
    qk_blk = lambda off: pl.BlockSpec((1, seq_len, GLA_DK), lambda b, h: (b, 0, off + h))
    vr_blk = lambda off: pl.BlockSpec((1, seq_len, GLA_DV), lambda b, h: (b, 0, off + h))
    return pl.pallas_call(
        kern,
        grid=(bsz, GLA_HEADS),
        in_specs=[qk_blk(_COL_GQ // GLA_DK), qk_blk(_COL_GK // GLA_DK),
                  vr_blk(_COL_GV // GLA_DV), vr_blk(_COL_GR // GLA_DV),
                  pl.BlockSpec((1, seq_len, d), lambda b, h: (b, 0, 0)),
                  pl.BlockSpec((pl.Squeezed(), pl.Element(_LR_TN), pl.Element(d)),
                               lambda b, h: (layer, _W_IN_LR, 0)),
                  pl.BlockSpec((None, 1, 128, 2 * GLA_DK), lambda b, h: (layer, h, 0, 0)),
                  pl.BlockSpec((None, 1, 1, 2 * GLA_DK), lambda b, h: (layer, h, 0, 0)),
                  pl.BlockSpec((None, 1, GLA_DV), lambda b, h: (layer, 0, 0))],
        out_specs=pl.BlockSpec((1, seq_len, GLA_DV), lambda b, h: (b, 0, h)),
        out_shape=jax.ShapeDtypeStruct((bsz, seq_len, GLA_HEADS * GLA_DV), BF16),
        scratch_shapes=[pltpu.VMEM((seq_len, 2 * GLA_DK), F32),
                        pltpu.VMEM((seq_len, 2 * GLA_DK), BF16),
                        pltpu.VMEM((seq_len, GLA_DV), F32),
                        pltpu.VMEM((n_chunks, GLA_DV, 2 * GLA_DK), F32),
                        pltpu.VMEM((n_chunks, 8, 2 * GLA_DK), F32),
                        pltpu.VMEM((n_chunks, GLA_DV, 2 * GLA_DK), BF16),
                        pltpu.VMEM((seq_len, _LR_TN), BF16)],
        compiler_params=pltpu.CompilerParams(
            dimension_semantics=("arbitrary", "arbitrary"),
            vmem_limit_bytes=V7X_VMEM_LIMIT_BYTES),
        name="gla",
    )(proj, proj, proj, proj, hb, wt, wc, bc, nw)


def _na_kernel(q_ref, k_ref, v_ref, t_ref, o_ref, *, seq_len):
    q_ref, k_ref, v_ref = q_ref.at[0], k_ref.at[0], v_ref.at[0]
    scaled = lambda q: (q.astype(F32) * (NA_DH ** -0.5 * LOG2_E)).astype(BF16)
    rows = (seq_len - N_META) // GRID_W
    hw = NA_HEAD_GROUP * NA_DH
    n_keys = NA_WIN_H * GRID_W
    meta_blk = 128

    def block_diag(q4):
        n = q4.shape[0]
        qrep = jnp.concatenate([q4] * NA_HEAD_GROUP, axis=0)
        rb = lax.broadcasted_iota(jnp.int32, (NA_HEAD_GROUP * n, hw), 0) // n
        lb = lax.broadcasted_iota(jnp.int32, (NA_HEAD_GROUP * n, hw), 1) // NA_DH
        return jnp.where(rb == lb, qrep, jnp.zeros_like(qrep))

    def pick_diag(o, n):
        lb = lax.broadcasted_iota(jnp.int32, (n, hw), 1) // NA_DH
        out = jnp.zeros((n, hw), F32)
        for h in range(NA_HEAD_GROUP):
            out = jnp.where(lb == h, o[h * n:(h + 1) * n, :], out)
        return out

    k_meta = k_ref[0:meta_blk, :]
    v_meta = v_ref[0:meta_blk, :]

    def meta_scores(qbd):
        s = _dot(qbd, k_meta, _NT)
        lane = lax.broadcasted_iota(jnp.int32, s.shape, 1)
        return jnp.where(lane < N_META, s, MASK_VALUE)

    qm = block_diag(scaled(q_ref[0:N_META, :]))
    sm = meta_scores(qm)
    pm = jnp.exp2(sm - jnp.max(sm, axis=-1, keepdims=True))
    om = _dot(pm.astype(BF16), v_meta) / jnp.sum(pm, axis=-1, keepdims=True)
    o_ref[0, 0:N_META, :] = pick_diag(om, N_META).astype(o_ref.dtype)

    def group_body(gi, carry):
        scores = []
        for j in range(_NA_ROW_GROUP):
            r = gi * _NA_ROW_GROUP + j
            rs = jnp.clip(r - NA_WIN_H // 2, 0, rows - NA_WIN_H)
            q_start = pl.multiple_of(N_META + r * GRID_W, 16)
            k_start = pl.multiple_of(N_META + rs * GRID_W, 16)
            qbd = block_diag(scaled(q_ref[pl.ds(q_start, GRID_W), :]))
            sw = _dot(qbd, k_ref[pl.ds(k_start, n_keys), :], _NT)
            scores.append((r - rs, q_start, k_start, sw, meta_scores(qbd)))
        probs = []
        for delta, q_start, k_start, sw, smeta in scores:
            bias = jnp.concatenate(
                [t_ref[2 * j - delta + NA_WIN_H - 1] for j in range(NA_WIN_H // 2)], axis=-1)
            sw = sw + bias
            blocks = lambda a: [a[:, i:i + meta_blk] for i in range(0, n_keys, meta_blk)]
            mx = jnp.max(functools.reduce(jnp.maximum, blocks(sw), smeta), axis=-1, keepdims=True)
            pw = jnp.exp2(sw - mx)
            pmeta = jnp.exp2(smeta - mx)
            den = jnp.sum(functools.reduce(jnp.add, blocks(pw), pmeta), axis=-1, keepdims=True)
            probs.append((q_start, k_start, pw.astype(BF16), pmeta.astype(BF16), den))
        outs = [(q_start, _dot(pw, v_ref[pl.ds(k_start, n_keys), :]) + _dot(pmeta, v_meta), den)
                for q_start, k_start, pw, pmeta, den in probs]
        for q_start, o, den in outs:
            o_ref[0, pl.ds(q_start, GRID_W), :] = pick_diag(o / den, GRID_W).astype(o_ref.dtype)
        return carry

    lax.fori_loop(0, rows // _NA_ROW_GROUP, group_body, 0)


def _na_bias_table(rel_bias):
    depth = rel_bias.shape[0]
    col = np.arange(GRID_W)
    cs = np.clip(col - NA_WIN_W // 2, 0, GRID_W - NA_WIN_W)
    inside = (col[None, :] >= cs[:, None]) & (col[None, :] < cs[:, None] + NA_WIN_W)
    dc = col[None, :] - col[:, None] + NA_WIN_W - 1
    onehot = (np.arange(2 * NA_WIN_W - 1)[:, None, None] == dc[None]) & inside[None]
    t = jnp.einsum("lhdj,jqk->lhdqk", rel_bias.astype(F32), jnp.asarray(onehot, F32),
                   precision=lax.Precision.HIGHEST)
    t = jnp.where(inside, t * LOG2_E, MASK_VALUE)
    t = jnp.concatenate([t[:, :, :-1], t[:, :, 1:]], axis=-1)
    n_dr = t.shape[2]
    t = t.reshape(depth, NA_HEADS // NA_HEAD_GROUP, NA_HEAD_GROUP, n_dr, GRID_W, 2 * GRID_W)
    return t.transpose(0, 1, 3, 2, 4, 5).reshape(depth, NA_HEADS // NA_HEAD_GROUP, n_dr,
                                                  NA_HEAD_GROUP * GRID_W, 2 * GRID_W)


def _na(proj, table, layer):
    bsz, seq_len, _ = proj.shape
    hw = NA_HEAD_GROUP * NA_DH
    n_groups = NA_HEADS // NA_HEAD_GROUP
    kern = functools.partial(_na_kernel, seq_len=seq_len)
    blk = lambda off: pl.BlockSpec((1, seq_len, hw), lambda g, b: (b, 0, off + g))
    return pl.pallas_call(
        kern,
        grid=(n_groups, bsz),
        in_specs=[blk(_COL_NQ // hw), blk(_COL_NK // hw), blk(_COL_NV // hw),
                  pl.BlockSpec((None, None) + table.shape[2:], lambda g, b: (layer, g, 0, 0, 0))],
        out_specs=pl.BlockSpec((1, seq_len, hw), lambda g, b: (b, 0, g)),
        out_shape=jax.ShapeDtypeStruct((bsz, seq_len, NA_HEADS * NA_DH), BF16),
        compiler_params=pltpu.CompilerParams(
            dimension_semantics=("arbitrary", "arbitrary"),
            vmem_limit_bytes=V7X_VMEM_LIMIT_BYTES),
        name="na",
    )(proj, proj, proj, table)


def _layer_norm(z, w, b):
    mu = jnp.mean(z, axis=-1, keepdims=True)
    zc = z - mu
    var = jnp.mean(zc * zc, axis=-1, keepdims=True)
    return zc * lax.rsqrt(var + LN_EPS) * w + b


def _token_rows_spec(tm, n, tiles_per_seq):
    return pl.BlockSpec(
        (pl.Squeezed(), pl.Element(tm), pl.Element(n)),
        lambda i: (i // tiles_per_seq, pl.multiple_of(jnp.maximum((i % tiles_per_seq) * tm - N_META, 0), 8), 0))


def _embed_kernel(x_ref, meta_ref, o_ref, *, tiles_per_seq):
    tm = o_ref.shape[0]
    head = jnp.concatenate([meta_ref[...], x_ref[0:tm - N_META, :]], axis=0)
    first = pl.program_id(0) % tiles_per_seq == 0
    o_ref[...] = jnp.where(first, head, x_ref[...]).astype(o_ref.dtype)


def _embed(x, meta, tm):
    bsz, seq, n = x.shape
    seq_len = seq + N_META
    assert seq_len % tm == 0 and tm % 16 == 0
    tiles_per_seq = seq_len // tm
    return pl.pallas_call(
        functools.partial(_embed_kernel, tiles_per_seq=tiles_per_seq),
        grid=(bsz * tiles_per_seq,),
        in_specs=[_token_rows_spec(tm, n, tiles_per_seq), pl.BlockSpec((N_META, n), lambda i: (0, 0))],
        out_specs=pl.BlockSpec((tm, n), lambda i: (i, 0)),
        out_shape=jax.ShapeDtypeStruct((bsz * seq_len, n), BF16),
        compiler_params=pltpu.CompilerParams(
            dimension_semantics=("arbitrary",),
            vmem_limit_bytes=V7X_VMEM_LIMIT_BYTES),
        name="embed",
    )(x, meta)


def _out_ln_kernel(yg_ref, yn_ref, w_ref, *rest, alpha, sub_rows, tiles_per_seq):
    from_tokens = len(rest) == 6
    res_ref = rest[0]
    lw_ref, lb_ref, o_ref, ob_ref = rest[-4:]
    kg = yg_ref.shape[1]
    starts = [sum(sub_rows[:i]) for i in range(len(sub_rows))]
    mixes = [_dot(yg_ref[r0:r0 + n, :], w_ref[0:kg, :]) + _dot(yn_ref[r0:r0 + n, :], w_ref[kg:, :])
             for r0, n in zip(starts, sub_rows)]
    for r0, n, mix in zip(starts, sub_rows, mixes):
        if not from_tokens:
            res = res_ref[r0:r0 + n, :]
        else:
            meta_ref = rest[1]
            if r0 == 0:
                head = jnp.concatenate([meta_ref[...], res_ref[0:n - N_META, :]], axis=0)
            else:
                head = res_ref[r0 - N_META:r0 - N_META + n, :]
            first = pl.program_id(0) % tiles_per_seq == 0
            res = jnp.where(first, head, res_ref[r0:r0 + n, :])
        o = _layer_norm(alpha * res + mix, lw_ref[...], lb_ref[...])
        o_ref[r0:r0 + n, :] = o
        ob_ref[r0:r0 + n, :] = o.astype(ob_ref.dtype)


def _out_ln(yg, yn, w, res, lw, lb, layer, alpha, tm, sub_rows, seq_len):
    m, kg = yg.shape
    kn = yn.shape[1]
    n = w.shape[1]
    assert sum(sub_rows) == tm and seq_len % tm == 0 and all(r % 16 == 0 for r in sub_rows)
    tiles_per_seq = seq_len // tm
    row = lambda width: pl.BlockSpec((tm, width), lambda i: (i, 0))
    per_layer = lambda shape: pl.BlockSpec((None,) + shape, lambda i: (layer, 0, 0))
    if isinstance(res, tuple):
        res_specs = [_token_rows_spec(tm, n, tiles_per_seq), pl.BlockSpec((N_META, n), lambda i: (0, 0))]
    else:
        res, res_specs = (res,), [row(n)]
    return pl.pallas_call(
        functools.partial(_out_ln_kernel, alpha=alpha, sub_rows=sub_rows, tiles_per_seq=tiles_per_seq),
        grid=(m // tm,),
        in_specs=[row(kg), row(kn),
                  pl.BlockSpec((kg + kn, n), lambda i: (0, 0), pipeline_mode=pl.Buffered(1))]
        + res_specs + [per_layer((1, n)), per_layer((1, n))],
        out_specs=[row(n), row(n)],
        out_shape=[jax.ShapeDtypeStruct((m, n), F32), jax.ShapeDtypeStruct((m, n), BF16)],
        compiler_params=pltpu.CompilerParams(
            dimension_semantics=("arbitrary",),
            vmem_limit_bytes=V7X_VMEM_LIMIT_BYTES),
        name="out_ln",
    )(yg, yn, w, *res, lw, lb)


def _ffn_ln_kernel(a_ref, w1_ref, w2_ref, h_ref, lw_ref, lb_ref, o_ref, ob_ref=None, *, alpha, sub_rows):
    f = pl.program_id(1)
    last = pl.num_programs(1) - 1

    def hidden():
        return jnp.square(jnp.maximum(_dot(a_ref[...], w1_ref[...]), 0.0)).astype(BF16)

    @pl.when(f == 0)
    def _():
        o_ref[...] = alpha * h_ref[...] + _dot(hidden(), w2_ref[...])

    @pl.when(jnp.logical_and(f > 0, f < last))
    def _():
        o_ref[...] += _dot(hidden(), w2_ref[...])

    @pl.when(f == last)
    def _():
        u = hidden()
        starts = [sum(sub_rows[:i]) for i in range(len(sub_rows))]
        parts = [_dot(u[r0:r0 + n, :], w2_ref[...]) for r0, n in zip(starts, sub_rows)]
        for r0, n, part in zip(starts, sub_rows, parts):
            o = _layer_norm(o_ref[r0:r0 + n, :] + part, lw_ref[...], lb_ref[...])
            o_ref[r0:r0 + n, :] = o
            if ob_ref is not None:
                ob_ref[r0:r0 + n, :] = o.astype(ob_ref.dtype)


def _ffn_ln(a, w1, w2, h, lw, lb, layer, alpha, tm, sub_rows, tf, skip_rows):
    bsz, seq_len, d = a.shape
    d_ff = w1.shape[1]
    rows = seq_len - skip_rows
    assert rows % tm == 0 and skip_rows % 16 == 0 and tm % 16 == 0 and d_ff % tf == 0 and d_ff // tf >= 2
    assert sum(sub_rows) == tm and all(r % 16 == 0 for r in sub_rows)
    nt = rows // tm
    row_in = pl.BlockSpec((pl.Squeezed(), pl.Element(tm), pl.Element(d)),
                          lambda i, f: (i // nt, pl.multiple_of(skip_rows + (i % nt) * tm, 16), 0))
    row_out = pl.BlockSpec((None, tm, d), lambda i, f: (i // nt, i % nt, 0))
    vec = pl.BlockSpec((None, 1, d), lambda i, f: (layer, 0, 0))
    out_shape = [jax.ShapeDtypeStruct((bsz, rows, d), F32)]
    if not skip_rows:
        out_shape.append(jax.ShapeDtypeStruct((bsz, rows, d), BF16))
    return pl.pallas_call(
        functools.partial(_ffn_ln_kernel, alpha=alpha, sub_rows=sub_rows),
        grid=(bsz * nt, d_ff // tf),
        in_specs=[row_in, pl.BlockSpec((d, tf), lambda i, f: (0, f)),
                  pl.BlockSpec((tf, d), lambda i, f: (f, 0)), row_in, vec, vec],
        out_specs=[row_out] * len(out_shape),
        out_shape=out_shape,
        compiler_params=pltpu.CompilerParams(
            dimension_semantics=("arbitrary", "arbitrary"),
            vmem_limit_bytes=V7X_VMEM_LIMIT_BYTES),
        name="ffn_ln",
    )(a, w1, w2, h, lw, lb)


_GLA_QK = GLA_HEADS * GLA_DK
_GLA_VR = GLA_HEADS * GLA_DV
_NA_W = NA_HEADS * NA_DH
_COL_GQ = 0
_COL_GK = _COL_GQ + _GLA_QK
_COL_GV = _COL_GK + _GLA_QK
_COL_GR = _COL_GV + _GLA_VR
_COL_NQ = _COL_GR + _GLA_VR
_COL_NK = _COL_NQ + _NA_W
_COL_NV = _COL_NK + _NA_W
_PROJ_COLS = _COL_NV + _NA_W
_W_IN_LR = _COL_NQ
_W_IN_COLS = _PROJ_COLS + 2 * GLA_RANK
_PROJ_TN = 1024
_LR_TN = 128
_ROW_TILE = 688
_PROJ_ROW_TILE = 1376
_OUT_ROW_TILE = 688
_SUB_ROWS = (176, 176, 176, 160)
_FFN_TF = 512
_LAST_ROW_TILE = 512
_LAST_SUB_ROWS = (128, 128, 128, 128)
_LAST_FFN_TF = 1024


def _gla_gate_params(w_up, b_up):
    depth = w_up.shape[0]
    wc = jnp.zeros((depth, GLA_HEADS, 128, 2 * GLA_DK), F32)
    for z in range(2):
        w = w_up[:, z].astype(F32).reshape(depth, GLA_RANK, GLA_HEADS, GLA_DK).transpose(0, 2, 1, 3)
        wc = wc.at[:, :, z * GLA_RANK:(z + 1) * GLA_RANK, z * GLA_DK:(z + 1) * GLA_DK].set(w)
    bc = b_up.astype(F32).reshape(depth, 2, GLA_HEADS, GLA_DK).transpose(0, 2, 1, 3)
    return wc.astype(BF16), bc.reshape(depth, GLA_HEADS, 1, 2 * GLA_DK)


def kernel(x, meta, w_in, gla_w_up, gla_b_up, gla_norm_w, na_rel_bias, w_out, ln1_w, ln1_b,
           w_ff1, w_ff2, ln2_w, ln2_b):
    bsz, seq, d = x.shape
    depth = w_in.shape[0]
    alpha = (2 * depth) ** 0.25
    seq_len = seq + N_META
    m = bsz * seq_len
    h = (x.astype(F32), meta.astype(F32))
    hb = _embed(*h, _ROW_TILE)
    assert w_in.shape[2] == _W_IN_COLS and _W_IN_LR % _PROJ_TN == 0 and _PROJ_COLS % _PROJ_TN == 0
    w_in_t = jnp.swapaxes(w_in, 1, 2)
    wc, bc = _gla_gate_params(gla_w_up, gla_b_up)
    nw = gla_norm_w.astype(F32).reshape(depth, 1, GLA_DV)
    table = _na_bias_table(na_rel_bias)
    vec = lambda p: p.astype(F32).reshape(depth, 1, d)
    ln1_w, ln1_b, ln2_w, ln2_b = vec(ln1_w), vec(ln1_b), vec(ln2_w), vec(ln2_b)
    for l in range(depth):
        proj, w_out_b, w_ff1_b, w_ff2_b = _proj(
            hb, w_in_t, l, _PROJ_COLS // _PROJ_TN, _PROJ_TN, 0, _W_IN_LR // _PROJ_TN, 2 * GLA_RANK,
            _PROJ_ROW_TILE, "proj", cast=(w_out, w_ff1, w_ff2))
        proj = proj.reshape(bsz, seq_len, _PROJ_COLS)
        y_gla = _gla(proj, hb.reshape(bsz, seq_len, d), w_in_t, wc, bc, nw, l).reshape(m, GLA_HEADS * GLA_DV)
        y_na = _na(proj, table, l).reshape(m, NA_HEADS * NA_DH)
        h, hb = _out_ln(y_gla, y_na, w_out_b, h, ln1_w, ln1_b, l, alpha, _OUT_ROW_TILE, _SUB_ROWS, seq_len)
        hb3, h3 = hb.reshape(bsz, seq_len, d), h.reshape(bsz, seq_len, d)
        if l + 1 < depth:
            h, hb = _ffn_ln(hb3, w_ff1_b, w_ff2_b, h3, ln2_w, ln2_b, l, alpha, _ROW_TILE, _SUB_ROWS, _FFN_TF, 0)
            h, hb = h.reshape(m, d), hb.reshape(m, d)
        else:
            out, = _ffn_ln(hb3, w_ff1_b, w_ff2_b, h3, ln2_w, ln2_b, l, alpha, _LAST_ROW_TILE, _LAST_SUB_ROWS,
                           _LAST_FFN_TF, N_META)
    return out.astype(x.dtype)
```
